```python
import math
import jax, jax.numpy as jnp
from jax import lax
import numpy as np

D_MODEL = 2048
BATCH = 4
SEQ = 2048
DEPTH = 2
DEC_BATCH = 128
DEC_SEQ = 8
PAST_LEN = 16384
PAGE_SIZE = 128

N_EVEN = (DEPTH + 1) // 2
N_ODD = DEPTH // 2
CHUNK = 128
EPS = 1e-6
M_INIT = -1e30
H_M = 4
DK_M = 256
W_M = H_M * DK_M
CONV_W = 4
H_R = 4
DK_R = 256
W_R = H_R * DK_R
ROPE_BASE = 10000.0
S5_GROUP = 16
W_S = D_MODEL
G_S5 = W_S // S5_GROUP
P_S5 = 64
DT_MIN = 1e-3
DT_MAX = 1e-1
D_FF = ((8 * D_MODEL // 3 + 255) // 256) * 256
PROJ_AB = 4 * W_M + 2 * H_M + 4 * W_R

kernel_name = "hybrid_mlstm_retention_s5_macaron_step"

F32 = jnp.float32


def rms_norm(x, g):
    xf = x.astype(F32)
    return xf * lax.rsqrt(jnp.mean(xf * xf, axis=-1, keepdims=True) + EPS) * g


def head_layer_norm(x, g):
    xf = x.astype(F32)
    xc = xf - jnp.mean(xf, axis=-1, keepdims=True)
    var = jnp.mean(xc * xc, axis=-1, keepdims=True)
    return xc * lax.rsqrt(var + EPS) * g.reshape(x.shape[-2:])


def swiglu(x, w_gate, w_up, w_down):
    return (jax.nn.silu(x @ w_gate) * (x @ w_up)) @ w_down


def causal_dwconv(x, buf, w, b):
    xp = jnp.concatenate([buf.astype(x.dtype), x], axis=1)
    length = x.shape[1]
    y = b
    for j in range(CONV_W):
        y = y + xp[:, j:j + length] * w[j]
    return y, xp[:, -(CONV_W - 1):]


def rope(x, pos):
    half = x.shape[-1] // 2
    freqs = ROPE_BASE ** (-jnp.arange(half, dtype=F32) / half)
    ang = pos[:, None] * freqs[None, :]
    cos = jnp.cos(ang)[None, :, None, :]
    sin = jnp.sin(ang)[None, :, None, :]
    x1, x2 = x[..., :half], x[..., half:]
    return jnp.concatenate([x1 * cos - x2 * sin, x1 * sin + x2 * cos], axis=-1)


def run_chunks(fn, state, seqs):
    bsz, total = seqs[0].shape[0], seqs[0].shape[1]
    length = min(CHUNK, total)
    n_chunks = total // length
    if n_chunks == 1:
        return fn(state, *seqs)
    xs = tuple(jnp.swapaxes(s.reshape((bsz, n_chunks, length) + s.shape[2:]), 0, 1) for s in seqs)

    def body(carry, xc):
        y, new_carry = fn(carry, *xc)
        return new_carry, y

    final, ys = lax.scan(body, state, xs)
    y = jnp.swapaxes(ys, 0, 1).reshape((bsz, total) + ys.shape[3:])
    return y, final


def mlstm_chunk(state, q, k, v, ig, lf):
    c_prev, n_prev, m_prev = state
    length = q.shape[1]
    b = jnp.cumsum(lf, axis=1).transpose(0, 2, 1)
    igt = ig.transpose(0, 2, 1)
    causal = jnp.tril(jnp.ones((length, length), dtype=bool))
    dlog = jnp.where(causal, b[..., :, None] - b[..., None, :] + igt[..., None, :], -jnp.inf)
    s_log = b + m_prev[..., None]
    m_row = jnp.maximum(s_log, jnp.max(dlog, axis=-1))
    w = jnp.exp(dlog - m_row[..., None]) * jnp.einsum('blhd,bshd->bhls', q, k)
    sc = jnp.exp(s_log - m_row)
    num = sc[..., None] * jnp.einsum('blhd,bhde->bhle', q, c_prev) + jnp.einsum('bhls,bshe->bhle', w, v)
    den = sc * jnp.einsum('blhd,bhd->bhl', q, n_prev) + jnp.sum(w, axis=-1)
    h = num / jnp.maximum(jnp.abs(den), jnp.exp(-m_row))[..., None]
    m_new = m_row[..., -1]
    b_last = b[..., -1]
    w_state = jnp.exp(b_last + m_prev - m_new)
    w_k = jnp.exp(b_last[..., None] - b + igt - m_new[..., None])
    c_new = w_state[..., None, None] * c_prev + jnp.einsum('bhs,bshd,bshe->bhde', w_k, k, v)
    n_new = w_state[..., None] * n_prev + jnp.einsum('bhs,bshd->bhd', w_k, k)
    return h.transpose(0, 2, 1, 3), (c_new, n_new, m_new)


def retention_log_gamma():
    return jnp.log1p(-(2.0 ** (-5.0 - jnp.arange(H_R, dtype=F32))))


def retention_chunk(s_prev, q, k, v):
    length = q.shape[1]
    lg = retention_log_gamma()
    idx = jnp.arange(length, dtype=F32)
    diff = idx[:, None] - idx[None, :]
    decay = jnp.where(diff >= 0, jnp.exp(jnp.maximum(diff, 0.0)[None] * lg[:, None, None]), 0.0)
    qk = jnp.einsum('blhd,bshd->bhls', q, k) * decay
    inner = jnp.einsum('bhls,bshe->blhe', qk, v)
    cross = jnp.einsum('blhd,bhde->blhe', q, s_prev) * jnp.exp((idx[:, None] + 1.0) * lg[None, :])[None, :, :, None]
    w_k = jnp.exp((length - 1.0 - idx)[:, None] * lg[None, :])
    s_new = jnp.exp(length * lg)[None, :, None, None] * s_prev + jnp.einsum('lh,blhd,blhe->bhde', w_k, k, v)
    return inner + cross, s_new


def mix_ab(h, pos, conv_buf, c_st, n_st, m_st, s_st, w_in, b_i, b_f, conv_w, conv_b, norm_m, norm_r, w_out):
    bsz, length, _ = h.shape
    z = h @ w_in
    sizes = [2 * W_M, W_M, W_M, H_M, H_M, W_R, W_R, W_R, W_R]
    cuts = np.cumsum(sizes)[:-1].tolist()
    qk_m, v_m, o_m, i_m, f_m, q_r, k_r, v_r, g_r = jnp.split(z, cuts, axis=-1)
    qk_c, new_buf = causal_dwconv(qk_m, conv_buf, conv_w, conv_b)
    qk_c = jax.nn.silu(qk_c)
    q_m = qk_c[..., :W_M].reshape(bsz, length, H_M, DK_M)
    k_m = qk_c[..., W_M:].reshape(bsz, length, H_M, DK_M) * (DK_M ** -0.5)
    v_m = v_m.reshape(bsz, length, H_M, DK_M)
    ig = i_m.astype(F32) + b_i
    lf = jax.nn.log_sigmoid(f_m.astype(F32) + b_f)
    h_m, (c_new, n_new, m_new) = run_chunks(
        mlstm_chunk, (c_st.astype(F32), n_st.astype(F32), m_st.astype(F32)), (q_m, k_m, v_m, ig, lf))
    h_m = head_layer_norm(h_m, norm_m) * jax.nn.sigmoid(o_m.reshape(bsz, length, H_M, DK_M))
    h_m = h_m.reshape(bsz, length, W_M)
    q_r = rope(q_r.reshape(bsz, length, H_R, DK_R), pos)
    k_r = rope(k_r.reshape(bsz, length, H_R, DK_R), pos) * (DK_R ** -0.5)
    v_r = v_r.reshape(bsz, length, H_R, DK_R)
    y_r, s_new = run_chunks(retention_chunk, s_st.astype(F32), (q_r, k_r, v_r))
    y_r = head_layer_norm(y_r, norm_r).reshape(bsz, length, W_R) * jax.nn.silu(g_r)
    out = jnp.concatenate([h_m, y_r], axis=-1) @ w_out
    return out, new_buf, c_new, n_new, m_new, s_new


def complex_affine_combine(e1, e2):
    a1r, a1i, b1r, b1i = e1
    a2r, a2i, b2r, b2i = e2
    return (a1r * a2r - a1i * a2i,
            a1r * a2i + a1i * a2r,
            a2r * b1r - a2i * b1i + b2r,
            a2r * b1i + a2i * b1r + b2i)


def mix_s5(h, x0_re, x0_im, w_in, lam_re, lam_im, log_dt, b_re, b_im, c_re, c_im, d_skip, w_glu, b_glu):
    bsz, length, _ = h.shape
    u = h @ w_in
    ug = u.reshape(bsz, length, G_S5, S5_GROUP).astype(F32)
    dt = jnp.exp(log_dt.astype(F32))[:, None]
    lr = lam_re.astype(F32)
    li = lam_im.astype(F32)
    mag = jnp.exp(lr * dt)
    a_re = mag * jnp.cos(li * dt)
    a_im = mag * jnp.sin(li * dt)
    denom = lr * lr + li * li
    g_re = ((a_re - 1.0) * lr + a_im * li) / denom
    g_im = (a_im * lr - (a_re - 1.0) * li) / denom
    bb_re = g_re[..., None] * b_re - g_im[..., None] * b_im
    bb_im = g_re[..., None] * b_im + g_im[..., None] * b_re

    def chunk(state, uc):
        xr0, xi0 = state
        lc = uc.shape[1]
        bu_re = jnp.einsum('blgc,gpc->blgp', uc, bb_re)
        bu_im = jnp.einsum('blgc,gpc->blgp', uc, bb_im)
        bu_re = bu_re.at[:, 0].add(a_re * xr0 - a_im * xi0)
        bu_im = bu_im.at[:, 0].add(a_re * xi0 + a_im * xr0)
        ar = jnp.broadcast_to(a_re, bu_re.shape)
        ai = jnp.broadcast_to(a_im, bu_im.shape)
        _, _, xr, xi = lax.associative_scan(complex_affine_combine, (ar, ai, bu_re, bu_im), axis=1)
        y = jnp.einsum('blgp,gcp->blgc', xr, c_re) - jnp.einsum('blgp,gcp->blgc', xi, c_im)
        return y, (xr[:, -1], xi[:, -1])

    y, (xr_new, xi_new) = run_chunks(chunk, (x0_re.astype(F32), x0_im.astype(F32)), (ug,))
    y = y.reshape(bsz, length, W_S) + d_skip * u
    zz = jax.nn.gelu(y) @ w_glu + b_glu
    out = zz[..., :D_MODEL] * jax.nn.sigmoid(zz[..., D_MODEL:])
    return out, xr_new, xi_new


def trunk(x, pos, m_c, m_n, m_m, m_conv, r_s, s5_re, s5_im, p):
    y = x
    out_mc, out_mn, out_mm, out_conv, out_rs, out_re, out_im = [], [], [], [], [], [], []
    for layer in range(DEPTH):
        y = y + 0.5 * swiglu(rms_norm(y, p['norm_ffn1'][layer]), p['ffn1_w_gate'][layer],
                             p['ffn1_w_up'][layer], p['ffn1_w_down'][layer])
        hn = rms_norm(y, p['norm_mix'][layer])
        if layer % 2 == 0:
            e = layer // 2
            out, buf, c_new, n_new, mm_new, s_new = mix_ab(
                hn, pos, m_conv[e], m_c[e], m_n[e], m_m[e], r_s[e], p['ab_w_in'][e], p['mlstm_b_i'][e],
                p['mlstm_b_f'][e], p['mlstm_conv_w'][e], p['mlstm_conv_b'][e], p['mlstm_norm'][e],
                p['ret_norm'][e], p['ab_w_out'][e])
            out_mc.append(c_new)
            out_mn.append(n_new)
            out_mm.append(mm_new)
            out_conv.append(buf)
            out_rs.append(s_new)
        else:
            o = layer // 2
            out, xr, xi = mix_s5(
                hn, s5_re[o], s5_im[o], p['s5_w_in'][o], p['s5_lambda_re'][o], p['s5_lambda_im'][o],
                p['s5_log_dt'][o], p['s5_B_re'][o], p['s5_B_im'][o], p['s5_C_re'][o], p['s5_C_im'][o],
                p['s5_D'][o], p['s5_w_glu'][o], p['s5_b_glu'][o])
            out_re.append(xr)
            out_im.append(xi)
        y = y + out
        y = y + 0.5 * swiglu(rms_norm(y, p['norm_ffn2'][layer]), p['ffn2_w_gate'][layer],
                             p['ffn2_w_up'][layer], p['ffn2_w_down'][layer])
    y = rms_norm(y, p['norm_final'])
    return (y, jnp.stack(out_mc), jnp.stack(out_mn), jnp.stack(out_mm), jnp.stack(out_conv),
            jnp.stack(out_rs), jnp.stack(out_re), jnp.stack(out_im))


def setup_inputs(seed: int = 0) -> dict:
    key = jax.random.key(seed)
    ks = iter(jax.random.split(key, 64))

    def nrm(shape, scale):
        return scale * jax.random.normal(next(ks), shape, F32)

    inp = {}
    inp['x_prompt'] = nrm((BATCH, SEQ, D_MODEL), 1.0)
    inp['x_sample'] = nrm((DEC_BATCH, DEC_SEQ, D_MODEL), 1.0)
    inp['state_mlstm_C'] = nrm((N_EVEN, DEC_BATCH, H_M, DK_M, DK_M), 0.5)
    inp['state_mlstm_n'] = nrm((N_EVEN, DEC_BATCH, H_M, DK_M), 0.5)
    inp['state_mlstm_m'] = nrm((N_EVEN, DEC_BATCH, H_M), 1.0)
    inp['state_mlstm_conv'] = nrm((N_EVEN, DEC_BATCH, CONV_W - 1, 2 * W_M), 1.0)
    inp['state_ret_S'] = nrm((N_EVEN, DEC_BATCH, H_R, DK_R, DK_R), 0.05)
    inp['state_s5_re'] = nrm((N_ODD, DEC_BATCH, G_S5, P_S5), 0.5)
    inp['state_s5_im'] = nrm((N_ODD, DEC_BATCH, G_S5, P_S5), 0.5)
    inp['norm_ffn1'] = 1.0 + nrm((DEPTH, D_MODEL), 0.02)
    inp['norm_mix'] = 1.0 + nrm((DEPTH, D_MODEL), 0.02)
    inp['norm_ffn2'] = 1.0 + nrm((DEPTH, D_MODEL), 0.02)
    inp['norm_final'] = 1.0 + nrm((D_MODEL,), 0.02)
    inp['ffn1_w_gate'] = nrm((DEPTH, D_MODEL, D_FF), D_MODEL ** -0.5)
    inp['ffn1_w_up'] = nrm((DEPTH, D_MODEL, D_FF), D_MODEL ** -0.5)
    inp['ffn1_w_down'] = nrm((DEPTH, D_FF, D_MODEL), D_FF ** -0.5)
    inp['ffn2_w_gate'] = nrm((DEPTH, D_MODEL, D_FF), D_MODEL ** -0.5)
    inp['ffn2_w_up'] = nrm((DEPTH, D_MODEL, D_FF), D_MODEL ** -0.5)
    inp['ffn2_w_down'] = nrm((DEPTH, D_FF, D_MODEL), D_FF ** -0.5)
    inp['ab_w_in'] = nrm((N_EVEN, D_MODEL, PROJ_AB), D_MODEL ** -0.5)
    inp['mlstm_b_i'] = nrm((N_EVEN, H_M), 0.1)
    inp['mlstm_b_f'] = jnp.linspace(3.0, 6.0, H_M, dtype=F32)[None, :] + nrm((N_EVEN, H_M), 0.1)
    inp['mlstm_conv_w'] = nrm((N_EVEN, CONV_W, 2 * W_M), CONV_W ** -0.5)
    inp['mlstm_conv_b'] = nrm((N_EVEN, 2 * W_M), 0.01)
    inp['mlstm_norm'] = 1.0 + nrm((N_EVEN, W_M), 0.02)
    inp['ret_norm'] = 1.0 + nrm((N_EVEN, W_R), 0.02)
    inp['ab_w_out'] = nrm((N_EVEN, W_M + W_R, D_MODEL), (W_M + W_R) ** -0.5)
    inp['s5_w_in'] = nrm((N_ODD, D_MODEL, W_S), D_MODEL ** -0.5)
    inp['s5_lambda_re'] = -0.5 + nrm((N_ODD, G_S5, P_S5), 0.01)
    inp['s5_lambda_im'] = jnp.pi * jnp.arange(P_S5, dtype=F32) + nrm((N_ODD, G_S5, P_S5), 0.01)
    inp['s5_log_dt'] = jax.random.uniform(next(ks), (N_ODD, G_S5), F32,
                                          minval=math.log(DT_MIN), maxval=math.log(DT_MAX))
    inp['s5_B_re'] = nrm((N_ODD, G_S5, P_S5, S5_GROUP), (2 * S5_GROUP) ** -0.5)
    inp['s5_B_im'] = nrm((N_ODD, G_S5, P_S5, S5_GROUP), (2 * S5_GROUP) ** -0.5)
    inp['s5_C_re'] = nrm((N_ODD, G_S5, S5_GROUP, P_S5), P_S5 ** -0.5)
    inp['s5_C_im'] = nrm((N_ODD, G_S5, S5_GROUP, P_S5), P_S5 ** -0.5)
    inp['s5_D'] = nrm((N_ODD, W_S), 1.0)
    inp['s5_w_glu'] = nrm((N_ODD, W_S, 2 * D_MODEL), W_S ** -0.5)
    inp['s5_b_glu'] = nrm((N_ODD, 2 * D_MODEL), 0.01)
    return inp


def reference(x_prompt, x_sample, state_mlstm_C, state_mlstm_n, state_mlstm_m, state_mlstm_conv,
              state_ret_S, state_s5_re, state_s5_im, norm_ffn1, norm_mix, norm_ffn2, norm_final,
              ffn1_w_gate, ffn1_w_up, ffn1_w_down, ffn2_w_gate, ffn2_w_up, ffn2_w_down,
              ab_w_in, mlstm_b_i, mlstm_b_f, mlstm_conv_w, mlstm_conv_b, mlstm_norm, ret_norm, ab_w_out,
              s5_w_in, s5_lambda_re, s5_lambda_im, s5_log_dt, s5_B_re, s5_B_im, s5_C_re, s5_C_im,
              s5_D, s5_w_glu, s5_b_glu):
    p = {
        'norm_ffn1': norm_ffn1, 'norm_mix': norm_mix, 'norm_ffn2': norm_ffn2, 'norm_final': norm_final,
        'ffn1_w_gate': ffn1_w_gate, 'ffn1_w_up': ffn1_w_up, 'ffn1_w_down': ffn1_w_down,
        'ffn2_w_gate': ffn2_w_gate, 'ffn2_w_up': ffn2_w_up, 'ffn2_w_down': ffn2_w_down,
        'ab_w_in': ab_w_in, 'mlstm_b_i': mlstm_b_i, 'mlstm_b_f': mlstm_b_f,
        'mlstm_conv_w': mlstm_conv_w, 'mlstm_conv_b': mlstm_conv_b, 'mlstm_norm': mlstm_norm,
        'ret_norm': ret_norm, 'ab_w_out': ab_w_out,
        's5_w_in': s5_w_in, 's5_lambda_re': s5_lambda_re, 's5_lambda_im': s5_lambda_im,
        's5_log_dt': s5_log_dt, 's5_B_re': s5_B_re, 's5_B_im': s5_B_im, 's5_C_re': s5_C_re,
        's5_C_im': s5_C_im, 's5_D': s5_D, 's5_w_glu': s5_w_glu, 's5_b_glu': s5_b_glu,
    }
    pos_prompt = jnp.arange(SEQ, dtype=F32)
    p0_c = jnp.zeros((N_EVEN, BATCH, H_M, DK_M, DK_M), F32)
    p0_n = jnp.zeros((N_EVEN, BATCH, H_M, DK_M), F32)
    p0_m = jnp.full((N_EVEN, BATCH, H_M), M_INIT, F32)
    p0_conv = jnp.zeros((N_EVEN, BATCH, CONV_W - 1, 2 * W_M), x_prompt.dtype)
    p0_s = jnp.zeros((N_EVEN, BATCH, H_R, DK_R, DK_R), F32)
    p0_re = jnp.zeros((N_ODD, BATCH, G_S5, P_S5), F32)
    p0_im = jnp.zeros((N_ODD, BATCH, G_S5, P_S5), F32)
    y_prompt, pc, pn, pm, pconv, ps, pre, pim = trunk(
        x_prompt, pos_prompt, p0_c, p0_n, p0_m, p0_conv, p0_s, p0_re, p0_im, p)
    pos_sample = PAST_LEN + jnp.arange(DEC_SEQ, dtype=F32)
    y_sample, sc, sn, sm, sconv, ss, sre, sim = trunk(
        x_sample, pos_sample, state_mlstm_C, state_mlstm_n, state_mlstm_m, state_mlstm_conv,
        state_ret_S, state_s5_re, state_s5_im, p)
    return (y_prompt, y_sample, pc, sc, pn, sn, pm, sm, pconv, sconv, ps, ss, pre, sre, pim, sim)
```

```python
import functools
import math

import jax
import jax.numpy as jnp
from jax import lax
from jax.experimental import pallas as pl
from jax.experimental.pallas import tpu as pltpu

F32 = jnp.float32
BF16 = jnp.bfloat16

EPS = 1e-6
M_INIT = -1e30
CHUNK = 128
CONV_W = 4
ROPE_BASE = 10000.0
N_HEADS = 4
D_HEAD = 256
W_HEADS = N_HEADS * D_HEAD
S5_GROUP = 16
S5_STATES = 64
GROUPS_PER_BLOCK = 16
LANES = 128
GATE_COLS = LANES

VMEM_LIMIT_BYTES = 56 * 1024 * 1024


def _params(*semantics):
    return pltpu.CompilerParams(dimension_semantics=semantics, vmem_limit_bytes=VMEM_LIMIT_BYTES)


def _rms_norm(x, g):
    return x * lax.rsqrt(jnp.mean(x * x, axis=-1, keepdims=True) + EPS) * g


def _ffn_kernel(x_ref, g_ref, wg_ref, wu_ref, wd_ref, o_ref, xn_ref):
    f = pl.program_id(1)

    @pl.when(f == 0)
    def _():
        xn_ref[...] = _rms_norm(x_ref[...], g_ref[...]).astype(BF16)
        o_ref[...] = jnp.zeros_like(o_ref)

    xn = xn_ref[...]
    a = jnp.dot(xn, wg_ref[...], preferred_element_type=F32)
    b = jnp.dot(xn, wu_ref[...], preferred_element_type=F32)
    h = (a * jax.nn.sigmoid(a) * b).astype(BF16)
    o_ref[...] += jnp.dot(h, wd_ref[...], preferred_element_type=F32)

    @pl.when(f == pl.num_programs(1) - 1)
    def _():
        o_ref[...] = x_ref[...] + 0.5 * o_ref[...]


def _ffn(x, g, w_gate, w_up, w_down, *, tm, tf):
    m, d = x.shape
    dff = w_gate.shape[1]
    return pl.pallas_call(
        _ffn_kernel,
        grid=(m // tm, dff // tf),
        in_specs=[
            pl.BlockSpec((tm, d), lambda i, f: (i, 0)),
            pl.BlockSpec((1, d), lambda i, f: (0, 0)),
            pl.BlockSpec((d, tf), lambda i, f: (0, f)),
            pl.BlockSpec((d, tf), lambda i, f: (0, f)),
            pl.BlockSpec((tf, d), lambda i, f: (f, 0)),
        ],
        out_specs=pl.BlockSpec((tm, d), lambda i, f: (i, 0)),
        out_shape=jax.ShapeDtypeStruct((m, d), F32),
        scratch_shapes=[pltpu.VMEM((tm, d), BF16)],
        compiler_params=_params("parallel", "arbitrary"),
        name="ffn",
    )(x, g.reshape(1, d), w_gate, w_up, w_down)


def _norm_matmul_kernel(x_ref, g_ref, w_ref, o_ref, xn_ref):
    @pl.when(pl.program_id(1) == 0)
    def _():
        xn_ref[...] = _rms_norm(x_ref[...], g_ref[...]).astype(BF16)

    o_ref[...] = jnp.dot(xn_ref[...], w_ref[...], preferred_element_type=F32)


def _norm_matmul(x, g, w, *, tm, tn):
    m, d = x.shape
    n = w.shape[1]
    return pl.pallas_call(
        _norm_matmul_kernel,
        grid=(m // tm, n // tn),
        in_specs=[
            pl.BlockSpec((tm, d), lambda i, j: (i, 0)),
            pl.BlockSpec((1, d), lambda i, j: (0, 0)),
            pl.BlockSpec((d, tn), lambda i, j: (0, j)),
        ],
        out_specs=pl.BlockSpec((tm, tn), lambda i, j: (i, j)),
        out_shape=jax.ShapeDtypeStruct((m, n), F32),
        scratch_shapes=[pltpu.VMEM((tm, d), BF16)],
        compiler_params=_params("parallel", "arbitrary"),
        name="norm_matmul",
    )(x, g.reshape(1, d), w)


def _matmul_res_kernel(a_ref, w_ref, r_ref, o_ref, ab_ref):
    @pl.when(pl.program_id(1) == 0)
    def _():
        ab_ref[...] = a_ref[...].astype(BF16)

    o_ref[...] = r_ref[...] + jnp.dot(ab_ref[...], w_ref[...], preferred_element_type=F32)


def _matmul_res(a, w, res, *, tm, tn):
    m, k = a.shape
    n = w.shape[1]
    return pl.pallas_call(
        _matmul_res_kernel,
        grid=(m // tm, n // tn),
        in_specs=[
            pl.BlockSpec((tm, k), lambda i, j: (i, 0)),
            pl.BlockSpec((k, tn), lambda i, j: (0, j)),
            pl.BlockSpec((tm, tn), lambda i, j: (i, j)),
        ],
        out_specs=pl.BlockSpec((tm, tn), lambda i, j: (i, j)),
        out_shape=jax.ShapeDtypeStruct((m, n), F32),
        scratch_shapes=[pltpu.VMEM((tm, k), BF16)],
        compiler_params=_params("parallel", "arbitrary"),
        name="matmul_res",
    )(a, w, res)


def _glu_kernel(a_ref, wv_ref, wg_ref, bv_ref, bg_ref, r_ref, o_ref):
    a = a_ref[...]
    v = jnp.dot(a, wv_ref[...], preferred_element_type=F32) + bv_ref[...]
    t = jnp.dot(a, wg_ref[...], preferred_element_type=F32) + bg_ref[...]
    o_ref[...] = r_ref[...] + v * jax.nn.sigmoid(t)


def _glu(a, w, b, res, *, tm, tn):
    m, k = a.shape
    n = w.shape[1] // 2
    nj = n // tn
    b2 = b.reshape(1, 2 * n)
    return pl.pallas_call(
        _glu_kernel,
        grid=(m // tm, nj),
        in_specs=[
            pl.BlockSpec((tm, k), lambda i, j: (i, 0)),
            pl.BlockSpec((k, tn), lambda i, j: (0, j)),
            pl.BlockSpec((k, tn), lambda i, j: (0, j + nj)),
            pl.BlockSpec((1, tn), lambda i, j: (0, j)),
            pl.BlockSpec((1, tn), lambda i, j: (0, j + nj)),
            pl.BlockSpec((tm, tn), lambda i, j: (i, j)),
        ],
        out_specs=pl.BlockSpec((tm, tn), lambda i, j: (i, j)),
        out_shape=jax.ShapeDtypeStruct((m, n), F32),
        compiler_params=_params("parallel", "arbitrary"),
        name="glu",
    )(a, w, w, b2, b2, res)


def _final_norm_kernel(x_ref, g_ref, o_ref):
    o_ref[...] = _rms_norm(x_ref[...], g_ref[...])


def _final_norm(x, g, *, tm):
    m, d = x.shape
    return pl.pallas_call(
        _final_norm_kernel,
        grid=(m // tm,),
        in_specs=[pl.BlockSpec((tm, d), lambda i: (i, 0)), pl.BlockSpec((1, d), lambda i: (0, 0))],
        out_specs=pl.BlockSpec((tm, d), lambda i: (i, 0)),
        out_shape=jax.ShapeDtypeStruct((m, d), F32),
        compiler_params=_params("parallel"),
        name="final_norm",
    )(x, g.reshape(1, d))


def _head_layer_norm(x, g):
    xc = x - jnp.mean(x, axis=-1, keepdims=True)
    var = jnp.mean(xc * xc, axis=-1, keepdims=True)
    return xc * lax.rsqrt(var + EPS) * g


def _dot_nt(a, b):
    return lax.dot_general(a, b, (((1,), (1,)), ((), ())), preferred_element_type=F32)


def _dot_tn(a, b):
    return lax.dot_general(a, b, (((0,), (0,)), ((), ())), preferred_element_type=F32)


def _ab_kernel(z_ref, cw_ref, cb_ref, gb_ref, nm_ref, nr_ref, cos_ref, sin_ref, dec_ref, rtab_ref, sdec_ref,
               c0_ref, n0_ref, m0_ref, conv0_ref, s0_ref,
               h_ref, c_ref, n_ref, m_ref, conv_ref, s_ref,
               xp_ref, *, lc):
    chunk = pl.program_id(1)
    pad = 8

    @pl.when(chunk == 0)
    def _():
        c_ref[...] = c0_ref[...]
        n_ref[...] = n0_ref[...]
        m_ref[...] = m0_ref[...]
        s_ref[...] = s0_ref[...]
        xp_ref[pad - (CONV_W - 1):pad, :] = conv0_ref[0]

    xp_ref[pad:pad + lc, :] = z_ref[:, 0:2 * W_HEADS]
    qk = cb_ref[...]
    for j in range(CONV_W):
        qk = qk + xp_ref[pad - (CONV_W - 1) + j:pad - (CONV_W - 1) + j + lc, :] * cw_ref[j:j + 1, :]
    new_buf = xp_ref[pad + lc - (CONV_W - 1):pad + lc, :]
    xp_ref[pad - (CONV_W - 1):pad, :] = new_buf
    conv_ref[0] = new_buf
    qk = qk * jax.nn.sigmoid(qk)

    row = lax.broadcasted_iota(jnp.int32, (lc, lc), 0)
    col = lax.broadcasted_iota(jnp.int32, (lc, lc), 1)
    causal = row >= col
    eye = row == col

    def row_of(c):
        return jnp.sum(jnp.where(eye, c, 0.0), axis=0, keepdims=True)

    gates = z_ref[:, 8 * W_HEADS:8 * W_HEADS + GATE_COLS] + gb_ref[...]
    lf = jax.nn.log_sigmoid(gates)
    bcum = jnp.dot(causal.astype(F32), lf, preferred_element_type=F32, precision=lax.Precision.HIGHEST)

    scale = D_HEAD ** -0.5
    cos = cos_ref[...]
    sin = sin_ref[...]

    def rope(x):
        x1 = x[:, :D_HEAD // 2]
        x2 = x[:, D_HEAD // 2:]
        return jnp.concatenate([x1 * cos - x2 * sin, x1 * sin + x2 * cos], axis=-1)

    for h in range(N_HEADS):
        lo, hi = h * D_HEAD, (h + 1) * D_HEAD
        q = qk[:, lo:hi]
        k = qk[:, W_HEADS + lo:W_HEADS + hi] * scale
        v = z_ref[:, 2 * W_HEADS + lo:2 * W_HEADS + hi]
        og = z_ref[:, 3 * W_HEADS + lo:3 * W_HEADS + hi]
        qb, kb, vb = q.astype(BF16), k.astype(BF16), v.astype(BF16)
        c_prev = c_ref[0, h]
        n_prev = n_ref[0, h:h + 1, :]
        m_prev = m_ref[0, :, h:h + 1]
        ig_c = gates[:, h:h + 1]
        b_c = bcum[:, N_HEADS + h:N_HEADS + h + 1]
        dlog = jnp.where(causal, b_c - row_of(b_c) + row_of(ig_c), -jnp.inf)
        s_log = b_c + m_prev
        m_row = jnp.maximum(s_log, jnp.max(dlog, axis=-1, keepdims=True))
        w = jnp.exp(dlog - m_row) * _dot_nt(qb, kb)
        sc = jnp.exp(s_log - m_row)
        num = sc * jnp.dot(qb, c_prev.astype(BF16), preferred_element_type=F32) + jnp.dot(
            w.astype(BF16), vb, preferred_element_type=F32)
        den = sc * jnp.sum(q * n_prev, axis=-1, keepdims=True) + jnp.sum(w, axis=-1, keepdims=True)
        hm = num / jnp.maximum(jnp.abs(den), jnp.exp(-m_row))
        m_new = m_row[lc - 1:lc, :]
        b_last = b_c[lc - 1:lc, :]
        w_state = jnp.exp(b_last + m_prev - m_new)
        kw = k * jnp.exp(b_last - b_c + ig_c - m_new)
        c_ref[0, h] = w_state * c_prev + _dot_tn(kw.astype(BF16), vb)
        n_ref[0, h:h + 1, :] = w_state * n_prev + jnp.sum(kw, axis=0, keepdims=True)
        m_ref[0, :, h:h + 1] = m_new
        hm = _head_layer_norm(hm, nm_ref[:, lo:hi]) * jax.nn.sigmoid(og)
        h_ref[:, lo:hi] = hm
        qr = rope(z_ref[:, 4 * W_HEADS + lo:4 * W_HEADS + hi])
        kr = rope(z_ref[:, 5 * W_HEADS + lo:5 * W_HEADS + hi]) * scale
        vr = z_ref[:, 6 * W_HEADS + lo:6 * W_HEADS + hi]
        gr = z_ref[:, 7 * W_HEADS + lo:7 * W_HEADS + hi]
        qrb, krb, vrb = qr.astype(BF16), kr.astype(BF16), vr.astype(BF16)
        s_prev = s_ref[0, h]
        att = _dot_nt(qrb, krb) * dec_ref[h]
        inner = jnp.dot(att.astype(BF16), vrb, preferred_element_type=F32)
        cross = jnp.dot(qrb, s_prev.astype(BF16), preferred_element_type=F32) * rtab_ref[:, h:h + 1]
        krw = kr * rtab_ref[:, N_HEADS + h:N_HEADS + h + 1]
        s_ref[0, h] = sdec_ref[0:1, h:h + 1] * s_prev + _dot_tn(krw.astype(BF16), vrb)
        yr = _head_layer_norm(inner + cross, nr_ref[:, lo:hi]) * (gr * jax.nn.sigmoid(gr))
        h_ref[:, W_HEADS + lo:W_HEADS + hi] = yr


def _retention_tables(lc):
    lg = jnp.log1p(-(2.0 ** (-5.0 - jnp.arange(N_HEADS, dtype=F32))))
    idx = jnp.arange(lc, dtype=F32)
    diff = idx[:, None] - idx[None, :]
    decay = jnp.where(diff >= 0, jnp.exp(jnp.maximum(diff, 0.0)[None] * lg[:, None, None]), 0.0)
    cross = jnp.exp((idx[:, None] + 1.0) * lg[None, :])
    w_k = jnp.exp((lc - 1.0 - idx)[:, None] * lg[None, :])
    rtab = jnp.zeros((lc, GATE_COLS), F32).at[:, :N_HEADS].set(cross).at[:, N_HEADS:2 * N_HEADS].set(w_k)
    sdec = jnp.zeros((8, GATE_COLS), F32).at[0, :N_HEADS].set(jnp.exp(lc * lg))
    return decay, rtab, sdec


def _rope_tables(pos):
    half = D_HEAD // 2
    freqs = ROPE_BASE ** (-jnp.arange(half, dtype=F32) / half)
    ang = pos[:, None] * freqs[None, :]
    return jnp.cos(ang), jnp.sin(ang)


def _mix_ab(z, pos, conv0, c0, n0, m0, s0, conv_w, conv_b, gate_bias, norm_m, norm_r, *, bsz, length, row0):
    lc = min(CHUNK, length)
    nc = length // lc
    blk0 = row0 // lc
    zw = z.shape[1]
    cos, sin = _rope_tables(pos)
    decay, rtab, sdec = _retention_tables(lc)
    full = lambda *shape: pl.BlockSpec(shape, lambda b, c: (0,) * len(shape))
    state4 = pl.BlockSpec((1, N_HEADS, D_HEAD, D_HEAD), lambda b, c: (b, 0, 0, 0))
    state_n = pl.BlockSpec((1, N_HEADS, D_HEAD), lambda b, c: (b, 0, 0))
    state_m = pl.BlockSpec((1, 1, N_HEADS), lambda b, c: (b, 0, 0))
    state_conv = pl.BlockSpec((1, CONV_W - 1, 2 * W_HEADS), lambda b, c: (b, 0, 0))
    outs = pl.pallas_call(
        functools.partial(_ab_kernel, lc=lc),
        grid=(bsz, nc),
        in_specs=[
            pl.BlockSpec((lc, zw), lambda b, c: (blk0 + b * nc + c, 0)),
            full(CONV_W, 2 * W_HEADS), full(1, 2 * W_HEADS), full(1, GATE_COLS),
            full(1, W_HEADS), full(1, W_HEADS),
            pl.BlockSpec((lc, D_HEAD // 2), lambda b, c: (c, 0)),
            pl.BlockSpec((lc, D_HEAD // 2), lambda b, c: (c, 0)),
            full(N_HEADS, lc, lc), full(lc, GATE_COLS), full(8, GATE_COLS),
            state4, state_n, state_m, state_conv, state4,
        ],
        out_specs=[
            pl.BlockSpec((lc, 2 * W_HEADS), lambda b, c: (b * nc + c, 0)),
            state4, state_n, state_m, state_conv, state4,
        ],
        out_shape=[
            jax.ShapeDtypeStruct((bsz * length, 2 * W_HEADS), F32),
            jax.ShapeDtypeStruct((bsz, N_HEADS, D_HEAD, D_HEAD), F32),
            jax.ShapeDtypeStruct((bsz, N_HEADS, D_HEAD), F32),
            jax.ShapeDtypeStruct((bsz, 1, N_HEADS), F32),
            jax.ShapeDtypeStruct((bsz, CONV_W - 1, 2 * W_HEADS), F32),
            jax.ShapeDtypeStruct((bsz, N_HEADS, D_HEAD, D_HEAD), F32),
        ],
        scratch_shapes=[pltpu.VMEM((lc + 8, 2 * W_HEADS), F32)],
        compiler_params=_params("parallel", "arbitrary"),
        name="mix_ab",
    )(z, conv_w, conv_b.reshape(1, -1), gate_bias, norm_m.reshape(1, -1), norm_r.reshape(1, -1),
      cos, sin, decay, rtab, sdec, c0, n0, m0.reshape(bsz, 1, N_HEADS), conv0, s0)
    h, c_new, n_new, m_new, conv_new, s_new = outs
    return h, conv_new, c_new, n_new, m_new.reshape(bsz, N_HEADS), s_new


def _s5_prep_kernel(lr_ref, li_ref, ldt_ref, bre_ref, bim_ref, expand_ref, are_ref, aim_ref, bbre_ref, bbim_ref):
    lr = lr_ref[...]
    li = li_ref[...]
    dt = jnp.exp(ldt_ref[...])
    mag = jnp.exp(lr * dt)
    a_re = mag * jnp.cos(li * dt)
    a_im = mag * jnp.sin(li * dt)
    denom = lr * lr + li * li
    g_re = ((a_re - 1.0) * lr + a_im * li) / denom
    g_im = (a_im * lr - (a_re - 1.0) * li) / denom
    are_ref[...] = a_re
    aim_ref[...] = a_im
    ge_re = jnp.dot(g_re, expand_ref[...], preferred_element_type=F32, precision=lax.Precision.HIGHEST)
    ge_im = jnp.dot(g_im, expand_ref[...], preferred_element_type=F32, precision=lax.Precision.HIGHEST)
    bre = bre_ref[...]
    bim = bim_ref[...]
    bbre_ref[...] = ge_re * bre - ge_im * bim
    bbim_ref[...] = ge_re * bim + ge_im * bre


def _s5_prep(lam_re, lam_im, log_dt, b_re, b_im):
    g, p = lam_re.shape
    expand = jnp.repeat(jnp.eye(p, dtype=F32), S5_GROUP, axis=1)
    return pl.pallas_call(
        _s5_prep_kernel,
        out_shape=[
            jax.ShapeDtypeStruct((g, p), F32), jax.ShapeDtypeStruct((g, p), F32),
            jax.ShapeDtypeStruct((g, p * S5_GROUP), F32), jax.ShapeDtypeStruct((g, p * S5_GROUP), F32),
        ],
        compiler_params=pltpu.CompilerParams(vmem_limit_bytes=VMEM_LIMIT_BYTES),
        name="s5_prep",
    )(lam_re, lam_im, log_dt.reshape(g, 1), b_re.reshape(g, p * S5_GROUP), b_im.reshape(g, p * S5_GROUP), expand)


def _s5_kernel(u_ref, bb_ref, cc_ref, a_ref, d_ref, x0_ref, y_ref, xn_ref, bu_ref, *, nb, tt):
    t_idx = pl.program_id(1)
    half = GROUPS_PER_BLOCK * S5_STATES

    @pl.when(t_idx == 0)
    def _():
        xn_ref[...] = x0_ref[...]

    nl = half // LANES
    u = u_ref[...].reshape(nb * tt, GROUPS_PER_BLOCK * S5_GROUP)
    bu = jnp.dot(u.astype(BF16), bb_ref[0], preferred_element_type=F32)
    for j in range(2 * nl):
        bu_ref[j] = bu[:, j * LANES:(j + 1) * LANES]

    sub = min(nb, 8)
    a_re = [jnp.broadcast_to(a_ref[0, 0:1, j * LANES:(j + 1) * LANES], (sub, LANES)) for j in range(nl)]
    a_im = [jnp.broadcast_to(a_ref[0, 1:2, j * LANES:(j + 1) * LANES], (sub, LANES)) for j in range(nl)]
    for s in range(nb // sub):
        def step(t, carry):
            rows = pl.ds(s * sub * tt + t, sub, stride=tt)
            new = []
            for j in range(nl):
                xr, xi = carry[j]
                xr_new = a_re[j] * xr - a_im[j] * xi + bu_ref[j, rows, :]
                xi_new = a_re[j] * xi + a_im[j] * xr + bu_ref[nl + j, rows, :]
                bu_ref[j, rows, :] = xr_new
                bu_ref[nl + j, rows, :] = xi_new
                new.append((xr_new, xi_new))
            return tuple(new)

        brows = slice(s * sub, (s + 1) * sub)
        init = tuple((xn_ref[brows, j * LANES:(j + 1) * LANES], xn_ref[brows, half + j * LANES:half + (j + 1) * LANES])
                     for j in range(nl))
        last = lax.fori_loop(0, tt, step, init)
        for j in range(nl):
            xn_ref[brows, j * LANES:(j + 1) * LANES] = last[j][0]
            xn_ref[brows, half + j * LANES:half + (j + 1) * LANES] = last[j][1]

    x = jnp.concatenate([bu_ref[j] for j in range(2 * nl)], axis=-1)
    y = jnp.dot(x.astype(BF16), cc_ref[0], preferred_element_type=F32) + d_ref[...] * u
    y_ref[...] = jax.nn.gelu(y).astype(BF16).reshape(nb, tt, GROUPS_PER_BLOCK * S5_GROUP)


def _mix_s5(u, x0_re, x0_im, bb_blk, cc_blk, a_blk, d_skip, *, bsz, length, nb, tt):
    w = u.shape[-1]
    nblk = w // (GROUPS_PER_BLOCK * S5_GROUP)
    half = GROUPS_PER_BLOCK * S5_STATES
    x0 = jnp.concatenate([x0_re.reshape(bsz, nblk, half), x0_im.reshape(bsz, nblk, half)], axis=-1)
    x0 = x0.transpose(1, 0, 2)
    cw = GROUPS_PER_BLOCK * S5_GROUP
    y, xn = pl.pallas_call(
        functools.partial(_s5_kernel, nb=nb, tt=tt),
        grid=(nblk, length // tt),
        in_specs=[
            pl.BlockSpec((nb, tt, cw), lambda g, t: (0, t, g)),
            pl.BlockSpec((1, cw, 2 * half), lambda g, t: (g, 0, 0)),
            pl.BlockSpec((1, 2 * half, cw), lambda g, t: (g, 0, 0)),
            pl.BlockSpec((1, 2, half), lambda g, t: (g, 0, 0)),
            pl.BlockSpec((1, cw), lambda g, t: (0, g)),
            pl.BlockSpec((None, nb, 2 * half), lambda g, t: (g, 0, 0)),
        ],
        out_specs=[
            pl.BlockSpec((nb, tt, cw), lambda g, t: (0, t, g)),
            pl.BlockSpec((None, nb, 2 * half), lambda g, t: (g, 0, 0)),
        ],
        out_shape=[
            jax.ShapeDtypeStruct((bsz, length, w), BF16),
            jax.ShapeDtypeStruct((nblk, bsz, 2 * half), F32),
        ],
        scratch_shapes=[pltpu.VMEM((2 * half // LANES, nb * tt, LANES), F32)],
        compiler_params=_params("parallel", "arbitrary"),
        name="mix_s5",
    )(u, bb_blk, cc_blk, a_blk, d_skip.reshape(1, w), x0)
    xn = xn.transpose(1, 0, 2)
    g_total = nblk * GROUPS_PER_BLOCK
    xr = xn[..., :half].reshape(bsz, g_total, S5_STATES)
    xi = xn[..., half:].reshape(bsz, g_total, S5_STATES)
    return y, xr, xi


def _block_diag(blocks, per):
    g, r, c = blocks.shape
    b = blocks.reshape(g // per, per, r, c)
    eye = jnp.eye(per, dtype=blocks.dtype)
    return jnp.einsum('nirc,ij->nirjc', b, eye).reshape(g // per, per * r, per * c)


TOKEN_TILE = 512
FFN_TILE = 512
PROJ_TILE = 768
S5_TIME_TILE = 128


def kernel(x_prompt, x_sample, state_mlstm_C, state_mlstm_n, state_mlstm_m, state_mlstm_conv, state_ret_S, state_s5_re, state_s5_im, norm_ffn1, norm_mix, norm_ffn2, norm_final, ffn1_w_gate, ffn1_w_up, ffn1_w_down, ffn2_w_gate, ffn2_w_up, ffn2_w_down, ab_w_in, mlstm_b_i, mlstm_b_f, mlstm_conv_w, mlstm_conv_b, mlstm_norm, ret_norm, ab_w_out, s5_w_in, s5_lambda_re, s5_lambda_im, s5_log_dt, s5_B_re, s5_B_im, s5_C_re, s5_C_im, s5_D, s5_w_glu, s5_b_glu):
    bp, lp, d = x_prompt.shape
    bs, ls, _ = x_sample.shape
    mp, ms = bp * lp, bs * ls
    depth = norm_ffn1.shape[0]
    past_len = 16384

    y = jnp.concatenate([x_prompt.reshape(mp, d), x_sample.reshape(ms, d)], axis=0)
    ffn = functools.partial(_ffn, tm=TOKEN_TILE, tf=FFN_TILE)
    pos_p = jnp.arange(lp, dtype=F32)
    pos_s = past_len + jnp.arange(ls, dtype=F32)

    out_mc, out_mn, out_mm, out_conv, out_rs, out_re, out_im = ([] for _ in range(7))
    for layer in range(depth):
        y = ffn(y, norm_ffn1[layer], ffn1_w_gate[layer].astype(BF16), ffn1_w_up[layer].astype(BF16),
                ffn1_w_down[layer].astype(BF16))
        if layer % 2 == 0:
            e = layer // 2
            w_in = ab_w_in[e]
            n_main = 8 * W_HEADS
            n_gate = 2 * N_HEADS
            zw = -(-(n_main + GATE_COLS) // PROJ_TILE) * PROJ_TILE
            w_all = jnp.concatenate(
                [w_in[:, :4 * W_HEADS], w_in[:, 4 * W_HEADS + n_gate:], w_in[:, 4 * W_HEADS:4 * W_HEADS + n_gate],
                 jnp.zeros((d, zw - n_main - n_gate), F32)], axis=1).astype(BF16)
            z = _norm_matmul(y, norm_mix[layer], w_all, tm=TOKEN_TILE, tn=PROJ_TILE)
            gate_bias = jnp.zeros((1, GATE_COLS), F32).at[0, :N_HEADS].set(mlstm_b_i[e]).at[
                0, N_HEADS:n_gate].set(mlstm_b_f[e])
            common = (mlstm_conv_w[e], mlstm_conv_b[e], gate_bias, mlstm_norm[e], ret_norm[e])
            hp, conv_p, c_p, n_p, m_p, s_p = _mix_ab(
                z, pos_p, jnp.zeros((bp, CONV_W - 1, 2 * W_HEADS), F32),
                jnp.zeros((bp, N_HEADS, D_HEAD, D_HEAD), F32), jnp.zeros((bp, N_HEADS, D_HEAD), F32),
                jnp.full((bp, N_HEADS), M_INIT, F32), jnp.zeros((bp, N_HEADS, D_HEAD, D_HEAD), F32),
                *common, bsz=bp, length=lp, row0=0)
            hs, conv_s, c_s, n_s, m_s, s_s = _mix_ab(
                z, pos_s, state_mlstm_conv[e], state_mlstm_C[e], state_mlstm_n[e], state_mlstm_m[e],
                state_ret_S[e], *common, bsz=bs, length=ls, row0=mp)
            out_mc.append((c_p, c_s))
            out_mn.append((n_p, n_s))
            out_mm.append((m_p, m_s))
            out_conv.append((conv_p, conv_s))
            out_rs.append((s_p, s_s))
            y = _matmul_res(jnp.concatenate([hp, hs], axis=0), ab_w_out[e].astype(BF16), y,
                            tm=TOKEN_TILE, tn=512)
        else:
            o = layer // 2
            u = _norm_matmul(y, norm_mix[layer], s5_w_in[o].astype(BF16), tm=TOKEN_TILE, tn=512)
            a_re, a_im, bb_re, bb_im = _s5_prep(s5_lambda_re[o], s5_lambda_im[o], s5_log_dt[o], s5_B_re[o], s5_B_im[o])
            g, p = a_re.shape
            nblk = g // GROUPS_PER_BLOCK
            bb = jnp.concatenate([
                _block_diag(bb_re.reshape(g, p, S5_GROUP).transpose(0, 2, 1), GROUPS_PER_BLOCK),
                _block_diag(bb_im.reshape(g, p, S5_GROUP).transpose(0, 2, 1), GROUPS_PER_BLOCK)], axis=-1).astype(BF16)
            cc = jnp.concatenate([
                _block_diag(s5_C_re[o].transpose(0, 2, 1), GROUPS_PER_BLOCK),
                _block_diag(-s5_C_im[o].transpose(0, 2, 1), GROUPS_PER_BLOCK)], axis=1).astype(BF16)
            a_blk = jnp.stack([a_re.reshape(nblk, GROUPS_PER_BLOCK * p), a_im.reshape(nblk, GROUPS_PER_BLOCK * p)], axis=1)
            zeros_p = jnp.zeros((bp, g, p), F32)
            ya_p, re_p, im_p = _mix_s5(u[:mp].reshape(bp, lp, -1), zeros_p, zeros_p, bb, cc, a_blk, s5_D[o],
                                       bsz=bp, length=lp, nb=bp, tt=S5_TIME_TILE)
            ya_s, re_s, im_s = _mix_s5(u[mp:].reshape(bs, ls, -1), state_s5_re[o], state_s5_im[o], bb, cc, a_blk,
                                       s5_D[o], bsz=bs, length=ls, nb=bs, tt=ls)
            out_re.append((re_p, re_s))
            out_im.append((im_p, im_s))
            ya = jnp.concatenate([ya_p.reshape(mp, -1), ya_s.reshape(ms, -1)], axis=0)
            y = _glu(ya, s5_w_glu[o].astype(BF16), s5_b_glu[o], y, tm=TOKEN_TILE, tn=512)
        y = ffn(y, norm_ffn2[layer], ffn2_w_gate[layer].astype(BF16), ffn2_w_up[layer].astype(BF16),
                ffn2_w_down[layer].astype(BF16))
    y = _final_norm(y, norm_final, tm=TOKEN_TILE)

    def both(pairs):
        return jnp.stack([p for p, _ in pairs]), jnp.stack([s for _, s in pairs])

    pc, sc = both(out_mc)
    pn, sn = both(out_mn)
    pm, sm = both(out_mm)
    pconv, sconv = both(out_conv)
    ps, ss = both(out_rs)
    pre, sre = both(out_re)
    pim, sim = both(out_im)
    return (y[:mp].reshape(bp, lp, d), y[mp:].reshape(bs, ls, d), pc, sc, pn, sn, pm, sm, pconv, sconv, ps, ss,
            pre, sre, pim, sim)
```

```python
import functools

import jax
import jax.numpy as jnp
from jax import lax
from jax.experimental import pallas as pl
from jax.experimental.pallas import tpu as pltpu

F32 = jnp.float32
BF16 = jnp.bfloat16

EPS = 1e-6
M_INIT = -1e30
CHUNK = 128
CONV_W = 4
ROPE_BASE = 10000.0
PAST_LEN = 16384
N_HEADS = 4
D_HEAD = 256
W_HEADS = N_HEADS * D_HEAD
S5_GROUP = 16
S5_STATES = 64
GROUPS_PER_BLOCK = 16
LANES = 128
SUBLANES = 8
GATE_COLS = LANES

VMEM_LIMIT_BYTES = 60 * 1024 * 1024

TOKEN_TILE = 1024
FFN_TILE = 256
PROJ_TILE = 512
S5_TIME_TILE = 128


def _params(*semantics):
    return pltpu.CompilerParams(dimension_semantics=semantics, vmem_limit_bytes=VMEM_LIMIT_BYTES)


def _single_buffered(shape, index_map):
    return pl.BlockSpec(shape, index_map, pipeline_mode=pl.Buffered(1))


def _rms_norm(x, g):
    return x * lax.rsqrt(jnp.mean(x * x, axis=-1, keepdims=True) + EPS) * g


def _two_group_specs(tm, width, n_a, single=False):
    mk = _single_buffered if single else pl.BlockSpec
    return [mk((tm, width), lambda i, *_: (jnp.minimum(i, n_a - 1), 0)),
            mk((tm, width), lambda i, *_: (jnp.maximum(i - n_a, 0), 0))]


def _for_row_group(i, n_a, a_ref, b_ref, fn):
    @pl.when(i < n_a)
    def _():
        fn(a_ref[...])

    @pl.when(i >= n_a)
    def _():
        fn(b_ref[...])


def _ffn_kernel(*refs, n_a):
    if n_a is None:
        x_ref, g_ref, wg_ref, wu_ref, wd_ref, o_ref, xn_ref = refs
    else:
        xa_ref, xb_ref, g_ref, wg_ref, wu_ref, wd_ref, o_ref, xn_ref = refs
    f = pl.program_id(1)

    def init(x):
        xn_ref[...] = _rms_norm(x, g_ref[...]).astype(BF16)
        o_ref[...] = 2.0 * x

    @pl.when(f == 0)
    def _():
        if n_a is None:
            init(x_ref[...])
        else:
            _for_row_group(pl.program_id(0), n_a, xa_ref, xb_ref, init)

    xn = xn_ref[...]
    a = jnp.dot(xn, wg_ref[...].astype(BF16), preferred_element_type=F32)
    b = jnp.dot(xn, wu_ref[...].astype(BF16), preferred_element_type=F32)
    h = (a * jax.nn.sigmoid(a) * b).astype(BF16)
    o_ref[...] += jnp.dot(h, wd_ref[...].astype(BF16), preferred_element_type=F32)

    @pl.when(f == pl.num_programs(1) - 1)
    def _():
        o_ref[...] = 0.5 * o_ref[...]


def _ffn(xs, g, w_gate, w_up, w_down, layer, *, tm, tf):
    d = xs[0].shape[1]
    m = sum(x.shape[0] for x in xs)
    dff = w_gate.shape[2]
    if len(xs) == 1:
        n_a = None
        x_specs = [_single_buffered((tm, d), lambda i, f: (i, 0))]
    else:
        n_a = xs[0].shape[0] // tm
        x_specs = _two_group_specs(tm, d, n_a, single=True)
    return pl.pallas_call(
        functools.partial(_ffn_kernel, n_a=n_a),
        grid=(m // tm, dff // tf),
        in_specs=x_specs + [
            pl.BlockSpec((None, 1, d), lambda i, f: (layer, 0, 0)),
            pl.BlockSpec((None, d, tf), lambda i, f: (layer, 0, f)),
            pl.BlockSpec((None, d, tf), lambda i, f: (layer, 0, f)),
            pl.BlockSpec((None, tf, d), lambda i, f: (layer, f, 0)),
        ],
        out_specs=_single_buffered((tm, d), lambda i, f: (i, 0)),
        out_shape=jax.ShapeDtypeStruct((m, d), F32),
        scratch_shapes=[pltpu.VMEM((tm, d), BF16)],
        compiler_params=_params("parallel", "arbitrary"),
        name="ffn",
    )(*xs, g.reshape(g.shape[0], 1, d), w_gate, w_up, w_down)


def _ab_proj_kernel(x_ref, g_ref, wlo_ref, whi_ref, wgate_ref, z_ref, zg_ref, xn_ref, *, n_lo):
    j = pl.program_id(1)

    @pl.when(j == 0)
    def _():
        xn_ref[...] = _rms_norm(x_ref[...], g_ref[...]).astype(BF16)
        zg_ref[...] = jnp.dot(xn_ref[...], wgate_ref[...].astype(BF16), preferred_element_type=F32)

    @pl.when(j < n_lo)
    def _():
        z_ref[...] = jnp.dot(xn_ref[...], wlo_ref[...].astype(BF16), preferred_element_type=F32)

    @pl.when(j >= n_lo)
    def _():
        z_ref[...] = jnp.dot(xn_ref[...], whi_ref[...].astype(BF16), preferred_element_type=F32)


def _ab_proj(x, g, w_in, w_hi, w_gate, layer, *, tm, tn):
    m, d = x.shape
    n_lo = 4 * W_HEADS // tn
    n_hi = w_hi.shape[1] // tn
    return pl.pallas_call(
        functools.partial(_ab_proj_kernel, n_lo=n_lo),
        grid=(m // tm, n_lo + n_hi),
        in_specs=[
            _single_buffered((tm, d), lambda i, j: (i, 0)),
            pl.BlockSpec((None, 1, d), lambda i, j: (layer, 0, 0)),
            pl.BlockSpec((d, tn), lambda i, j: (0, jnp.minimum(j, n_lo - 1))),
            pl.BlockSpec((d, tn), lambda i, j: (0, jnp.maximum(j - n_lo, 0))),
            pl.BlockSpec((d, GATE_COLS), lambda i, j: (0, 0)),
        ],
        out_specs=[
            pl.BlockSpec((tm, tn), lambda i, j: (i, j)),
            pl.BlockSpec((tm, GATE_COLS), lambda i, j: (i, 0)),
        ],
        out_shape=[
            jax.ShapeDtypeStruct((m, (n_lo + n_hi) * tn), F32),
            jax.ShapeDtypeStruct((m, GATE_COLS), F32),
        ],
        scratch_shapes=[pltpu.VMEM((tm, d), BF16)],
        compiler_params=_params("parallel", "arbitrary"),
        name="ab_proj",
    )(x, g.reshape(g.shape[0], 1, d), w_in, w_hi, w_gate)


def _norm_proj_kernel(x_ref, g_ref, w_ref, o_ref, wb_ref):
    @pl.when(pl.program_id(0) == 0)
    def _():
        wb_ref[...] = w_ref[...].astype(BF16)

    xn = _rms_norm(x_ref[...], g_ref[...]).astype(BF16)
    o_ref[...] = jnp.dot(xn, wb_ref[...], preferred_element_type=F32)


def _norm_proj(x, g, w, layer, *, tm):
    m, d = x.shape
    n = w.shape[1]
    return pl.pallas_call(
        _norm_proj_kernel,
        grid=(m // tm,),
        in_specs=[
            pl.BlockSpec((tm, d), lambda i: (i, 0)),
            pl.BlockSpec((None, 1, d), lambda i: (layer, 0, 0)),
            _single_buffered((d, n), lambda i: (0, 0)),
        ],
        out_specs=pl.BlockSpec((tm, n), lambda i: (i, 0)),
        out_shape=jax.ShapeDtypeStruct((m, n), F32),
        scratch_shapes=[pltpu.VMEM((d, n), BF16)],
        compiler_params=_params("arbitrary"),
        name="norm_proj",
    )(x, g.reshape(g.shape[0], 1, d), w)


def _proj_res_kernel(aa_ref, ab_ref, w_ref, r_ref, o_ref, wb_ref, *, n_a):
    i = pl.program_id(0)

    @pl.when(i == 0)
    def _():
        wb_ref[...] = w_ref[...].astype(BF16)

    def run(a):
        o_ref[...] = r_ref[...] + jnp.dot(a.astype(BF16), wb_ref[...], preferred_element_type=F32)

    _for_row_group(i, n_a, aa_ref, ab_ref, run)


def _proj_res(a_pair, w, res, *, tm):
    k = a_pair[0].shape[1]
    m = res.shape[0]
    n = w.shape[1]
    n_a = a_pair[0].shape[0] // tm
    return pl.pallas_call(
        functools.partial(_proj_res_kernel, n_a=n_a),
        grid=(m // tm,),
        in_specs=_two_group_specs(tm, k, n_a) + [
            _single_buffered((k, n), lambda i: (0, 0)),
            pl.BlockSpec((tm, n), lambda i: (i, 0)),
        ],
        out_specs=pl.BlockSpec((tm, n), lambda i: (i, 0)),
        out_shape=jax.ShapeDtypeStruct((m, n), F32),
        scratch_shapes=[pltpu.VMEM((k, n), BF16)],
        compiler_params=_params("arbitrary"),
        name="proj_res",
    )(*a_pair, w, res)


def _glu_kernel(aa_ref, ab_ref, wv_ref, wg_ref, bv_ref, bg_ref, r_ref, o_ref, *, n_a):
    def run(a):
        v = jnp.dot(a, wv_ref[...].astype(BF16), preferred_element_type=F32) + bv_ref[...]
        t = jnp.dot(a, wg_ref[...].astype(BF16), preferred_element_type=F32) + bg_ref[...]
        o_ref[...] = r_ref[...] + v * jax.nn.sigmoid(t)

    _for_row_group(pl.program_id(0), n_a, aa_ref, ab_ref, run)


def _glu(a_pair, w, b, res, layer, *, tm, tn):
    k = a_pair[0].shape[1]
    m = res.shape[0]
    n = w.shape[2] // 2
    nj = n // tn
    n_a = a_pair[0].shape[0] // tm
    b3 = b.reshape(b.shape[0], 1, 2 * n)
    return pl.pallas_call(
        functools.partial(_glu_kernel, n_a=n_a),
        grid=(m // tm, nj),
        in_specs=_two_group_specs(tm, k, n_a) + [
            pl.BlockSpec((None, k, tn), lambda i, j: (layer, 0, j)),
            pl.BlockSpec((None, k, tn), lambda i, j: (layer, 0, j + nj)),
            pl.BlockSpec((None, 1, tn), lambda i, j: (layer, 0, j)),
            pl.BlockSpec((None, 1, tn), lambda i, j: (layer, 0, j + nj)),
            pl.BlockSpec((tm, tn), lambda i, j: (i, j)),
        ],
        out_specs=pl.BlockSpec((tm, tn), lambda i, j: (i, j)),
        out_shape=jax.ShapeDtypeStruct((m, n), F32),
        compiler_params=_params("parallel", "arbitrary"),
        name="glu",
    )(*a_pair, w, w, b3, b3, res)


def _final_norm_kernel(x_ref, g_ref, o_ref):
    o_ref[...] = _rms_norm(x_ref[...], g_ref[...])


def _final_norm(x, g, *, row0, m, tm):
    d = x.shape[1]
    blk0 = row0 // tm
    return pl.pallas_call(
        _final_norm_kernel,
        grid=(m // tm,),
        in_specs=[pl.BlockSpec((tm, d), lambda i: (blk0 + i, 0)), pl.BlockSpec((1, d), lambda i: (0, 0))],
        out_specs=pl.BlockSpec((tm, d), lambda i: (i, 0)),
        out_shape=jax.ShapeDtypeStruct((m, d), F32),
        compiler_params=_params("parallel"),
        name="final_norm",
    )(x, g.reshape(1, d))


def _head_layer_norm(x, g):
    xc = x - jnp.mean(x, axis=-1, keepdims=True)
    var = jnp.mean(xc * xc, axis=-1, keepdims=True)
    return xc * lax.rsqrt(var + EPS) * g


def _dot_nt(a, b):
    return lax.dot_general(a, b, (((1,), (1,)), ((), ())), preferred_element_type=F32)


def _dot_tn(a, b):
    return lax.dot_general(a, b, (((0,), (0,)), ((), ())), preferred_element_type=F32)


def _ab_kernel(z_ref, zg_ref, cw_ref, cb_ref, gb_ref, nm_ref, nr_ref, cos_ref, sin_ref, dec_ref, rtab_ref, sdec_ref,
               c0_ref, n0_ref, m0_ref, conv0_ref, s0_ref,
               h_ref, c_ref, n_ref, m_ref, conv_ref, s_ref,
               xp_ref, *, lc):
    chunk = pl.program_id(1)
    pad = SUBLANES

    @pl.when(chunk == 0)
    def _():
        c_ref[...] = c0_ref[...]
        n_ref[...] = n0_ref[...]
        m_ref[...] = m0_ref[...]
        s_ref[...] = s0_ref[...]
        xp_ref[pad - (CONV_W - 1):pad, :] = conv0_ref[0]

    xp_ref[pad:pad + lc, :] = z_ref[:, 0:2 * W_HEADS]
    qk = cb_ref[...]
    for j in range(CONV_W):
        qk = qk + xp_ref[pad - (CONV_W - 1) + j:pad - (CONV_W - 1) + j + lc, :] * cw_ref[j:j + 1, :]
    new_buf = xp_ref[pad + lc - (CONV_W - 1):pad + lc, :]
    xp_ref[pad - (CONV_W - 1):pad, :] = new_buf
    conv_ref[0] = new_buf
    qk = qk * jax.nn.sigmoid(qk)

    row = lax.broadcasted_iota(jnp.int32, (lc, lc), 0)
    col = lax.broadcasted_iota(jnp.int32, (lc, lc), 1)
    causal = row >= col
    eye = row == col

    def row_of(c):
        return jnp.sum(jnp.where(eye, c, 0.0), axis=0, keepdims=True)

    gates = zg_ref[...] + gb_ref[...]
    lf = jax.nn.log_sigmoid(gates)
    bcum = jnp.dot(causal.astype(F32), lf, preferred_element_type=F32, precision=lax.Precision.HIGHEST)

    scale = D_HEAD ** -0.5
    cos = cos_ref[...]
    sin = sin_ref[...]

    def rope(x):
        x1 = x[:, :D_HEAD // 2]
        x2 = x[:, D_HEAD // 2:]
        return jnp.concatenate([x1 * cos - x2 * sin, x1 * sin + x2 * cos], axis=-1)

    for h in range(N_HEADS):
        lo, hi = h * D_HEAD, (h + 1) * D_HEAD
        q = qk[:, lo:hi]
        k = qk[:, W_HEADS + lo:W_HEADS + hi] * scale
        v = z_ref[:, 2 * W_HEADS + lo:2 * W_HEADS + hi]
        og = z_ref[:, 3 * W_HEADS + lo:3 * W_HEADS + hi]
        qb, kb, vb = q.astype(BF16), k.astype(BF16), v.astype(BF16)
        c_prev = c_ref[0, h]
        n_prev = n_ref[0, h:h + 1, :]
        m_prev = m_ref[0, :, h:h + 1]
        ig_c = gates[:, h:h + 1]
        b_c = bcum[:, N_HEADS + h:N_HEADS + h + 1]
        dlog = jnp.where(causal, b_c - row_of(b_c) + row_of(ig_c), -jnp.inf)
        s_log = b_c + m_prev
        m_row = jnp.maximum(s_log, jnp.max(dlog, axis=-1, keepdims=True))
        w = jnp.exp(dlog - m_row) * _dot_nt(qb, kb)
        sc = jnp.exp(s_log - m_row)
        num = sc * jnp.dot(qb, c_prev.astype(BF16), preferred_element_type=F32) + jnp.dot(
            w.astype(BF16), vb, preferred_element_type=F32)
        den = sc * jnp.sum(q * n_prev, axis=-1, keepdims=True) + jnp.sum(w, axis=-1, keepdims=True)
        hm = num / jnp.maximum(jnp.abs(den), jnp.exp(-m_row))
        m_new = m_row[lc - 1:lc, :]
        b_last = b_c[lc - 1:lc, :]
        w_state = jnp.exp(b_last + m_prev - m_new)
        kw = k * jnp.exp(b_last - b_c + ig_c - m_new)
        c_ref[0, h] = w_state * c_prev + _dot_tn(kw.astype(BF16), vb)
        n_ref[0, h:h + 1, :] = w_state * n_prev + jnp.sum(kw, axis=0, keepdims=True)
        m_ref[0, :, h:h + 1] = m_new
        hm = _head_layer_norm(hm, nm_ref[:, lo:hi]) * jax.nn.sigmoid(og)
        h_ref[:, lo:hi] = hm
        qr = rope(z_ref[:, 4 * W_HEADS + lo:4 * W_HEADS + hi])
        kr = rope(z_ref[:, 5 * W_HEADS + lo:5 * W_HEADS + hi]) * scale
        vr = z_ref[:, 6 * W_HEADS + lo:6 * W_HEADS + hi]
        gr = z_ref[:, 7 * W_HEADS + lo:7 * W_HEADS + hi]
        qrb, krb, vrb = qr.astype(BF16), kr.astype(BF16), vr.astype(BF16)
        s_prev = s_ref[0, h]
        att = _dot_nt(qrb, krb) * dec_ref[h]
        inner = jnp.dot(att.astype(BF16), vrb, preferred_element_type=F32)
        cross = jnp.dot(qrb, s_prev.astype(BF16), preferred_element_type=F32) * rtab_ref[:, h:h + 1]
        krw = kr * rtab_ref[:, N_HEADS + h:N_HEADS + h + 1]
        s_ref[0, h] = sdec_ref[0:1, h:h + 1] * s_prev + _dot_tn(krw.astype(BF16), vrb)
        yr = _head_layer_norm(inner + cross, nr_ref[:, lo:hi]) * (gr * jax.nn.sigmoid(gr))
        h_ref[:, W_HEADS + lo:W_HEADS + hi] = yr


def _retention_tables(lc):
    lg = jnp.log1p(-(2.0 ** (-5.0 - jnp.arange(N_HEADS, dtype=F32))))
    idx = jnp.arange(lc, dtype=F32)
    diff = idx[:, None] - idx[None, :]
    decay = jnp.where(diff >= 0, jnp.exp(jnp.maximum(diff, 0.0)[None] * lg[:, None, None]), 0.0)
    cross = jnp.exp((idx[:, None] + 1.0) * lg[None, :])
    w_k = jnp.exp((lc - 1.0 - idx)[:, None] * lg[None, :])
    rtab = jnp.zeros((lc, GATE_COLS), F32).at[:, :N_HEADS].set(cross).at[:, N_HEADS:2 * N_HEADS].set(w_k)
    sdec = jnp.zeros((SUBLANES, GATE_COLS), F32).at[0, :N_HEADS].set(jnp.exp(lc * lg))
    return decay, rtab, sdec


def _rope_tables(pos):
    half = D_HEAD // 2
    freqs = ROPE_BASE ** (-jnp.arange(half, dtype=F32) / half)
    ang = pos[:, None] * freqs[None, :]
    return jnp.cos(ang), jnp.sin(ang)


def _mix_ab(z, zg, pos, conv0, c0, n0, m0, s0, conv_w, conv_b, gate_bias, norm_m, norm_r, *, bsz, length, row0):
    lc = min(CHUNK, length)
    nc = length // lc
    blk0 = row0 // lc
    zw = z.shape[1]
    cos, sin = _rope_tables(pos)
    decay, rtab, sdec = _retention_tables(lc)
    full = lambda *shape: pl.BlockSpec(shape, lambda b, c: (0,) * len(shape))
    state4 = pl.BlockSpec((1, N_HEADS, D_HEAD, D_HEAD), lambda b, c: (b, 0, 0, 0))
    state_n = pl.BlockSpec((1, N_HEADS, D_HEAD), lambda b, c: (b, 0, 0))
    state_m = pl.BlockSpec((1, 1, N_HEADS), lambda b, c: (b, 0, 0))
    state_conv = pl.BlockSpec((1, CONV_W - 1, 2 * W_HEADS), lambda b, c: (b, 0, 0))
    outs = pl.pallas_call(
        functools.partial(_ab_kernel, lc=lc),
        grid=(bsz, nc),
        in_specs=[
            pl.BlockSpec((lc, zw), lambda b, c: (blk0 + b * nc + c, 0)),
            pl.BlockSpec((lc, GATE_COLS), lambda b, c: (blk0 + b * nc + c, 0)),
            full(CONV_W, 2 * W_HEADS), full(1, 2 * W_HEADS), full(1, GATE_COLS),
            full(1, W_HEADS), full(1, W_HEADS),
            pl.BlockSpec((lc, D_HEAD // 2), lambda b, c: (c, 0)),
            pl.BlockSpec((lc, D_HEAD // 2), lambda b, c: (c, 0)),
            full(N_HEADS, lc, lc), full(lc, GATE_COLS), full(SUBLANES, GATE_COLS),
            state4, state_n, state_m, state_conv, state4,
        ],
        out_specs=[
            pl.BlockSpec((lc, 2 * W_HEADS), lambda b, c: (b * nc + c, 0)),
            state4, state_n, state_m, state_conv, state4,
        ],
        out_shape=[
            jax.ShapeDtypeStruct((bsz * length, 2 * W_HEADS), F32),
            jax.ShapeDtypeStruct((bsz, N_HEADS, D_HEAD, D_HEAD), F32),
            jax.ShapeDtypeStruct((bsz, N_HEADS, D_HEAD), F32),
            jax.ShapeDtypeStruct((bsz, 1, N_HEADS), F32),
            jax.ShapeDtypeStruct((bsz, CONV_W - 1, 2 * W_HEADS), F32),
            jax.ShapeDtypeStruct((bsz, N_HEADS, D_HEAD, D_HEAD), F32),
        ],
        scratch_shapes=[pltpu.VMEM((lc + SUBLANES, 2 * W_HEADS), F32)],
        compiler_params=_params("parallel", "arbitrary"),
        name="mix_ab",
    )(z, zg, conv_w, conv_b.reshape(1, -1), gate_bias, norm_m.reshape(1, -1), norm_r.reshape(1, -1),
      cos, sin, decay, rtab, sdec, c0, n0, m0.reshape(bsz, 1, N_HEADS), conv0, s0)
    h, c_new, n_new, m_new, conv_new, s_new = outs
    return h, conv_new, c_new, n_new, m_new.reshape(bsz, N_HEADS), s_new


def _s5_prep_kernel(lr_ref, li_ref, ldt_ref, bre_ref, bim_ref, expand_ref, are_ref, aim_ref, bbre_ref, bbim_ref):
    lr = lr_ref[...]
    li = li_ref[...]
    dt = jnp.exp(ldt_ref[...])
    mag = jnp.exp(lr * dt)
    a_re = mag * jnp.cos(li * dt)
    a_im = mag * jnp.sin(li * dt)
    denom = lr * lr + li * li
    g_re = ((a_re - 1.0) * lr + a_im * li) / denom
    g_im = (a_im * lr - (a_re - 1.0) * li) / denom
    are_ref[...] = a_re
    aim_ref[...] = a_im
    ge_re = jnp.dot(g_re, expand_ref[...], preferred_element_type=F32, precision=lax.Precision.HIGHEST)
    ge_im = jnp.dot(g_im, expand_ref[...], preferred_element_type=F32, precision=lax.Precision.HIGHEST)
    bre = bre_ref[...]
    bim = bim_ref[...]
    bbre_ref[...] = ge_re * bre - ge_im * bim
    bbim_ref[...] = ge_re * bim + ge_im * bre


def _s5_prep(lam_re, lam_im, log_dt, b_re, b_im):
    g, p = lam_re.shape
    expand = jnp.repeat(jnp.eye(p, dtype=F32), S5_GROUP, axis=1)
    return pl.pallas_call(
        _s5_prep_kernel,
        out_shape=[
            jax.ShapeDtypeStruct((g, p), F32), jax.ShapeDtypeStruct((g, p), F32),
            jax.ShapeDtypeStruct((g, p * S5_GROUP), F32), jax.ShapeDtypeStruct((g, p * S5_GROUP), F32),
        ],
        compiler_params=pltpu.CompilerParams(vmem_limit_bytes=VMEM_LIMIT_BYTES),
        name="s5_prep",
    )(lam_re, lam_im, log_dt.reshape(g, 1), b_re.reshape(g, p * S5_GROUP), b_im.reshape(g, p * S5_GROUP), expand)


S5_COLS = GROUPS_PER_BLOCK * S5_GROUP
S5_HALF = GROUPS_PER_BLOCK * S5_STATES
S5_TILES = S5_HALF // LANES
S5_COL_TILES = S5_COLS // LANES


def _s5_scan_dense(bu_ref, xn_ref, a_ref, nb, tt):
    a_re = [jnp.broadcast_to(a_ref[0, 0:1, j * LANES:(j + 1) * LANES], (SUBLANES, LANES)) for j in range(S5_TILES)]
    a_im = [jnp.broadcast_to(a_ref[0, 1:2, j * LANES:(j + 1) * LANES], (SUBLANES, LANES)) for j in range(S5_TILES)]

    def group(s, _):
        r0 = pl.multiple_of(s * SUBLANES, SUBLANES)
        for j in range(S5_TILES):
            xr = xn_ref[pl.ds(r0, SUBLANES), j * LANES:(j + 1) * LANES]
            xi = xn_ref[pl.ds(r0, SUBLANES), S5_HALF + j * LANES:S5_HALF + (j + 1) * LANES]
            for t in range(tt):
                rows = pl.ds(t * nb + r0, SUBLANES)
                xr, xi = (a_re[j] * xr - a_im[j] * xi + bu_ref[j, rows, :],
                          a_re[j] * xi + a_im[j] * xr + bu_ref[S5_TILES + j, rows, :])
                bu_ref[j, rows, :] = xr
                bu_ref[S5_TILES + j, rows, :] = xi
            xn_ref[pl.ds(r0, SUBLANES), j * LANES:(j + 1) * LANES] = xr
            xn_ref[pl.ds(r0, SUBLANES), S5_HALF + j * LANES:S5_HALF + (j + 1) * LANES] = xi
        return 0

    lax.fori_loop(0, nb // SUBLANES, group, 0)


def _s5_scan_pairs(bu_ref, xn_ref, a_ref, nb, tt):
    upper = lax.broadcasted_iota(jnp.int32, (SUBLANES, LANES), 0) >= nb
    c1_re, c1_im, c2_re, c2_im = [], [], [], []
    for j in range(S5_TILES):
        ar = jnp.broadcast_to(a_ref[0, 0:1, j * LANES:(j + 1) * LANES], (SUBLANES, LANES))
        ai = jnp.broadcast_to(a_ref[0, 1:2, j * LANES:(j + 1) * LANES], (SUBLANES, LANES))
        c1_re.append(jnp.where(upper, ar * ar - ai * ai, ar))
        c1_im.append(jnp.where(upper, 2.0 * ar * ai, ai))
        c2_re.append(jnp.where(upper, ar, 0.0))
        c2_im.append(jnp.where(upper, ai, 0.0))

    def dup_state(x):
        return jnp.where(upper, pltpu.roll(x, nb, axis=0), x)

    init = tuple((dup_state(xn_ref[:, j * LANES:(j + 1) * LANES]),
                  dup_state(xn_ref[:, S5_HALF + j * LANES:S5_HALF + (j + 1) * LANES])) for j in range(S5_TILES))

    def pair(i, carry):
        rows = pl.ds(pl.multiple_of(i * SUBLANES, SUBLANES), SUBLANES)
        new = []
        for j in range(S5_TILES):
            pr, pi = carry[j]
            vr = bu_ref[j, rows, :]
            vi = bu_ref[S5_TILES + j, rows, :]
            rr = pltpu.roll(vr, nb, axis=0)
            ri = pltpu.roll(vi, nb, axis=0)
            yr = c1_re[j] * pr - c1_im[j] * pi + (c2_re[j] * rr - c2_im[j] * ri) + vr
            yi = c1_re[j] * pi + c1_im[j] * pr + (c2_re[j] * ri + c2_im[j] * rr) + vi
            bu_ref[j, rows, :] = yr
            bu_ref[S5_TILES + j, rows, :] = yi
            new.append((jnp.where(upper, yr, pltpu.roll(yr, nb, axis=0)),
                        jnp.where(upper, yi, pltpu.roll(yi, nb, axis=0))))
        return tuple(new)

    last = lax.fori_loop(0, tt * nb // SUBLANES, pair, init)
    for j in range(S5_TILES):
        xn_ref[:, j * LANES:(j + 1) * LANES] = last[j][0]
        xn_ref[:, S5_HALF + j * LANES:S5_HALF + (j + 1) * LANES] = last[j][1]


def _s5_kernel(*refs, nb, tt, interleave):
    n_u = nb * S5_COL_TILES if interleave else 1
    u_refs = refs[:n_u]
    bb_ref, cc_ref, a_ref, d_ref, x0_ref, y_ref, xn_ref, bu_ref = refs[n_u:n_u + 8]
    rows = nb * tt

    @pl.when(pl.program_id(1) == 0)
    def _():
        xn_ref[...] = x0_ref[...]

    if interleave:
        ut_ref, yt_ref = refs[n_u + 8:]
        for b in range(nb):
            for c in range(S5_COL_TILES):
                ut_ref[c, pl.ds(b, tt, stride=nb), :] = u_refs[b * S5_COL_TILES + c][...]
        u = jnp.concatenate([ut_ref[c] for c in range(S5_COL_TILES)], axis=-1)
    else:
        u = u_refs[0][...]

    bu = jnp.dot(u.astype(BF16), bb_ref[0], preferred_element_type=F32)
    for j in range(2 * S5_TILES):
        bu_ref[j] = bu[:, j * LANES:(j + 1) * LANES]

    if nb % SUBLANES == 0:
        _s5_scan_dense(bu_ref, xn_ref, a_ref, nb, tt)
    else:
        assert 2 * nb == SUBLANES
        _s5_scan_pairs(bu_ref, xn_ref, a_ref, nb, tt)

    x = jnp.concatenate([bu_ref[j] for j in range(2 * S5_TILES)], axis=-1)
    y = jnp.dot(x.astype(BF16), cc_ref[0], preferred_element_type=F32) + d_ref[...] * u
    y = jax.nn.gelu(y)
    if interleave:
        for c in range(S5_COL_TILES):
            yt_ref[c] = y[:, c * LANES:(c + 1) * LANES]
        for b in range(nb):
            for c in range(S5_COL_TILES):
                y_ref[b, :, c * LANES:(c + 1) * LANES] = yt_ref[c, pl.ds(b, tt, stride=nb), :].astype(BF16)
    else:
        y_ref[...] = y.astype(BF16)


def _mix_s5(u, x0_re, x0_im, bb_blk, cc_blk, a_blk, d_skip, *, bsz, length, tt, row0, interleave):
    w = u.shape[-1]
    nblk = w // S5_COLS
    nt = length // tt
    x0 = jnp.concatenate([x0_re.reshape(bsz, nblk, S5_HALF), x0_im.reshape(bsz, nblk, S5_HALF)], axis=-1)
    srows = max(bsz, SUBLANES)
    x0 = jnp.pad(x0.transpose(1, 0, 2), ((0, 0), (0, srows - bsz), (0, 0)))
    rows = bsz * tt
    if interleave:
        blk0 = row0 // tt
        u_specs = [pl.BlockSpec((tt, LANES), functools.partial(
            lambda g, t, b, c: (blk0 + b * nt + t, g * S5_COL_TILES + c), b=b, c=c))
            for b in range(bsz) for c in range(S5_COL_TILES)]
        u_args = [u] * (bsz * S5_COL_TILES)
        y_spec = pl.BlockSpec((bsz, tt, S5_COLS), lambda g, t: (0, t, g))
        y_shape = jax.ShapeDtypeStruct((bsz, length, w), BF16)
        extra_scratch = [pltpu.VMEM((S5_COL_TILES, rows, LANES), F32), pltpu.VMEM((S5_COL_TILES, rows, LANES), F32)]
    else:
        assert nt == 1
        blk0 = row0 // rows
        u_specs = [pl.BlockSpec((rows, S5_COLS), lambda g, t: (blk0, g))]
        u_args = [u]
        y_spec = pl.BlockSpec((rows, S5_COLS), lambda g, t: (0, g))
        y_shape = jax.ShapeDtypeStruct((rows, w), BF16)
        extra_scratch = []
    y, xn = pl.pallas_call(
        functools.partial(_s5_kernel, nb=bsz, tt=tt, interleave=interleave),
        grid=(nblk, nt),
        in_specs=u_specs + [
            pl.BlockSpec((1, S5_COLS, 2 * S5_HALF), lambda g, t: (g, 0, 0)),
            pl.BlockSpec((1, 2 * S5_HALF, S5_COLS), lambda g, t: (g, 0, 0)),
            pl.BlockSpec((1, 2, S5_HALF), lambda g, t: (g, 0, 0)),
            pl.BlockSpec((1, S5_COLS), lambda g, t: (0, g)),
            pl.BlockSpec((None, srows, 2 * S5_HALF), lambda g, t: (g, 0, 0)),
        ],
        out_specs=[y_spec, pl.BlockSpec((None, srows, 2 * S5_HALF), lambda g, t: (g, 0, 0))],
        out_shape=[y_shape, jax.ShapeDtypeStruct((nblk, srows, 2 * S5_HALF), F32)],
        scratch_shapes=[pltpu.VMEM((2 * S5_TILES, rows, LANES), F32)] + extra_scratch,
        compiler_params=_params("parallel", "arbitrary"),
        name="mix_s5",
    )(*u_args, bb_blk, cc_blk, a_blk, d_skip.reshape(1, w), x0)
    xn = xn[:, :bsz].transpose(1, 0, 2)
    g_total = nblk * GROUPS_PER_BLOCK
    xr = xn[..., :S5_HALF].reshape(bsz, g_total, S5_STATES)
    xi = xn[..., S5_HALF:].reshape(bsz, g_total, S5_STATES)
    return y, xr, xi


def _block_diag(blocks, per):
    g, r, c = blocks.shape
    b = blocks.reshape(g // per, per, r, c)
    eye = jnp.eye(per, dtype=blocks.dtype)
    return jnp.einsum('nirc,ij->nirjc', b, eye).reshape(g // per, per * r, per * c)


def kernel(x_prompt, x_sample, state_mlstm_C, state_mlstm_n, state_mlstm_m, state_mlstm_conv, state_ret_S, state_s5_re, state_s5_im, norm_ffn1, norm_mix, norm_ffn2, norm_final, ffn1_w_gate, ffn1_w_up, ffn1_w_down, ffn2_w_gate, ffn2_w_up, ffn2_w_down, ab_w_in, mlstm_b_i, mlstm_b_f, mlstm_conv_w, mlstm_conv_b, mlstm_norm, ret_norm, ab_w_out, s5_w_in, s5_lambda_re, s5_lambda_im, s5_log_dt, s5_B_re, s5_B_im, s5_C_re, s5_C_im, s5_D, s5_w_glu, s5_b_glu):
    bp, lp, d = x_prompt.shape
    bs, ls, _ = x_sample.shape
    mp, ms = bp * lp, bs * ls
    depth = norm_ffn1.shape[0]
    assert mp % TOKEN_TILE == 0 and ms % TOKEN_TILE == 0

    ffn = functools.partial(_ffn, tm=TOKEN_TILE, tf=FFN_TILE)
    pos_p = jnp.arange(lp, dtype=F32)
    pos_s = PAST_LEN + jnp.arange(ls, dtype=F32)

    ys = (x_prompt.reshape(mp, d), x_sample.reshape(ms, d))
    out_mc, out_mn, out_mm, out_conv, out_rs, out_re, out_im = ([] for _ in range(7))
    for layer in range(depth):
        y = ffn(ys, norm_ffn1, ffn1_w_gate, ffn1_w_up, ffn1_w_down, layer)
        if layer % 2 == 0:
            e = layer // 2
            w_in = ab_w_in[e]
            n_gate = 2 * N_HEADS
            gate0 = 4 * W_HEADS
            w_hi = w_in[:, gate0 + n_gate:]
            w_gate = jnp.pad(w_in[:, gate0:gate0 + n_gate], ((0, 0), (0, GATE_COLS - n_gate)))
            z, zg = _ab_proj(y, norm_mix, w_in, w_hi, w_gate, layer, tm=TOKEN_TILE, tn=PROJ_TILE)
            gate_bias = jnp.zeros((1, GATE_COLS), F32).at[0, :N_HEADS].set(mlstm_b_i[e]).at[
                0, N_HEADS:n_gate].set(mlstm_b_f[e])
            common = (mlstm_conv_w[e], mlstm_conv_b[e], gate_bias, mlstm_norm[e], ret_norm[e])
            hp, conv_p, c_p, n_p, m_p, s_p = _mix_ab(
                z, zg, pos_p, jnp.zeros((bp, CONV_W - 1, 2 * W_HEADS), F32),
                jnp.zeros((bp, N_HEADS, D_HEAD, D_HEAD), F32), jnp.zeros((bp, N_HEADS, D_HEAD), F32),
                jnp.full((bp, N_HEADS), M_INIT, F32), jnp.zeros((bp, N_HEADS, D_HEAD, D_HEAD), F32),
                *common, bsz=bp, length=lp, row0=0)
            hs, conv_s, c_s, n_s, m_s, s_s = _mix_ab(
                z, zg, pos_s, state_mlstm_conv[e], state_mlstm_C[e], state_mlstm_n[e], state_mlstm_m[e],
                state_ret_S[e], *common, bsz=bs, length=ls, row0=mp)
            out_mc.append((c_p, c_s))
            out_mn.append((n_p, n_s))
            out_mm.append((m_p, m_s))
            out_conv.append((conv_p, conv_s))
            out_rs.append((s_p, s_s))
            y = _proj_res((hp, hs), ab_w_out[e], y, tm=TOKEN_TILE // 4)
        else:
            o = layer // 2
            u = _norm_proj(y, norm_mix, s5_w_in[o], layer, tm=TOKEN_TILE // 2)
            a_re, a_im, bb_re, bb_im = _s5_prep(s5_lambda_re[o], s5_lambda_im[o], s5_log_dt[o], s5_B_re[o], s5_B_im[o])
            g, p = a_re.shape
            nblk = g // GROUPS_PER_BLOCK
            bb = jnp.concatenate([
                _block_diag(bb_re.reshape(g, p, S5_GROUP).transpose(0, 2, 1), GROUPS_PER_BLOCK),
                _block_diag(bb_im.reshape(g, p, S5_GROUP).transpose(0, 2, 1), GROUPS_PER_BLOCK)], axis=-1).astype(BF16)
            cc = jnp.concatenate([
                _block_diag(s5_C_re[o].transpose(0, 2, 1), GROUPS_PER_BLOCK),
                _block_diag(-s5_C_im[o].transpose(0, 2, 1), GROUPS_PER_BLOCK)], axis=1).astype(BF16)
            a_blk = jnp.stack([a_re.reshape(nblk, GROUPS_PER_BLOCK * p), a_im.reshape(nblk, GROUPS_PER_BLOCK * p)], axis=1)
            zeros_p = jnp.zeros((bp, g, p), F32)
            ya_p, re_p, im_p = _mix_s5(u, zeros_p, zeros_p, bb, cc, a_blk, s5_D[o], bsz=bp, length=lp,
                                       tt=S5_TIME_TILE, row0=0, interleave=True)
            u_s = u[mp:].reshape(bs, ls, -1).transpose(1, 0, 2).reshape(ms, -1)
            ya_s, re_s, im_s = _mix_s5(u_s, state_s5_re[o], state_s5_im[o], bb, cc, a_blk, s5_D[o], bsz=bs,
                                       length=ls, tt=ls, row0=0, interleave=False)
            ya_s = ya_s.reshape(ls, bs, -1).transpose(1, 0, 2).reshape(ms, -1)
            out_re.append((re_p, re_s))
            out_im.append((im_p, im_s))
            y = _glu((ya_p.reshape(mp, -1), ya_s), s5_w_glu, s5_b_glu, y, o, tm=TOKEN_TILE, tn=PROJ_TILE)
        ys = (ffn((y,), norm_ffn2, ffn2_w_gate, ffn2_w_up, ffn2_w_down, layer),)
    y = ys[0]
    y_p = _final_norm(y, norm_final, row0=0, m=mp, tm=TOKEN_TILE)
    y_s = _final_norm(y, norm_final, row0=mp, m=ms, tm=TOKEN_TILE)

    def both(pairs):
        return jnp.stack([p for p, _ in pairs]), jnp.stack([s for _, s in pairs])

    pc, sc = both(out_mc)
    pn, sn = both(out_mn)
    pm, sm = both(out_mm)
    pconv, sconv = both(out_conv)
    ps, ss = both(out_rs)
    pre, sre = both(out_re)
    pim, sim = both(out_im)
    return (y_p.reshape(bp, lp, d), y_s.reshape(bs, ls, d), pc, sc, pn, sn, pm, sm, pconv, sconv, ps, ss,
            pre, sre, pim, sim)
```

```python
import functools

import jax
import jax.numpy as jnp
from jax import lax
from jax.experimental import pallas as pl
from jax.experimental.pallas import tpu as pltpu

F32 = jnp.float32
BF16 = jnp.bfloat16

EPS = 1e-6
M_INIT = -1e30
CHUNK = 128
CONV_W = 4
ROPE_BASE = 10000.0
PAST_LEN = 16384
N_HEADS = 4
D_HEAD = 256
W_HEADS = N_HEADS * D_HEAD
S5_GROUP = 16
S5_STATES = 64
GROUPS_PER_BLOCK = 16
LANES = 128
SUBLANES = 8
GATE_COLS = LANES

VMEM_LIMIT_BYTES = 60 * 1024 * 1024

TOKEN_TILE = 1024
FFN_TILE = 256
PROJ_TILE = 512
AB_PROJ_ROWS = 1536
S5_TIME_TILE = 128
SAMPLE_BATCH_TILE = 4


def _params(*semantics):
    return pltpu.CompilerParams(dimension_semantics=semantics, vmem_limit_bytes=VMEM_LIMIT_BYTES)


def _single_buffered(shape, index_map):
    return pl.BlockSpec(shape, index_map, pipeline_mode=pl.Buffered(1))


def _rms_norm(x, g):
    return x * lax.rsqrt(jnp.mean(x * x, axis=-1, keepdims=True) + EPS) * g


def _two_group_specs(tm, width, n_a, single=False):
    mk = _single_buffered if single else pl.BlockSpec
    return [mk((tm, width), lambda i, *_: (jnp.minimum(i, n_a - 1), 0)),
            mk((tm, width), lambda i, *_: (jnp.maximum(i - n_a, 0), 0))]


def _for_row_group(i, n_a, a_ref, b_ref, fn):
    @pl.when(i < n_a)
    def _():
        fn(a_ref[...])

    @pl.when(i >= n_a)
    def _():
        fn(b_ref[...])


def _ffn_kernel(*refs, n_a):
    if n_a is None:
        x_ref, g_ref, wg_ref, wu_ref, wd_ref, o_ref, xn_ref = refs
    else:
        xa_ref, xb_ref, g_ref, wg_ref, wu_ref, wd_ref, o_ref, xn_ref = refs
    f = pl.program_id(1)

    def init(x):
        xn_ref[...] = _rms_norm(x, g_ref[...]).astype(BF16)
        o_ref[...] = 2.0 * x

    @pl.when(f == 0)
    def _():
        if n_a is None:
            init(x_ref[...])
        else:
            _for_row_group(pl.program_id(0), n_a, xa_ref, xb_ref, init)

    xn = xn_ref[...]
    a = jnp.dot(xn, wg_ref[...].astype(BF16), preferred_element_type=F32)
    b = jnp.dot(xn, wu_ref[...].astype(BF16), preferred_element_type=F32)
    h = (a * jax.nn.sigmoid(a) * b).astype(BF16)
    o_ref[...] += jnp.dot(h, wd_ref[...].astype(BF16), preferred_element_type=F32)

    @pl.when(f == pl.num_programs(1) - 1)
    def _():
        o_ref[...] = 0.5 * o_ref[...]


def _ffn(xs, g, w_gate, w_up, w_down, layer, *, tm, tf):
    d = xs[0].shape[1]
    m = sum(x.shape[0] for x in xs)
    dff = w_gate.shape[2]
    if len(xs) == 1:
        n_a = None
        x_specs = [pl.BlockSpec((tm, d), lambda i, f: (i, 0))]
    else:
        n_a = xs[0].shape[0] // tm
        x_specs = _two_group_specs(tm, d, n_a, single=True)
    return pl.pallas_call(
        functools.partial(_ffn_kernel, n_a=n_a),
        grid=(m // tm, dff // tf),
        in_specs=x_specs + [
            pl.BlockSpec((None, 1, d), lambda i, f: (layer, 0, 0)),
            pl.BlockSpec((None, d, tf), lambda i, f: (layer, 0, f)),
            pl.BlockSpec((None, d, tf), lambda i, f: (layer, 0, f)),
            pl.BlockSpec((None, tf, d), lambda i, f: (layer, f, 0)),
        ],
        out_specs=pl.BlockSpec((tm, d), lambda i, f: (i, 0)),
        out_shape=jax.ShapeDtypeStruct((m, d), F32),
        scratch_shapes=[pltpu.VMEM((tm, d), BF16)],
        compiler_params=_params("parallel", "arbitrary"),
        name="ffn",
    )(*xs, g.reshape(g.shape[0], 1, d), w_gate, w_up, w_down)


def _ab_proj_kernel(x_ref, g_ref, wlo_ref, whi_ref, wgate_ref, z_ref, zg_ref, xn_ref, *, n_lo):
    j = pl.program_id(1)

    @pl.when(j == 0)
    def _():
        xn_ref[...] = _rms_norm(x_ref[...], g_ref[...]).astype(BF16)
        zg_ref[...] = _dot_nt(xn_ref[...], wgate_ref[...].astype(BF16))

    @pl.when(j < n_lo)
    def _():
        z_ref[...] = _dot_nt(xn_ref[...], wlo_ref[...].astype(BF16))

    @pl.when(j >= n_lo)
    def _():
        z_ref[...] = _dot_nt(xn_ref[...], whi_ref[...].astype(BF16))


def _ab_proj(x, g, wt, wt_hi, layer, *, tm, tn):
    m, d = x.shape
    n_lo = 4 * W_HEADS // tn
    n_hi = wt_hi.shape[0] // tn
    gate_blk = 4 * W_HEADS // GATE_COLS
    return pl.pallas_call(
        functools.partial(_ab_proj_kernel, n_lo=n_lo),
        grid=(m // tm, n_lo + n_hi),
        in_specs=[
            _single_buffered((tm, d), lambda i, j: (i, 0)),
            pl.BlockSpec((None, 1, d), lambda i, j: (layer, 0, 0)),
            pl.BlockSpec((tn, d), lambda i, j: (jnp.minimum(j, n_lo - 1), 0)),
            pl.BlockSpec((tn, d), lambda i, j: (jnp.maximum(j - n_lo, 0), 0)),
            pl.BlockSpec((GATE_COLS, d), lambda i, j: (gate_blk, 0)),
        ],
        out_specs=[
            pl.BlockSpec((tm, tn), lambda i, j: (i, j)),
            pl.BlockSpec((tm, GATE_COLS), lambda i, j: (i, 0)),
        ],
        out_shape=[
            jax.ShapeDtypeStruct((m, (n_lo + n_hi) * tn), F32),
            jax.ShapeDtypeStruct((m, GATE_COLS), F32),
        ],
        scratch_shapes=[pltpu.VMEM((tm, d), BF16)],
        compiler_params=_params("parallel", "arbitrary"),
        name="ab_proj",
    )(x, g.reshape(g.shape[0], 1, d), wt, wt_hi, wt)


def _norm_proj_kernel(x_ref, g_ref, w_ref, o_ref, wb_ref):
    @pl.when(pl.program_id(0) == 0)
    def _():
        wb_ref[...] = w_ref[...].astype(BF16)

    xn = _rms_norm(x_ref[...], g_ref[...]).astype(BF16)
    o_ref[...] = jnp.dot(xn, wb_ref[...], preferred_element_type=F32)


def _norm_proj(x, g, w, layer, *, tm):
    m, d = x.shape
    n = w.shape[1]
    return pl.pallas_call(
        _norm_proj_kernel,
        grid=(m // tm,),
        in_specs=[
            pl.BlockSpec((tm, d), lambda i: (i, 0)),
            pl.BlockSpec((None, 1, d), lambda i: (layer, 0, 0)),
            _single_buffered((d, n), lambda i: (0, 0)),
        ],
        out_specs=pl.BlockSpec((tm, n), lambda i: (i, 0)),
        out_shape=jax.ShapeDtypeStruct((m, n), F32),
        scratch_shapes=[pltpu.VMEM((d, n), BF16)],
        compiler_params=_params("arbitrary"),
        name="norm_proj",
    )(x, g.reshape(g.shape[0], 1, d), w)


def _proj_res_kernel(aa_ref, ab_ref, w_ref, r_ref, o_ref, wb_ref, *, n_a):
    i = pl.program_id(0)

    @pl.when(i == 0)
    def _():
        wb_ref[...] = w_ref[...].astype(BF16)

    def run(a):
        o_ref[...] = r_ref[...] + jnp.dot(a.astype(BF16), wb_ref[...], preferred_element_type=F32)

    _for_row_group(i, n_a, aa_ref, ab_ref, run)


def _proj_res(a_pair, w, res, *, tm):
    k = a_pair[0].shape[1]
    m = res.shape[0]
    n = w.shape[1]
    n_a = a_pair[0].shape[0] // tm
    return pl.pallas_call(
        functools.partial(_proj_res_kernel, n_a=n_a),
        grid=(m // tm,),
        in_specs=_two_group_specs(tm, k, n_a) + [
            _single_buffered((k, n), lambda i: (0, 0)),
            pl.BlockSpec((tm, n), lambda i: (i, 0)),
        ],
        out_specs=pl.BlockSpec((tm, n), lambda i: (i, 0)),
        out_shape=jax.ShapeDtypeStruct((m, n), F32),
        scratch_shapes=[pltpu.VMEM((k, n), BF16)],
        compiler_params=_params("arbitrary"),
        name="proj_res",
    )(*a_pair, w, res)


def _glu_kernel(aa_ref, ab_ref, wv_ref, wg_ref, bv_ref, bg_ref, r_ref, o_ref, *, n_a):
    def run(a):
        v = jnp.dot(a, wv_ref[...].astype(BF16), preferred_element_type=F32) + bv_ref[...]
        t = jnp.dot(a, wg_ref[...].astype(BF16), preferred_element_type=F32) + bg_ref[...]
        o_ref[...] = r_ref[...] + v * jax.nn.sigmoid(t)

    _for_row_group(pl.program_id(0), n_a, aa_ref, ab_ref, run)


def _glu(a_pair, w, b, res, layer, *, tm, tn):
    k = a_pair[0].shape[1]
    m = res.shape[0]
    n = w.shape[2] // 2
    nj = n // tn
    n_a = a_pair[0].shape[0] // tm
    b3 = b.reshape(b.shape[0], 1, 2 * n)
    return pl.pallas_call(
        functools.partial(_glu_kernel, n_a=n_a),
        grid=(m // tm, nj),
        in_specs=_two_group_specs(tm, k, n_a) + [
            pl.BlockSpec((None, k, tn), lambda i, j: (layer, 0, j)),
            pl.BlockSpec((None, k, tn), lambda i, j: (layer, 0, j + nj)),
            pl.BlockSpec((None, 1, tn), lambda i, j: (layer, 0, j)),
            pl.BlockSpec((None, 1, tn), lambda i, j: (layer, 0, j + nj)),
            pl.BlockSpec((tm, tn), lambda i, j: (i, j)),
        ],
        out_specs=pl.BlockSpec((tm, tn), lambda i, j: (i, j)),
        out_shape=jax.ShapeDtypeStruct((m, n), F32),
        compiler_params=_params("parallel", "arbitrary"),
        name="glu",
    )(*a_pair, w, w, b3, b3, res)


def _final_norm_kernel(x_ref, g_ref, o_ref):
    o_ref[...] = _rms_norm(x_ref[...], g_ref[...])


def _final_norm(x, g, *, row0, m, tm):
    d = x.shape[1]
    blk0 = row0 // tm
    return pl.pallas_call(
        _final_norm_kernel,
        grid=(m // tm,),
        in_specs=[pl.BlockSpec((tm, d), lambda i: (blk0 + i, 0)), pl.BlockSpec((1, d), lambda i: (0, 0))],
        out_specs=pl.BlockSpec((tm, d), lambda i: (i, 0)),
        out_shape=jax.ShapeDtypeStruct((m, d), F32),
        compiler_params=_params("parallel"),
        name="final_norm",
    )(x, g.reshape(1, d))


def _head_layer_norm(x, g):
    xc = x - jnp.mean(x, axis=-1, keepdims=True)
    var = jnp.mean(xc * xc, axis=-1, keepdims=True)
    return xc * lax.rsqrt(var + EPS) * g


def _dot_nt(a, b):
    return lax.dot_general(a, b, (((1,), (1,)), ((), ())), preferred_element_type=F32)


def _dot_tn(a, b):
    return lax.dot_general(a, b, (((0,), (0,)), ((), ())), preferred_element_type=F32)


CONV_PAD = SUBLANES


def _ab_kernel(z_ref, zg_ref, cw_ref, cb_ref, gb_ref, nm_ref, nr_ref, cos_ref, sin_ref, dec_ref, rtab_ref, sdec_ref,
               c0_ref, n0_ref, m0_ref, conv0_ref, s0_ref,
               h_ref, c_ref, n_ref, m_ref, conv_ref, s_ref,
               xp_ref, *, lc, nb):
    @pl.when(pl.program_id(1) == 0)
    def _():
        c_ref[...] = c0_ref[...]
        n_ref[...] = n0_ref[...]
        m_ref[...] = m0_ref[...]
        s_ref[...] = s0_ref[...]
        xp_ref[:, CONV_PAD - (CONV_W - 1):CONV_PAD, :] = conv0_ref[...]

    for bi in range(nb):
        _ab_sequence(bi, z_ref, zg_ref, cw_ref, cb_ref, gb_ref, nm_ref, nr_ref, cos_ref, sin_ref, dec_ref, rtab_ref,
                     sdec_ref, h_ref, c_ref, n_ref, m_ref, conv_ref, s_ref, xp_ref, lc=lc)


def _ab_sequence(bi, z_ref, zg_ref, cw_ref, cb_ref, gb_ref, nm_ref, nr_ref, cos_ref, sin_ref, dec_ref, rtab_ref,
                 sdec_ref, h_ref, c_ref, n_ref, m_ref, conv_ref, s_ref, xp_ref, *, lc):
    rows = slice(bi * lc, (bi + 1) * lc)
    pad = CONV_PAD

    xp_ref[bi, pad:pad + lc, :] = z_ref[rows, 0:2 * W_HEADS]
    qk = cb_ref[...]
    for j in range(CONV_W):
        qk = qk + xp_ref[bi, pad - (CONV_W - 1) + j:pad - (CONV_W - 1) + j + lc, :] * cw_ref[j:j + 1, :]
    new_buf = xp_ref[bi, pad + lc - (CONV_W - 1):pad + lc, :]
    xp_ref[bi, pad - (CONV_W - 1):pad, :] = new_buf
    conv_ref[bi] = new_buf
    qk = qk * jax.nn.sigmoid(qk)

    row = lax.broadcasted_iota(jnp.int32, (lc, lc), 0)
    col = lax.broadcasted_iota(jnp.int32, (lc, lc), 1)
    causal = row >= col
    eye = row == col

    def row_of(c):
        return jnp.sum(jnp.where(eye, c, 0.0), axis=0, keepdims=True)

    gates = zg_ref[rows, :] + gb_ref[...]
    lf = jax.nn.log_sigmoid(gates)
    bcum = jnp.dot(causal.astype(F32), lf, preferred_element_type=F32, precision=lax.Precision.HIGHEST)

    scale = D_HEAD ** -0.5
    cos = cos_ref[...]
    sin = sin_ref[...]

    def rope(x):
        x1 = x[:, :D_HEAD // 2]
        x2 = x[:, D_HEAD // 2:]
        return jnp.concatenate([x1 * cos - x2 * sin, x1 * sin + x2 * cos], axis=-1)

    for h in range(N_HEADS):
        lo, hi = h * D_HEAD, (h + 1) * D_HEAD
        q = qk[:, lo:hi]
        k = qk[:, W_HEADS + lo:W_HEADS + hi] * scale
        v = z_ref[rows,2 * W_HEADS + lo:2 * W_HEADS + hi]
        og = z_ref[rows,3 * W_HEADS + lo:3 * W_HEADS + hi]
        qb, kb, vb = q.astype(BF16), k.astype(BF16), v.astype(BF16)
        c_prev = c_ref[bi, h]
        n_prev = n_ref[bi, h:h + 1, :]
        m_prev = m_ref[bi, :, h:h + 1]
        ig_c = gates[:, h:h + 1]
        b_c = bcum[:, N_HEADS + h:N_HEADS + h + 1]
        dlog = jnp.where(causal, b_c - row_of(b_c) + row_of(ig_c), -jnp.inf)
        s_log = b_c + m_prev
        m_row = jnp.maximum(s_log, jnp.max(dlog, axis=-1, keepdims=True))
        w = jnp.exp(dlog - m_row) * _dot_nt(qb, kb)
        sc = jnp.exp(s_log - m_row)
        num = sc * jnp.dot(qb, c_prev.astype(BF16), preferred_element_type=F32) + jnp.dot(
            w.astype(BF16), vb, preferred_element_type=F32)
        den = sc * jnp.sum(q * n_prev, axis=-1, keepdims=True) + jnp.sum(w, axis=-1, keepdims=True)
        hm = num / jnp.maximum(jnp.abs(den), jnp.exp(-m_row))
        m_new = m_row[lc - 1:lc, :]
        b_last = b_c[lc - 1:lc, :]
        w_state = jnp.exp(b_last + m_prev - m_new)
        kw = k * jnp.exp(b_last - b_c + ig_c - m_new)
        c_ref[bi, h] = w_state * c_prev + _dot_tn(kw.astype(BF16), vb)
        n_ref[bi, h:h + 1, :] = w_state * n_prev + jnp.sum(kw, axis=0, keepdims=True)
        m_ref[bi, :, h:h + 1] = m_new
        hm = _head_layer_norm(hm, nm_ref[:, lo:hi]) * jax.nn.sigmoid(og)
        h_ref[rows,lo:hi] = hm
        qr = rope(z_ref[rows,4 * W_HEADS + lo:4 * W_HEADS + hi])
        kr = rope(z_ref[rows,5 * W_HEADS + lo:5 * W_HEADS + hi]) * scale
        vr = z_ref[rows,6 * W_HEADS + lo:6 * W_HEADS + hi]
        gr = z_ref[rows,7 * W_HEADS + lo:7 * W_HEADS + hi]
        qrb, krb, vrb = qr.astype(BF16), kr.astype(BF16), vr.astype(BF16)
        s_prev = s_ref[bi, h]
        att = _dot_nt(qrb, krb) * dec_ref[h]
        inner = jnp.dot(att.astype(BF16), vrb, preferred_element_type=F32)
        cross = jnp.dot(qrb, s_prev.astype(BF16), preferred_element_type=F32) * rtab_ref[:, h:h + 1]
        krw = kr * rtab_ref[:, N_HEADS + h:N_HEADS + h + 1]
        s_ref[bi, h] = sdec_ref[0:1, h:h + 1] * s_prev + _dot_tn(krw.astype(BF16), vrb)
        yr = _head_layer_norm(inner + cross, nr_ref[:, lo:hi]) * (gr * jax.nn.sigmoid(gr))
        h_ref[rows,W_HEADS + lo:W_HEADS + hi] = yr


def _retention_tables(lc):
    lg = jnp.log1p(-(2.0 ** (-5.0 - jnp.arange(N_HEADS, dtype=F32))))
    idx = jnp.arange(lc, dtype=F32)
    diff = idx[:, None] - idx[None, :]
    decay = jnp.where(diff >= 0, jnp.exp(jnp.maximum(diff, 0.0)[None] * lg[:, None, None]), 0.0)
    cross = jnp.exp((idx[:, None] + 1.0) * lg[None, :])
    w_k = jnp.exp((lc - 1.0 - idx)[:, None] * lg[None, :])
    rtab = jnp.zeros((lc, GATE_COLS), F32).at[:, :N_HEADS].set(cross).at[:, N_HEADS:2 * N_HEADS].set(w_k)
    sdec = jnp.zeros((SUBLANES, GATE_COLS), F32).at[0, :N_HEADS].set(jnp.exp(lc * lg))
    return decay, rtab, sdec


def _rope_tables(pos):
    half = D_HEAD // 2
    freqs = ROPE_BASE ** (-jnp.arange(half, dtype=F32) / half)
    ang = pos[:, None] * freqs[None, :]
    return jnp.cos(ang), jnp.sin(ang)


def _mix_ab(z, zg, pos, conv0, c0, n0, m0, s0, conv_w, conv_b, gate_bias, norm_m, norm_r, *, bsz, length, row0, nb=1):
    lc = min(CHUNK, length)
    nc = length // lc
    assert nb == 1 or nc == 1
    rows = nb * lc
    blk0 = row0 // rows
    zw = z.shape[1]
    cos, sin = _rope_tables(pos)
    decay, rtab, sdec = _retention_tables(lc)
    full = lambda *shape: pl.BlockSpec(shape, lambda b, c: (0,) * len(shape))
    state4 = pl.BlockSpec((nb, N_HEADS, D_HEAD, D_HEAD), lambda b, c: (b, 0, 0, 0))
    state_n = pl.BlockSpec((nb, N_HEADS, D_HEAD), lambda b, c: (b, 0, 0))
    state_m = pl.BlockSpec((nb, 1, N_HEADS), lambda b, c: (b, 0, 0))
    state_conv = pl.BlockSpec((nb, CONV_W - 1, 2 * W_HEADS), lambda b, c: (b, 0, 0))
    outs = pl.pallas_call(
        functools.partial(_ab_kernel, lc=lc, nb=nb),
        grid=(bsz // nb, nc),
        in_specs=[
            pl.BlockSpec((rows, zw), lambda b, c: (blk0 + b * nc + c, 0)),
            pl.BlockSpec((rows, GATE_COLS), lambda b, c: (blk0 + b * nc + c, 0)),
            full(CONV_W, 2 * W_HEADS), full(1, 2 * W_HEADS), full(1, GATE_COLS),
            full(1, W_HEADS), full(1, W_HEADS),
            pl.BlockSpec((lc, D_HEAD // 2), lambda b, c: (c, 0)),
            pl.BlockSpec((lc, D_HEAD // 2), lambda b, c: (c, 0)),
            full(N_HEADS, lc, lc), full(lc, GATE_COLS), full(SUBLANES, GATE_COLS),
            state4, state_n, state_m, state_conv, state4,
        ],
        out_specs=[
            pl.BlockSpec((rows, 2 * W_HEADS), lambda b, c: (b * nc + c, 0)),
            state4, state_n, state_m, state_conv, state4,
        ],
        out_shape=[
            jax.ShapeDtypeStruct((bsz * length, 2 * W_HEADS), F32),
            jax.ShapeDtypeStruct((bsz, N_HEADS, D_HEAD, D_HEAD), F32),
            jax.ShapeDtypeStruct((bsz, N_HEADS, D_HEAD), F32),
            jax.ShapeDtypeStruct((bsz, 1, N_HEADS), F32),
            jax.ShapeDtypeStruct((bsz, CONV_W - 1, 2 * W_HEADS), F32),
            jax.ShapeDtypeStruct((bsz, N_HEADS, D_HEAD, D_HEAD), F32),
        ],
        scratch_shapes=[pltpu.VMEM((nb, lc + CONV_PAD, 2 * W_HEADS), F32)],
        compiler_params=_params("parallel", "arbitrary"),
        name="mix_ab",
    )(z, zg, conv_w, conv_b.reshape(1, -1), gate_bias, norm_m.reshape(1, -1), norm_r.reshape(1, -1),
      cos, sin, decay, rtab, sdec, c0, n0, m0.reshape(bsz, 1, N_HEADS), conv0, s0)
    h, c_new, n_new, m_new, conv_new, s_new = outs
    return h, conv_new, c_new, n_new, m_new.reshape(bsz, N_HEADS), s_new


def _s5_prep_kernel(lr_ref, li_ref, ldt_ref, bre_ref, bim_ref, expand_ref, are_ref, aim_ref, bbre_ref, bbim_ref):
    lr = lr_ref[...]
    li = li_ref[...]
    dt = jnp.exp(ldt_ref[...])
    mag = jnp.exp(lr * dt)
    a_re = mag * jnp.cos(li * dt)
    a_im = mag * jnp.sin(li * dt)
    denom = lr * lr + li * li
    g_re = ((a_re - 1.0) * lr + a_im * li) / denom
    g_im = (a_im * lr - (a_re - 1.0) * li) / denom
    are_ref[...] = a_re
    aim_ref[...] = a_im
    ge_re = jnp.dot(g_re, expand_ref[...], preferred_element_type=F32, precision=lax.Precision.HIGHEST)
    ge_im = jnp.dot(g_im, expand_ref[...], preferred_element_type=F32, precision=lax.Precision.HIGHEST)
    bre = bre_ref[...]
    bim = bim_ref[...]
    bbre_ref[...] = ge_re * bre - ge_im * bim
    bbim_ref[...] = ge_re * bim + ge_im * bre


def _s5_prep(lam_re, lam_im, log_dt, b_re, b_im):
    g, p = lam_re.shape
    expand = jnp.repeat(jnp.eye(p, dtype=F32), S5_GROUP, axis=1)
    return pl.pallas_call(
        _s5_prep_kernel,
        out_shape=[
            jax.ShapeDtypeStruct((g, p), F32), jax.ShapeDtypeStruct((g, p), F32),
            jax.ShapeDtypeStruct((g, p * S5_GROUP), F32), jax.ShapeDtypeStruct((g, p * S5_GROUP), F32),
        ],
        compiler_params=pltpu.CompilerParams(vmem_limit_bytes=VMEM_LIMIT_BYTES),
        name="s5_prep",
    )(lam_re, lam_im, log_dt.reshape(g, 1), b_re.reshape(g, p * S5_GROUP), b_im.reshape(g, p * S5_GROUP), expand)


S5_COLS = GROUPS_PER_BLOCK * S5_GROUP
S5_HALF = GROUPS_PER_BLOCK * S5_STATES
S5_TILES = S5_HALF // LANES
S5_COL_TILES = S5_COLS // LANES


def _s5_scan_dense(bu_ref, xn_ref, a_ref, nb, tt):
    a_re = [jnp.broadcast_to(a_ref[0, 0:1, j * LANES:(j + 1) * LANES], (SUBLANES, LANES)) for j in range(S5_TILES)]
    a_im = [jnp.broadcast_to(a_ref[0, 1:2, j * LANES:(j + 1) * LANES], (SUBLANES, LANES)) for j in range(S5_TILES)]

    def group(s, _):
        r0 = pl.multiple_of(s * SUBLANES, SUBLANES)
        for j in range(S5_TILES):
            xr = xn_ref[pl.ds(r0, SUBLANES), j * LANES:(j + 1) * LANES]
            xi = xn_ref[pl.ds(r0, SUBLANES), S5_HALF + j * LANES:S5_HALF + (j + 1) * LANES]
            for t in range(tt):
                rows = pl.ds(t * nb + r0, SUBLANES)
                xr, xi = (a_re[j] * xr - a_im[j] * xi + bu_ref[j, rows, :],
                          a_re[j] * xi + a_im[j] * xr + bu_ref[S5_TILES + j, rows, :])
                bu_ref[j, rows, :] = xr
                bu_ref[S5_TILES + j, rows, :] = xi
            xn_ref[pl.ds(r0, SUBLANES), j * LANES:(j + 1) * LANES] = xr
            xn_ref[pl.ds(r0, SUBLANES), S5_HALF + j * LANES:S5_HALF + (j + 1) * LANES] = xi
        return 0

    lax.fori_loop(0, nb // SUBLANES, group, 0)


def _s5_scan_pairs(bu_ref, xn_ref, a_ref, nb, tt):
    upper = lax.broadcasted_iota(jnp.int32, (SUBLANES, LANES), 0) >= nb
    c1_re, c1_im, c2_re, c2_im = [], [], [], []
    for j in range(S5_TILES):
        ar = jnp.broadcast_to(a_ref[0, 0:1, j * LANES:(j + 1) * LANES], (SUBLANES, LANES))
        ai = jnp.broadcast_to(a_ref[0, 1:2, j * LANES:(j + 1) * LANES], (SUBLANES, LANES))
        c1_re.append(jnp.where(upper, ar * ar - ai * ai, ar))
        c1_im.append(jnp.where(upper, 2.0 * ar * ai, ai))
        c2_re.append(jnp.where(upper, ar, 0.0))
        c2_im.append(jnp.where(upper, ai, 0.0))

    def dup_state(x):
        return jnp.where(upper, pltpu.roll(x, nb, axis=0), x)

    init = tuple((dup_state(xn_ref[:, j * LANES:(j + 1) * LANES]),
                  dup_state(xn_ref[:, S5_HALF + j * LANES:S5_HALF + (j + 1) * LANES])) for j in range(S5_TILES))

    def pair(i, carry):
        rows = pl.ds(pl.multiple_of(i * SUBLANES, SUBLANES), SUBLANES)
        new = []
        for j in range(S5_TILES):
            pr, pi = carry[j]
            vr = bu_ref[j, rows, :]
            vi = bu_ref[S5_TILES + j, rows, :]
            rr = pltpu.roll(vr, nb, axis=0)
            ri = pltpu.roll(vi, nb, axis=0)
            yr = c1_re[j] * pr - c1_im[j] * pi + (c2_re[j] * rr - c2_im[j] * ri) + vr
            yi = c1_re[j] * pi + c1_im[j] * pr + (c2_re[j] * ri + c2_im[j] * rr) + vi
            bu_ref[j, rows, :] = yr
            bu_ref[S5_TILES + j, rows, :] = yi
            new.append((jnp.where(upper, yr, pltpu.roll(yr, nb, axis=0)),
                        jnp.where(upper, yi, pltpu.roll(yi, nb, axis=0))))
        return tuple(new)

    last = lax.fori_loop(0, tt * nb // SUBLANES, pair, init)
    for j in range(S5_TILES):
        xn_ref[:, j * LANES:(j + 1) * LANES] = last[j][0]
        xn_ref[:, S5_HALF + j * LANES:S5_HALF + (j + 1) * LANES] = last[j][1]


def _s5_kernel(*refs, nb, tt, interleave):
    n_u = nb * S5_COL_TILES if interleave else 1
    u_refs = refs[:n_u]
    bb_ref, cc_ref, a_ref, d_ref, x0_ref, y_ref, xn_ref, bu_ref = refs[n_u:n_u + 8]
    rows = nb * tt

    @pl.when(pl.program_id(1) == 0)
    def _():
        xn_ref[...] = x0_ref[...]

    if interleave:
        ut_ref, yt_ref = refs[n_u + 8:]
        for b in range(nb):
            for c in range(S5_COL_TILES):
                ut_ref[c, pl.ds(b, tt, stride=nb), :] = u_refs[b * S5_COL_TILES + c][...]
        u = jnp.concatenate([ut_ref[c] for c in range(S5_COL_TILES)], axis=-1)
    else:
        u = u_refs[0][...]

    bu = jnp.dot(u.astype(BF16), bb_ref[0], preferred_element_type=F32)
    for j in range(2 * S5_TILES):
        bu_ref[j] = bu[:, j * LANES:(j + 1) * LANES]

    if nb % SUBLANES == 0:
        _s5_scan_dense(bu_ref, xn_ref, a_ref, nb, tt)
    else:
        assert 2 * nb == SUBLANES
        _s5_scan_pairs(bu_ref, xn_ref, a_ref, nb, tt)

    x = jnp.concatenate([bu_ref[j] for j in range(2 * S5_TILES)], axis=-1)
    y = jnp.dot(x.astype(BF16), cc_ref[0], preferred_element_type=F32) + d_ref[...] * u
    y = jax.nn.gelu(y)
    if interleave:
        for c in range(S5_COL_TILES):
            yt_ref[c] = y[:, c * LANES:(c + 1) * LANES]
        for b in range(nb):
            for c in range(S5_COL_TILES):
                y_ref[b, :, c * LANES:(c + 1) * LANES] = yt_ref[c, pl.ds(b, tt, stride=nb), :].astype(BF16)
    else:
        y_ref[...] = y.astype(BF16)


def _mix_s5(u, x0_re, x0_im, bb_blk, cc_blk, a_blk, d_skip, *, bsz, length, tt, row0, interleave):
    w = u.shape[-1]
    nblk = w // S5_COLS
    nt = length // tt
    x0 = jnp.concatenate([x0_re.reshape(bsz, nblk, S5_HALF), x0_im.reshape(bsz, nblk, S5_HALF)], axis=-1)
    srows = max(bsz, SUBLANES)
    x0 = jnp.pad(x0.transpose(1, 0, 2), ((0, 0), (0, srows - bsz), (0, 0)))
    rows = bsz * tt
    if interleave:
        blk0 = row0 // tt
        u_specs = [pl.BlockSpec((tt, LANES), functools.partial(
            lambda g, t, b, c: (blk0 + b * nt + t, g * S5_COL_TILES + c), b=b, c=c))
            for b in range(bsz) for c in range(S5_COL_TILES)]
        u_args = [u] * (bsz * S5_COL_TILES)
        y_spec = pl.BlockSpec((bsz, tt, S5_COLS), lambda g, t: (0, t, g))
        y_shape = jax.ShapeDtypeStruct((bsz, length, w), BF16)
        extra_scratch = [pltpu.VMEM((S5_COL_TILES, rows, LANES), F32), pltpu.VMEM((S5_COL_TILES, rows, LANES), F32)]
    else:
        assert nt == 1
        blk0 = row0 // rows
        u_specs = [pl.BlockSpec((rows, S5_COLS), lambda g, t: (blk0, g))]
        u_args = [u]
        y_spec = pl.BlockSpec((rows, S5_COLS), lambda g, t: (0, g))
        y_shape = jax.ShapeDtypeStruct((rows, w), BF16)
        extra_scratch = []
    y, xn = pl.pallas_call(
        functools.partial(_s5_kernel, nb=bsz, tt=tt, interleave=interleave),
        grid=(nblk, nt),
        in_specs=u_specs + [
            pl.BlockSpec((1, S5_COLS, 2 * S5_HALF), lambda g, t: (g, 0, 0)),
            pl.BlockSpec((1, 2 * S5_HALF, S5_COLS), lambda g, t: (g, 0, 0)),
            pl.BlockSpec((1, 2, S5_HALF), lambda g, t: (g, 0, 0)),
            pl.BlockSpec((1, S5_COLS), lambda g, t: (0, g)),
            pl.BlockSpec((None, srows, 2 * S5_HALF), lambda g, t: (g, 0, 0)),
        ],
        out_specs=[y_spec, pl.BlockSpec((None, srows, 2 * S5_HALF), lambda g, t: (g, 0, 0))],
        out_shape=[y_shape, jax.ShapeDtypeStruct((nblk, srows, 2 * S5_HALF), F32)],
        scratch_shapes=[pltpu.VMEM((2 * S5_TILES, rows, LANES), F32)] + extra_scratch,
        compiler_params=_params("parallel", "arbitrary"),
        name="mix_s5",
    )(*u_args, bb_blk, cc_blk, a_blk, d_skip.reshape(1, w), x0)
    xn = xn[:, :bsz].transpose(1, 0, 2)
    g_total = nblk * GROUPS_PER_BLOCK
    xr = xn[..., :S5_HALF].reshape(bsz, g_total, S5_STATES)
    xi = xn[..., S5_HALF:].reshape(bsz, g_total, S5_STATES)
    return y, xr, xi


def _block_diag(blocks, per):
    g, r, c = blocks.shape
    b = blocks.reshape(g // per, per, r, c)
    eye = jnp.eye(per, dtype=blocks.dtype)
    return jnp.einsum('nirc,ij->nirjc', b, eye).reshape(g // per, per * r, per * c)


def kernel(x_prompt, x_sample, state_mlstm_C, state_mlstm_n, state_mlstm_m, state_mlstm_conv, state_ret_S, state_s5_re, state_s5_im, norm_ffn1, norm_mix, norm_ffn2, norm_final, ffn1_w_gate, ffn1_w_up, ffn1_w_down, ffn2_w_gate, ffn2_w_up, ffn2_w_down, ab_w_in, mlstm_b_i, mlstm_b_f, mlstm_conv_w, mlstm_conv_b, mlstm_norm, ret_norm, ab_w_out, s5_w_in, s5_lambda_re, s5_lambda_im, s5_log_dt, s5_B_re, s5_B_im, s5_C_re, s5_C_im, s5_D, s5_w_glu, s5_b_glu):
    bp, lp, d = x_prompt.shape
    bs, ls, _ = x_sample.shape
    mp, ms = bp * lp, bs * ls
    depth = norm_ffn1.shape[0]
    assert mp % TOKEN_TILE == 0 and ms % TOKEN_TILE == 0

    ffn = functools.partial(_ffn, tm=TOKEN_TILE, tf=FFN_TILE)
    pos_p = jnp.arange(lp, dtype=F32)
    pos_s = PAST_LEN + jnp.arange(ls, dtype=F32)

    ys = (x_prompt.reshape(mp, d), x_sample.reshape(ms, d))
    out_mc, out_mn, out_mm, out_conv, out_rs, out_re, out_im = ([] for _ in range(7))
    for layer in range(depth):
        y = ffn(ys, norm_ffn1, ffn1_w_gate, ffn1_w_up, ffn1_w_down, layer)
        if layer % 2 == 0:
            e = layer // 2
            n_gate = 2 * N_HEADS
            wt = jnp.swapaxes(ab_w_in, 1, 2)[e]
            wt_hi = wt[4 * W_HEADS + n_gate:]
            z, zg = _ab_proj(y, norm_mix, wt, wt_hi, layer, tm=AB_PROJ_ROWS, tn=PROJ_TILE)
            gate_bias = jnp.zeros((1, GATE_COLS), F32).at[0, :N_HEADS].set(mlstm_b_i[e]).at[
                0, N_HEADS:n_gate].set(mlstm_b_f[e])
            common = (mlstm_conv_w[e], mlstm_conv_b[e], gate_bias, mlstm_norm[e], ret_norm[e])
            hp, conv_p, c_p, n_p, m_p, s_p = _mix_ab(
                z, zg, pos_p, jnp.zeros((bp, CONV_W - 1, 2 * W_HEADS), F32),
                jnp.zeros((bp, N_HEADS, D_HEAD, D_HEAD), F32), jnp.zeros((bp, N_HEADS, D_HEAD), F32),
                jnp.full((bp, N_HEADS), M_INIT, F32), jnp.zeros((bp, N_HEADS, D_HEAD, D_HEAD), F32),
                *common, bsz=bp, length=lp, row0=0)
            hs, conv_s, c_s, n_s, m_s, s_s = _mix_ab(
                z, zg, pos_s, state_mlstm_conv[e], state_mlstm_C[e], state_mlstm_n[e], state_mlstm_m[e],
                state_ret_S[e], *common, bsz=bs, length=ls, row0=mp, nb=SAMPLE_BATCH_TILE)
            out_mc.append((c_p, c_s))
            out_mn.append((n_p, n_s))
            out_mm.append((m_p, m_s))
            out_conv.append((conv_p, conv_s))
            out_rs.append((s_p, s_s))
            y = _proj_res((hp, hs), ab_w_out[e], y, tm=TOKEN_TILE // 4)
        else:
            o = layer // 2
            u = _norm_proj(y, norm_mix, s5_w_in[o], layer, tm=TOKEN_TILE // 2)
            a_re, a_im, bb_re, bb_im = _s5_prep(s5_lambda_re[o], s5_lambda_im[o], s5_log_dt[o], s5_B_re[o], s5_B_im[o])
            g, p = a_re.shape
            nblk = g // GROUPS_PER_BLOCK
            bb = jnp.concatenate([
                _block_diag(bb_re.reshape(g, p, S5_GROUP).transpose(0, 2, 1), GROUPS_PER_BLOCK),
                _block_diag(bb_im.reshape(g, p, S5_GROUP).transpose(0, 2, 1), GROUPS_PER_BLOCK)], axis=-1).astype(BF16)
            cc = jnp.concatenate([
                _block_diag(s5_C_re[o].transpose(0, 2, 1), GROUPS_PER_BLOCK),
                _block_diag(-s5_C_im[o].transpose(0, 2, 1), GROUPS_PER_BLOCK)], axis=1).astype(BF16)
            a_blk = jnp.stack([a_re.reshape(nblk, GROUPS_PER_BLOCK * p), a_im.reshape(nblk, GROUPS_PER_BLOCK * p)], axis=1)
            zeros_p = jnp.zeros((bp, g, p), F32)
            ya_p, re_p, im_p = _mix_s5(u, zeros_p, zeros_p, bb, cc, a_blk, s5_D[o], bsz=bp, length=lp,
                                       tt=S5_TIME_TILE, row0=0, interleave=True)
            u_s = u[mp:].reshape(bs, ls, -1).transpose(1, 0, 2).reshape(ms, -1)
            ya_s, re_s, im_s = _mix_s5(u_s, state_s5_re[o], state_s5_im[o], bb, cc, a_blk, s5_D[o], bsz=bs,
                                       length=ls, tt=ls, row0=0, interleave=False)
            ya_s = ya_s.reshape(ls, bs, -1).transpose(1, 0, 2).reshape(ms, -1)
            out_re.append((re_p, re_s))
            out_im.append((im_p, im_s))
            y = _glu((ya_p.reshape(mp, -1), ya_s), s5_w_glu, s5_b_glu, y, o, tm=TOKEN_TILE, tn=PROJ_TILE)
        ys = (ffn((y,), norm_ffn2, ffn2_w_gate, ffn2_w_up, ffn2_w_down, layer),)
    y = ys[0]
    y_p = _final_norm(y, norm_final, row0=0, m=mp, tm=TOKEN_TILE)
    y_s = _final_norm(y, norm_final, row0=mp, m=ms, tm=TOKEN_TILE)

    def both(pairs):
        return jnp.stack([p for p, _ in pairs]), jnp.stack([s for _, s in pairs])

    pc, sc = both(out_mc)
    pn, sn = both(out_mn)
    pm, sm = both(out_mm)
    pconv, sconv = both(out_conv)
    ps, ss = both(out_rs)
    pre, sre = both(out_re)
    pim, sim = both(out_im)
    return (y_p.reshape(bp, lp, d), y_s.reshape(bs, ls, d), pc, sc, pn, sn, pm, sm, pconv, sconv, ps, ss,
            pre, sre, pim, sim)
```

```python
import functools

import jax
import jax.numpy as jnp
from jax import lax
from jax.experimental import pallas as pl
from jax.experimental.pallas import tpu as pltpu

F32 = jnp.float32
BF16 = jnp.bfloat16

EPS = 1e-6
M_INIT = -1e30
CHUNK = 128
CONV_W = 4
ROPE_BASE = 10000.0
PAST_LEN = 16384
N_HEADS = 4
D_HEAD = 256
W_HEADS = N_HEADS * D_HEAD
S5_GROUP = 16
S5_STATES = 64
GROUPS_PER_BLOCK = 16
LANES = 128
SUBLANES = 8
GATE_COLS = LANES

VMEM_LIMIT_BYTES = 60 * 1024 * 1024

TOKEN_TILE = 1024
FFN_TILE = 256
PROJ_TILE = 512
AB_PROJ_ROWS = 1536
S5_TIME_TILE = 128
SAMPLE_BATCH_TILE = 4


def _params(*semantics):
    return pltpu.CompilerParams(dimension_semantics=semantics, vmem_limit_bytes=VMEM_LIMIT_BYTES)


def _single_buffered(shape, index_map):
    return pl.BlockSpec(shape, index_map, pipeline_mode=pl.Buffered(1))


def _rms_norm(x, g):
    return x * lax.rsqrt(jnp.mean(x * x, axis=-1, keepdims=True) + EPS) * g


def _two_group_specs(tm, width, n_a, single=False):
    mk = _single_buffered if single else pl.BlockSpec
    return [mk((tm, width), lambda i, *_: (jnp.minimum(i, n_a - 1), 0)),
            mk((tm, width), lambda i, *_: (jnp.maximum(i - n_a, 0), 0))]


def _for_row_group(i, n_a, a_ref, b_ref, fn):
    @pl.when(i < n_a)
    def _():
        fn(a_ref[...])

    @pl.when(i >= n_a)
    def _():
        fn(b_ref[...])


def _ffn_kernel(*refs, n_a):
    if n_a is None:
        x_ref, g_ref, wg_ref, wu_ref, wd_ref, o_ref, xn_ref = refs
    else:
        xa_ref, xb_ref, g_ref, wg_ref, wu_ref, wd_ref, o_ref, xn_ref = refs
    f = pl.program_id(1)

    def init(x):
        xn_ref[...] = _rms_norm(x, g_ref[...]).astype(BF16)
        o_ref[...] = 2.0 * x

    @pl.when(f == 0)
    def _():
        if n_a is None:
            init(x_ref[...])
        else:
            _for_row_group(pl.program_id(0), n_a, xa_ref, xb_ref, init)

    xn = xn_ref[...]
    a = jnp.dot(xn, wg_ref[...].astype(BF16), preferred_element_type=F32)
    b = jnp.dot(xn, wu_ref[...].astype(BF16), preferred_element_type=F32)
    h = (a * jax.nn.sigmoid(a) * b).astype(BF16)
    o_ref[...] += jnp.dot(h, wd_ref[...].astype(BF16), preferred_element_type=F32)

    @pl.when(f == pl.num_programs(1) - 1)
    def _():
        o_ref[...] = 0.5 * o_ref[...]


def _ffn(xs, g, w_gate, w_up, w_down, layer, *, tm, tf):
    d = xs[0].shape[1]
    m = sum(x.shape[0] for x in xs)
    dff = w_gate.shape[2]
    if len(xs) == 1:
        n_a = None
        x_specs = [pl.BlockSpec((tm, d), lambda i, f: (i, 0))]
    else:
        n_a = xs[0].shape[0] // tm
        x_specs = _two_group_specs(tm, d, n_a, single=True)
    return pl.pallas_call(
        functools.partial(_ffn_kernel, n_a=n_a),
        grid=(m // tm, dff // tf),
        in_specs=x_specs + [
            pl.BlockSpec((None, 1, d), lambda i, f: (layer, 0, 0)),
            pl.BlockSpec((None, d, tf), lambda i, f: (layer, 0, f)),
            pl.BlockSpec((None, d, tf), lambda i, f: (layer, 0, f)),
            pl.BlockSpec((None, tf, d), lambda i, f: (layer, f, 0)),
        ],
        out_specs=pl.BlockSpec((tm, d), lambda i, f: (i, 0)),
        out_shape=jax.ShapeDtypeStruct((m, d), F32),
        scratch_shapes=[pltpu.VMEM((tm, d), BF16)],
        compiler_params=_params("parallel", "arbitrary"),
        name="ffn",
    )(*xs, g.reshape(g.shape[0], 1, d), w_gate, w_up, w_down)


def _ab_proj_kernel(x_ref, g_ref, wlo_ref, whi_ref, wgate_ref, z_ref, zg_ref, xn_ref, *, n_lo):
    j = pl.program_id(1)

    @pl.when(j == 0)
    def _():
        xn_ref[...] = _rms_norm(x_ref[...], g_ref[...]).astype(BF16)
        zg_ref[...] = _dot_nt(xn_ref[...], wgate_ref[...].astype(BF16))

    @pl.when(j < n_lo)
    def _():
        z_ref[...] = _dot_nt(xn_ref[...], wlo_ref[...].astype(BF16))

    @pl.when(j >= n_lo)
    def _():
        z_ref[...] = _dot_nt(xn_ref[...], whi_ref[...].astype(BF16))


def _ab_proj(x, g, wt, wt_hi, layer, *, tm, tn):
    m, d = x.shape
    n_lo = 4 * W_HEADS // tn
    n_hi = wt_hi.shape[0] // tn
    gate_blk = 4 * W_HEADS // GATE_COLS
    return pl.pallas_call(
        functools.partial(_ab_proj_kernel, n_lo=n_lo),
        grid=(m // tm, n_lo + n_hi),
        in_specs=[
            _single_buffered((tm, d), lambda i, j: (i, 0)),
            pl.BlockSpec((None, 1, d), lambda i, j: (layer, 0, 0)),
            pl.BlockSpec((tn, d), lambda i, j: (jnp.minimum(j, n_lo - 1), 0)),
            pl.BlockSpec((tn, d), lambda i, j: (jnp.maximum(j - n_lo, 0), 0)),
            pl.BlockSpec((GATE_COLS, d), lambda i, j: (gate_blk, 0)),
        ],
        out_specs=[
            pl.BlockSpec((tm, tn), lambda i, j: (i, j)),
            pl.BlockSpec((tm, GATE_COLS), lambda i, j: (i, 0)),
        ],
        out_shape=[
            jax.ShapeDtypeStruct((m, (n_lo + n_hi) * tn), F32),
            jax.ShapeDtypeStruct((m, GATE_COLS), F32),
        ],
        scratch_shapes=[pltpu.VMEM((tm, d), BF16)],
        compiler_params=_params("parallel", "arbitrary"),
        name="ab_proj",
    )(x, g.reshape(g.shape[0], 1, d), wt, wt_hi, wt)


def _norm_proj_kernel(x_ref, g_ref, w_ref, o_ref, wb_ref):
    @pl.when(pl.program_id(0) == 0)
    def _():
        wb_ref[...] = w_ref[...].astype(BF16)

    xn = _rms_norm(x_ref[...], g_ref[...]).astype(BF16)
    o_ref[...] = jnp.dot(xn, wb_ref[...], preferred_element_type=F32)


def _norm_proj(x, g, w, layer, *, tm):
    m, d = x.shape
    n = w.shape[1]
    return pl.pallas_call(
        _norm_proj_kernel,
        grid=(m // tm,),
        in_specs=[
            pl.BlockSpec((tm, d), lambda i: (i, 0)),
            pl.BlockSpec((None, 1, d), lambda i: (layer, 0, 0)),
            _single_buffered((d, n), lambda i: (0, 0)),
        ],
        out_specs=pl.BlockSpec((tm, n), lambda i: (i, 0)),
        out_shape=jax.ShapeDtypeStruct((m, n), F32),
        scratch_shapes=[pltpu.VMEM((d, n), BF16)],
        compiler_params=_params("arbitrary"),
        name="norm_proj",
    )(x, g.reshape(g.shape[0], 1, d), w)


def _proj_res_kernel(aa_ref, ab_ref, w_ref, r_ref, o_ref, wb_ref, *, n_a):
    i = pl.program_id(0)

    @pl.when(i == 0)
    def _():
        wb_ref[...] = w_ref[...].astype(BF16)

    def run(a):
        o_ref[...] = r_ref[...] + jnp.dot(a.astype(BF16), wb_ref[...], preferred_element_type=F32)

    _for_row_group(i, n_a, aa_ref, ab_ref, run)


def _proj_res(a_pair, w, res, *, tm):
    k = a_pair[0].shape[1]
    m = res.shape[0]
    n = w.shape[1]
    n_a = a_pair[0].shape[0] // tm
    return pl.pallas_call(
        functools.partial(_proj_res_kernel, n_a=n_a),
        grid=(m // tm,),
        in_specs=_two_group_specs(tm, k, n_a) + [
            _single_buffered((k, n), lambda i: (0, 0)),
            pl.BlockSpec((tm, n), lambda i: (i, 0)),
        ],
        out_specs=pl.BlockSpec((tm, n), lambda i: (i, 0)),
        out_shape=jax.ShapeDtypeStruct((m, n), F32),
        scratch_shapes=[pltpu.VMEM((k, n), BF16)],
        compiler_params=_params("arbitrary"),
        name="proj_res",
    )(*a_pair, w, res)


def _glu_kernel(aa_ref, ab_ref, wv_ref, wg_ref, bv_ref, bg_ref, r_ref, o_ref, *, n_a):
    def run(a):
        v = jnp.dot(a, wv_ref[...].astype(BF16), preferred_element_type=F32) + bv_ref[...]
        t = jnp.dot(a, wg_ref[...].astype(BF16), preferred_element_type=F32) + bg_ref[...]
        o_ref[...] = r_ref[...] + v * jax.nn.sigmoid(t)

    _for_row_group(pl.program_id(0), n_a, aa_ref, ab_ref, run)


def _glu(a_pair, w, b, res, layer, *, tm, tn):
    k = a_pair[0].shape[1]
    m = res.shape[0]
    n = w.shape[2] // 2
    nj = n // tn
    n_a = a_pair[0].shape[0] // tm
    b3 = b.reshape(b.shape[0], 1, 2 * n)
    return pl.pallas_call(
        functools.partial(_glu_kernel, n_a=n_a),
        grid=(m // tm, nj),
        in_specs=_two_group_specs(tm, k, n_a) + [
            pl.BlockSpec((None, k, tn), lambda i, j: (layer, 0, j)),
            pl.BlockSpec((None, k, tn), lambda i, j: (layer, 0, j + nj)),
            pl.BlockSpec((None, 1, tn), lambda i, j: (layer, 0, j)),
            pl.BlockSpec((None, 1, tn), lambda i, j: (layer, 0, j + nj)),
            pl.BlockSpec((tm, tn), lambda i, j: (i, j)),
        ],
        out_specs=pl.BlockSpec((tm, tn), lambda i, j: (i, j)),
        out_shape=jax.ShapeDtypeStruct((m, n), F32),
        compiler_params=_params("parallel", "arbitrary"),
        name="glu",
    )(*a_pair, w, w, b3, b3, res)


def _final_norm_kernel(x_ref, g_ref, o_ref):
    o_ref[...] = _rms_norm(x_ref[...], g_ref[...])


def _final_norm(x, g, *, row0, m, tm):
    d = x.shape[1]
    blk0 = row0 // tm
    return pl.pallas_call(
        _final_norm_kernel,
        grid=(m // tm,),
        in_specs=[pl.BlockSpec((tm, d), lambda i: (blk0 + i, 0)), pl.BlockSpec((1, d), lambda i: (0, 0))],
        out_specs=pl.BlockSpec((tm, d), lambda i: (i, 0)),
        out_shape=jax.ShapeDtypeStruct((m, d), F32),
        compiler_params=_params("parallel"),
        name="final_norm",
    )(x, g.reshape(1, d))


def _head_layer_norm(x, g):
    xc = x - jnp.mean(x, axis=-1, keepdims=True)
    var = jnp.mean(xc * xc, axis=-1, keepdims=True)
    return xc * lax.rsqrt(var + EPS) * g


def _dot_nt(a, b):
    return lax.dot_general(a, b, (((1,), (1,)), ((), ())), preferred_element_type=F32)


def _dot_tn(a, b):
    return lax.dot_general(a, b, (((0,), (0,)), ((), ())), preferred_element_type=F32)


CONV_PAD = SUBLANES


def _ab_kernel(z_ref, zg_ref, cw_ref, cb_ref, gb_ref, nm_ref, nr_ref, cos_ref, sin_ref, dec_ref, rtab_ref, sdec_ref,
               c0_ref, n0_ref, m0_ref, conv0_ref, s0_ref,
               h_ref, c_ref, n_ref, m_ref, conv_ref, s_ref,
               xp_ref, *, lc, nb):
    @pl.when(pl.program_id(1) == 0)
    def _():
        c_ref[...] = c0_ref[...]
        n_ref[...] = n0_ref[...]
        m_ref[...] = m0_ref[...]
        s_ref[...] = s0_ref[...]
        xp_ref[:, CONV_PAD - (CONV_W - 1):CONV_PAD, :] = conv0_ref[...]

    for bi in range(nb):
        _ab_sequence(bi, z_ref, zg_ref, cw_ref, cb_ref, gb_ref, nm_ref, nr_ref, cos_ref, sin_ref, dec_ref, rtab_ref,
                     sdec_ref, h_ref, c_ref, n_ref, m_ref, conv_ref, s_ref, xp_ref, lc=lc)


def _ab_sequence(bi, z_ref, zg_ref, cw_ref, cb_ref, gb_ref, nm_ref, nr_ref, cos_ref, sin_ref, dec_ref, rtab_ref,
                 sdec_ref, h_ref, c_ref, n_ref, m_ref, conv_ref, s_ref, xp_ref, *, lc):
    rows = slice(bi * lc, (bi + 1) * lc)
    pad = CONV_PAD

    xp_ref[bi, pad:pad + lc, :] = z_ref[rows, 0:2 * W_HEADS]
    qk = cb_ref[...]
    for j in range(CONV_W):
        qk = qk + xp_ref[bi, pad - (CONV_W - 1) + j:pad - (CONV_W - 1) + j + lc, :] * cw_ref[j:j + 1, :]
    new_buf = xp_ref[bi, pad + lc - (CONV_W - 1):pad + lc, :]
    xp_ref[bi, pad - (CONV_W - 1):pad, :] = new_buf
    conv_ref[bi] = new_buf
    qk = qk * jax.nn.sigmoid(qk)

    row = lax.broadcasted_iota(jnp.int32, (lc, lc), 0)
    col = lax.broadcasted_iota(jnp.int32, (lc, lc), 1)
    causal = row >= col
    eye = row == col

    def row_of(c):
        return jnp.sum(jnp.where(eye, c, 0.0), axis=0, keepdims=True)

    gates = zg_ref[rows, :] + gb_ref[...]
    lf = jax.nn.log_sigmoid(gates)
    bcum = jnp.dot(causal.astype(F32), lf, preferred_element_type=F32, precision=lax.Precision.HIGHEST)

    scale = D_HEAD ** -0.5
    cos = cos_ref[...]
    sin = sin_ref[...]

    def rope(x):
        x1 = x[:, :D_HEAD // 2]
        x2 = x[:, D_HEAD // 2:]
        return jnp.concatenate([x1 * cos - x2 * sin, x1 * sin + x2 * cos], axis=-1)

    for h in range(N_HEADS):
        lo, hi = h * D_HEAD, (h + 1) * D_HEAD
        q = qk[:, lo:hi]
        k = qk[:, W_HEADS + lo:W_HEADS + hi] * scale
        v = z_ref[rows,2 * W_HEADS + lo:2 * W_HEADS + hi]
        og = z_ref[rows,3 * W_HEADS + lo:3 * W_HEADS + hi]
        qb, kb, vb = q.astype(BF16), k.astype(BF16), v.astype(BF16)
        c_prev = c_ref[bi, h]
        n_prev = n_ref[bi, h:h + 1, :]
        m_prev = m_ref[bi, :, h:h + 1]
        ig_c = gates[:, h:h + 1]
        b_c = bcum[:, N_HEADS + h:N_HEADS + h + 1]
        dlog = jnp.where(causal, b_c - row_of(b_c) + row_of(ig_c), -jnp.inf)
        s_log = b_c + m_prev
        m_row = jnp.maximum(s_log, jnp.max(dlog, axis=-1, keepdims=True))
        w = jnp.exp(dlog - m_row) * _dot_nt(qb, kb)
        sc = jnp.exp(s_log - m_row)
        num = sc * jnp.dot(qb, c_prev.astype(BF16), preferred_element_type=F32) + jnp.dot(
            w.astype(BF16), vb, preferred_element_type=F32)
        den = sc * jnp.sum(q * n_prev, axis=-1, keepdims=True) + jnp.sum(w, axis=-1, keepdims=True)
        hm = num / jnp.maximum(jnp.abs(den), jnp.exp(-m_row))
        m_new = m_row[lc - 1:lc, :]
        b_last = b_c[lc - 1:lc, :]
        w_state = jnp.exp(b_last + m_prev - m_new)
        kw = k * jnp.exp(b_last - b_c + ig_c - m_new)
        c_ref[bi, h] = w_state * c_prev + _dot_tn(kw.astype(BF16), vb)
        n_ref[bi, h:h + 1, :] = w_state * n_prev + jnp.sum(kw, axis=0, keepdims=True)
        m_ref[bi, :, h:h + 1] = m_new
        hm = _head_layer_norm(hm, nm_ref[:, lo:hi]) * jax.nn.sigmoid(og)
        h_ref[rows,lo:hi] = hm
        qr = rope(z_ref[rows,4 * W_HEADS + lo:4 * W_HEADS + hi])
        kr = rope(z_ref[rows,5 * W_HEADS + lo:5 * W_HEADS + hi]) * scale
        vr = z_ref[rows,6 * W_HEADS + lo:6 * W_HEADS + hi]
        gr = z_ref[rows,7 * W_HEADS + lo:7 * W_HEADS + hi]
        qrb, krb, vrb = qr.astype(BF16), kr.astype(BF16), vr.astype(BF16)
        s_prev = s_ref[bi, h]
        att = _dot_nt(qrb, krb) * dec_ref[h]
        inner = jnp.dot(att.astype(BF16), vrb, preferred_element_type=F32)
        cross = jnp.dot(qrb, s_prev.astype(BF16), preferred_element_type=F32) * rtab_ref[:, h:h + 1]
        krw = kr * rtab_ref[:, N_HEADS + h:N_HEADS + h + 1]
        s_ref[bi, h] = sdec_ref[0:1, h:h + 1] * s_prev + _dot_tn(krw.astype(BF16), vrb)
        yr = _head_layer_norm(inner + cross, nr_ref[:, lo:hi]) * (gr * jax.nn.sigmoid(gr))
        h_ref[rows,W_HEADS + lo:W_HEADS + hi] = yr


def _retention_tables(lc):
    lg = jnp.log1p(-(2.0 ** (-5.0 - jnp.arange(N_HEADS, dtype=F32))))
    idx = jnp.arange(lc, dtype=F32)
    diff = idx[:, None] - idx[None, :]
    decay = jnp.where(diff >= 0, jnp.exp(jnp.maximum(diff, 0.0)[None] * lg[:, None, None]), 0.0)
    cross = jnp.exp((idx[:, None] + 1.0) * lg[None, :])
    w_k = jnp.exp((lc - 1.0 - idx)[:, None] * lg[None, :])
    rtab = jnp.zeros((lc, GATE_COLS), F32).at[:, :N_HEADS].set(cross).at[:, N_HEADS:2 * N_HEADS].set(w_k)
    sdec = jnp.zeros((SUBLANES, GATE_COLS), F32).at[0, :N_HEADS].set(jnp.exp(lc * lg))
    return decay, rtab, sdec


def _rope_tables(pos):
    half = D_HEAD // 2
    freqs = ROPE_BASE ** (-jnp.arange(half, dtype=F32) / half)
    ang = pos[:, None] * freqs[None, :]
    return jnp.cos(ang), jnp.sin(ang)


def _mix_ab(z, zg, pos, conv0, c0, n0, m0, s0, conv_w, conv_b, gate_bias, norm_m, norm_r, *, bsz, length, row0, nb=1):
    lc = min(CHUNK, length)
    nc = length // lc
    assert nb == 1 or nc == 1
    rows = nb * lc
    blk0 = row0 // rows
    zw = z.shape[1]
    cos, sin = _rope_tables(pos)
    decay, rtab, sdec = _retention_tables(lc)
    full = lambda *shape: pl.BlockSpec(shape, lambda b, c: (0,) * len(shape))
    state4 = pl.BlockSpec((nb, N_HEADS, D_HEAD, D_HEAD), lambda b, c: (b, 0, 0, 0))
    state_n = pl.BlockSpec((nb, N_HEADS, D_HEAD), lambda b, c: (b, 0, 0))
    state_m = pl.BlockSpec((nb, 1, N_HEADS), lambda b, c: (b, 0, 0))
    state_conv = pl.BlockSpec((nb, CONV_W - 1, 2 * W_HEADS), lambda b, c: (b, 0, 0))
    outs = pl.pallas_call(
        functools.partial(_ab_kernel, lc=lc, nb=nb),
        grid=(bsz // nb, nc),
        in_specs=[
            pl.BlockSpec((rows, zw), lambda b, c: (blk0 + b * nc + c, 0)),
            pl.BlockSpec((rows, GATE_COLS), lambda b, c: (blk0 + b * nc + c, 0)),
            full(CONV_W, 2 * W_HEADS), full(1, 2 * W_HEADS), full(1, GATE_COLS),
            full(1, W_HEADS), full(1, W_HEADS),
            pl.BlockSpec((lc, D_HEAD // 2), lambda b, c: (c, 0)),
            pl.BlockSpec((lc, D_HEAD // 2), lambda b, c: (c, 0)),
            full(N_HEADS, lc, lc), full(lc, GATE_COLS), full(SUBLANES, GATE_COLS),
            state4, state_n, state_m, state_conv, state4,
        ],
        out_specs=[
            pl.BlockSpec((rows, 2 * W_HEADS), lambda b, c: (b * nc + c, 0)),
            state4, state_n, state_m, state_conv, state4,
        ],
        out_shape=[
            jax.ShapeDtypeStruct((bsz * length, 2 * W_HEADS), F32),
            jax.ShapeDtypeStruct((bsz, N_HEADS, D_HEAD, D_HEAD), F32),
            jax.ShapeDtypeStruct((bsz, N_HEADS, D_HEAD), F32),
            jax.ShapeDtypeStruct((bsz, 1, N_HEADS), F32),
            jax.ShapeDtypeStruct((bsz, CONV_W - 1, 2 * W_HEADS), F32),
            jax.ShapeDtypeStruct((bsz, N_HEADS, D_HEAD, D_HEAD), F32),
        ],
        scratch_shapes=[pltpu.VMEM((nb, lc + CONV_PAD, 2 * W_HEADS), F32)],
        compiler_params=_params("parallel", "arbitrary"),
        name="mix_ab",
    )(z, zg, conv_w, conv_b.reshape(1, -1), gate_bias, norm_m.reshape(1, -1), norm_r.reshape(1, -1),
      cos, sin, decay, rtab, sdec, c0, n0, m0.reshape(bsz, 1, N_HEADS), conv0, s0)
    h, c_new, n_new, m_new, conv_new, s_new = outs
    return h, conv_new, c_new, n_new, m_new.reshape(bsz, N_HEADS), s_new


def _s5_prep_kernel(lr_ref, li_ref, ldt_ref, bre_ref, bim_ref, expand_ref, are_ref, aim_ref, bbre_ref, bbim_ref):
    lr = lr_ref[...]
    li = li_ref[...]
    dt = jnp.exp(ldt_ref[...])
    mag = jnp.exp(lr * dt)
    a_re = mag * jnp.cos(li * dt)
    a_im = mag * jnp.sin(li * dt)
    denom = lr * lr + li * li
    g_re = ((a_re - 1.0) * lr + a_im * li) / denom
    g_im = (a_im * lr - (a_re - 1.0) * li) / denom
    are_ref[...] = a_re
    aim_ref[...] = a_im
    ge_re = jnp.dot(g_re, expand_ref[...], preferred_element_type=F32, precision=lax.Precision.HIGHEST)
    ge_im = jnp.dot(g_im, expand_ref[...], preferred_element_type=F32, precision=lax.Precision.HIGHEST)
    bre = bre_ref[...]
    bim = bim_ref[...]
    bbre_ref[...] = ge_re * bre - ge_im * bim
    bbim_ref[...] = ge_re * bim + ge_im * bre


def _s5_prep(lam_re, lam_im, log_dt, b_re, b_im):
    g, p = lam_re.shape
    expand = jnp.repeat(jnp.eye(p, dtype=F32), S5_GROUP, axis=1)
    return pl.pallas_call(
        _s5_prep_kernel,
        out_shape=[
            jax.ShapeDtypeStruct((g, p), F32), jax.ShapeDtypeStruct((g, p), F32),
            jax.ShapeDtypeStruct((g, p * S5_GROUP), F32), jax.ShapeDtypeStruct((g, p * S5_GROUP), F32),
        ],
        compiler_params=pltpu.CompilerParams(vmem_limit_bytes=VMEM_LIMIT_BYTES),
        name="s5_prep",
    )(lam_re, lam_im, log_dt.reshape(g, 1), b_re.reshape(g, p * S5_GROUP), b_im.reshape(g, p * S5_GROUP), expand)


S5_COLS = GROUPS_PER_BLOCK * S5_GROUP
S5_HALF = GROUPS_PER_BLOCK * S5_STATES
S5_TILES = S5_HALF // LANES
S5_COL_TILES = S5_COLS // LANES
S5_TILE_COLS = 2 * LANES


def _tile_major(re, im):
    lead = re.shape[:-1]
    pair = jnp.stack([re.reshape(lead + (S5_TILES, LANES)), im.reshape(lead + (S5_TILES, LANES))], axis=-2)
    return pair.reshape(lead + (2 * S5_HALF,))


def _s5_scan_dense(bu_ref, xn_ref, a_ref, nb, tt):
    a_re = [jnp.broadcast_to(a_ref[0, :, (2 * j) * LANES:(2 * j + 1) * LANES], (SUBLANES, LANES))
            for j in range(S5_TILES)]
    a_im = [jnp.broadcast_to(a_ref[0, :, (2 * j + 1) * LANES:(2 * j + 2) * LANES], (SUBLANES, LANES))
            for j in range(S5_TILES)]

    def group(s, _):
        r0 = pl.multiple_of(s * SUBLANES, SUBLANES)
        for j in range(S5_TILES):
            re_cols = slice((2 * j) * LANES, (2 * j + 1) * LANES)
            im_cols = slice((2 * j + 1) * LANES, (2 * j + 2) * LANES)
            xr = xn_ref[pl.ds(r0, SUBLANES), re_cols]
            xi = xn_ref[pl.ds(r0, SUBLANES), im_cols]
            for t in range(tt):
                rows = pl.ds(t * nb + r0, SUBLANES)
                xr, xi = (a_re[j] * xr - a_im[j] * xi + bu_ref[2 * j, rows, :],
                          a_re[j] * xi + a_im[j] * xr + bu_ref[2 * j + 1, rows, :])
                bu_ref[2 * j, rows, :] = xr
                bu_ref[2 * j + 1, rows, :] = xi
            xn_ref[pl.ds(r0, SUBLANES), re_cols] = xr
            xn_ref[pl.ds(r0, SUBLANES), im_cols] = xi
        return 0

    lax.fori_loop(0, nb // SUBLANES, group, 0)


def _s5_scan_pairs(bu, state_re, state_im, a_re, a_im, nb):
    upper = lax.broadcasted_iota(jnp.int32, (SUBLANES, LANES), 0) >= nb
    ar = jnp.broadcast_to(a_re, (SUBLANES, LANES))
    ai = jnp.broadcast_to(a_im, (SUBLANES, LANES))
    c1_re = jnp.where(upper, ar * ar - ai * ai, ar)
    c1_im = jnp.where(upper, 2.0 * ar * ai, ai)
    c2_re = jnp.where(upper, ar, 0.0)
    c2_im = jnp.where(upper, ai, 0.0)

    def both_halves(x):
        return jnp.where(upper, x, pltpu.roll(x, nb, axis=0))

    pr = jnp.where(upper, pltpu.roll(state_re, nb, axis=0), state_re)
    pi = jnp.where(upper, pltpu.roll(state_im, nb, axis=0), state_im)
    out = []
    for i in range(bu.shape[0] // SUBLANES):
        vr = bu[i * SUBLANES:(i + 1) * SUBLANES, :LANES]
        vi = bu[i * SUBLANES:(i + 1) * SUBLANES, LANES:]
        rr = pltpu.roll(vr, nb, axis=0)
        ri = pltpu.roll(vi, nb, axis=0)
        yr = c1_re * pr - c1_im * pi + (c2_re * rr - c2_im * ri) + vr
        yi = c1_re * pi + c1_im * pr + (c2_re * ri + c2_im * rr) + vi
        out.append(jnp.concatenate([yr, yi], axis=-1))
        pr, pi = both_halves(yr), both_halves(yi)
    return jnp.concatenate(out, axis=0), pr, pi


def _s5_kernel(*refs, nb, tt, interleave):
    n_u = nb * S5_COL_TILES if interleave else 1
    u_refs = refs[:n_u]
    bb_ref, cc_ref, a_ref, d_ref, x0_ref, y_ref, xn_ref = refs[n_u:n_u + 7]

    @pl.when(pl.program_id(1) == 0)
    def _():
        xn_ref[...] = x0_ref[...]

    if interleave:
        assert 2 * nb == SUBLANES
        ut_ref, yt_ref = refs[n_u + 7:]
        for b in range(nb):
            for c in range(S5_COL_TILES):
                ut_ref[c, pl.ds(b, tt, stride=nb), :] = u_refs[b * S5_COL_TILES + c][...]
        u = jnp.concatenate([ut_ref[c] for c in range(S5_COL_TILES)], axis=-1)
        ub = u.astype(BF16)
        y = d_ref[...] * u
        for j in range(S5_TILES):
            cols = slice(j * S5_TILE_COLS, (j + 1) * S5_TILE_COLS)
            re_cols = slice(j * S5_TILE_COLS, j * S5_TILE_COLS + LANES)
            im_cols = slice(j * S5_TILE_COLS + LANES, (j + 1) * S5_TILE_COLS)
            bu = jnp.dot(ub, bb_ref[0, :, cols], preferred_element_type=F32)
            x, fin_re, fin_im = _s5_scan_pairs(bu, xn_ref[:, re_cols], xn_ref[:, im_cols],
                                               a_ref[0, :, re_cols], a_ref[0, :, im_cols], nb)
            xn_ref[:, re_cols] = fin_re
            xn_ref[:, im_cols] = fin_im
            y = y + jnp.dot(x.astype(BF16), cc_ref[0, cols, :], preferred_element_type=F32)
    else:
        assert nb % SUBLANES == 0
        (bu_ref,) = refs[n_u + 7:]
        u = u_refs[0][...]
        bu = jnp.dot(u.astype(BF16), bb_ref[0], preferred_element_type=F32)
        for j in range(2 * S5_TILES):
            bu_ref[j] = bu[:, j * LANES:(j + 1) * LANES]
        _s5_scan_dense(bu_ref, xn_ref, a_ref, nb, tt)
        x = jnp.concatenate([bu_ref[j] for j in range(2 * S5_TILES)], axis=-1)
        y = jnp.dot(x.astype(BF16), cc_ref[0], preferred_element_type=F32) + d_ref[...] * u
    y = jax.nn.gelu(y)
    if interleave:
        for c in range(S5_COL_TILES):
            yt_ref[c] = y[:, c * LANES:(c + 1) * LANES]
        for b in range(nb):
            for c in range(S5_COL_TILES):
                y_ref[b, :, c * LANES:(c + 1) * LANES] = yt_ref[c, pl.ds(b, tt, stride=nb), :].astype(BF16)
    else:
        y_ref[...] = y.astype(BF16)


def _mix_s5(u, x0_re, x0_im, bb_blk, cc_blk, a_blk, d_skip, *, bsz, length, tt, row0, interleave):
    w = u.shape[-1]
    nblk = w // S5_COLS
    nt = length // tt
    x0 = _tile_major(x0_re.reshape(bsz, nblk, S5_HALF), x0_im.reshape(bsz, nblk, S5_HALF))
    srows = max(bsz, SUBLANES)
    x0 = jnp.pad(x0.transpose(1, 0, 2), ((0, 0), (0, srows - bsz), (0, 0)))
    rows = bsz * tt
    if interleave:
        blk0 = row0 // tt
        u_specs = [pl.BlockSpec((tt, LANES), functools.partial(
            lambda g, t, b, c: (blk0 + b * nt + t, g * S5_COL_TILES + c), b=b, c=c))
            for b in range(bsz) for c in range(S5_COL_TILES)]
        u_args = [u] * (bsz * S5_COL_TILES)
        y_spec = pl.BlockSpec((bsz, tt, S5_COLS), lambda g, t: (0, t, g))
        y_shape = jax.ShapeDtypeStruct((bsz, length, w), BF16)
        scratch = [pltpu.VMEM((S5_COL_TILES, rows, LANES), F32), pltpu.VMEM((S5_COL_TILES, rows, LANES), F32)]
    else:
        assert nt == 1
        blk0 = row0 // rows
        u_specs = [pl.BlockSpec((rows, S5_COLS), lambda g, t: (blk0, g))]
        u_args = [u]
        y_spec = pl.BlockSpec((rows, S5_COLS), lambda g, t: (0, g))
        y_shape = jax.ShapeDtypeStruct((rows, w), BF16)
        scratch = [pltpu.VMEM((2 * S5_TILES, rows, LANES), F32)]
    y, xn = pl.pallas_call(
        functools.partial(_s5_kernel, nb=bsz, tt=tt, interleave=interleave),
        grid=(nblk, nt),
        in_specs=u_specs + [
            pl.BlockSpec((1, S5_COLS, 2 * S5_HALF), lambda g, t: (g, 0, 0)),
            pl.BlockSpec((1, 2 * S5_HALF, S5_COLS), lambda g, t: (g, 0, 0)),
            pl.BlockSpec((1, 1, 2 * S5_HALF), lambda g, t: (g, 0, 0)),
            pl.BlockSpec((1, S5_COLS), lambda g, t: (0, g)),
            pl.BlockSpec((None, srows, 2 * S5_HALF), lambda g, t: (g, 0, 0)),
        ],
        out_specs=[y_spec, pl.BlockSpec((None, srows, 2 * S5_HALF), lambda g, t: (g, 0, 0))],
        out_shape=[y_shape, jax.ShapeDtypeStruct((nblk, srows, 2 * S5_HALF), F32)],
        scratch_shapes=scratch,
        compiler_params=_params("parallel", "arbitrary"),
        name="mix_s5",
    )(*u_args, bb_blk, cc_blk, a_blk, d_skip.reshape(1, w), x0)
    xn = xn[:, :bsz].transpose(1, 0, 2).reshape(bsz, nblk, S5_TILES, 2, LANES)
    g_total = nblk * GROUPS_PER_BLOCK
    xr = xn[..., 0, :].reshape(bsz, g_total, S5_STATES)
    xi = xn[..., 1, :].reshape(bsz, g_total, S5_STATES)
    return y, xr, xi


def _block_diag(blocks, per):
    g, r, c = blocks.shape
    b = blocks.reshape(g // per, per, r, c)
    eye = jnp.eye(per, dtype=blocks.dtype)
    return jnp.einsum('nirc,ij->nirjc', b, eye).reshape(g // per, per * r, per * c)


def _s5_block_weights(a_re, a_im, bb_re, bb_im, c_re, c_im):
    g, p = a_re.shape
    nblk = g // GROUPS_PER_BLOCK
    to_cp = lambda m: m.reshape(g, p, S5_GROUP).transpose(0, 2, 1)
    bb = _tile_major(_block_diag(to_cp(bb_re), GROUPS_PER_BLOCK), _block_diag(to_cp(bb_im), GROUPS_PER_BLOCK))
    cc_re = _block_diag(c_re.transpose(0, 2, 1), GROUPS_PER_BLOCK)
    cc_im = _block_diag(-c_im.transpose(0, 2, 1), GROUPS_PER_BLOCK)
    cc = jnp.swapaxes(_tile_major(jnp.swapaxes(cc_re, 1, 2), jnp.swapaxes(cc_im, 1, 2)), 1, 2)
    a_blk = _tile_major(a_re.reshape(nblk, S5_HALF), a_im.reshape(nblk, S5_HALF)).reshape(nblk, 1, 2 * S5_HALF)
    return bb.astype(BF16), cc.astype(BF16), a_blk


def kernel(x_prompt, x_sample, state_mlstm_C, state_mlstm_n, state_mlstm_m, state_mlstm_conv, state_ret_S, state_s5_re, state_s5_im, norm_ffn1, norm_mix, norm_ffn2, norm_final, ffn1_w_gate, ffn1_w_up, ffn1_w_down, ffn2_w_gate, ffn2_w_up, ffn2_w_down, ab_w_in, mlstm_b_i, mlstm_b_f, mlstm_conv_w, mlstm_conv_b, mlstm_norm, ret_norm, ab_w_out, s5_w_in, s5_lambda_re, s5_lambda_im, s5_log_dt, s5_B_re, s5_B_im, s5_C_re, s5_C_im, s5_D, s5_w_glu, s5_b_glu):
    bp, lp, d = x_prompt.shape
    bs, ls, _ = x_sample.shape
    mp, ms = bp * lp, bs * ls
    depth = norm_ffn1.shape[0]
    assert mp % TOKEN_TILE == 0 and ms % TOKEN_TILE == 0

    ffn = functools.partial(_ffn, tm=TOKEN_TILE, tf=FFN_TILE)
    pos_p = jnp.arange(lp, dtype=F32)
    pos_s = PAST_LEN + jnp.arange(ls, dtype=F32)

    ys = (x_prompt.reshape(mp, d), x_sample.reshape(ms, d))
    out_mc, out_mn, out_mm, out_conv, out_rs, out_re, out_im = ([] for _ in range(7))
    for layer in range(depth):
        y = ffn(ys, norm_ffn1, ffn1_w_gate, ffn1_w_up, ffn1_w_down, layer)
        if layer % 2 == 0:
            e = layer // 2
            n_gate = 2 * N_HEADS
            wt = jnp.swapaxes(ab_w_in, 1, 2)[e]
            wt_hi = wt[4 * W_HEADS + n_gate:]
            z, zg = _ab_proj(y, norm_mix, wt, wt_hi, layer, tm=AB_PROJ_ROWS, tn=PROJ_TILE)
            gate_bias = jnp.zeros((1, GATE_COLS), F32).at[0, :N_HEADS].set(mlstm_b_i[e]).at[
                0, N_HEADS:n_gate].set(mlstm_b_f[e])
            common = (mlstm_conv_w[e], mlstm_conv_b[e], gate_bias, mlstm_norm[e], ret_norm[e])
            hp, conv_p, c_p, n_p, m_p, s_p = _mix_ab(
                z, zg, pos_p, jnp.zeros((bp, CONV_W - 1, 2 * W_HEADS), F32),
                jnp.zeros((bp, N_HEADS, D_HEAD, D_HEAD), F32), jnp.zeros((bp, N_HEADS, D_HEAD), F32),
                jnp.full((bp, N_HEADS), M_INIT, F32), jnp.zeros((bp, N_HEADS, D_HEAD, D_HEAD), F32),
                *common, bsz=bp, length=lp, row0=0)
            hs, conv_s, c_s, n_s, m_s, s_s = _mix_ab(
                z, zg, pos_s, state_mlstm_conv[e], state_mlstm_C[e], state_mlstm_n[e], state_mlstm_m[e],
                state_ret_S[e], *common, bsz=bs, length=ls, row0=mp, nb=SAMPLE_BATCH_TILE)
            out_mc.append((c_p, c_s))
            out_mn.append((n_p, n_s))
            out_mm.append((m_p, m_s))
            out_conv.append((conv_p, conv_s))
            out_rs.append((s_p, s_s))
            y = _proj_res((hp, hs), ab_w_out[e], y, tm=TOKEN_TILE // 4)
        else:
            o = layer // 2
            u = _norm_proj(y, norm_mix, s5_w_in[o], layer, tm=TOKEN_TILE // 2)
            a_re, a_im, bb_re, bb_im = _s5_prep(s5_lambda_re[o], s5_lambda_im[o], s5_log_dt[o], s5_B_re[o], s5_B_im[o])
            bb, cc, a_blk = _s5_block_weights(a_re, a_im, bb_re, bb_im, s5_C_re[o], s5_C_im[o])
            zeros_p = jnp.zeros((bp,) + a_re.shape, F32)
            ya_p, re_p, im_p = _mix_s5(u, zeros_p, zeros_p, bb, cc, a_blk, s5_D[o], bsz=bp, length=lp,
                                       tt=S5_TIME_TILE, row0=0, interleave=True)
            u_s = u[mp:].reshape(bs, ls, -1).transpose(1, 0, 2).reshape(ms, -1)
            ya_s, re_s, im_s = _mix_s5(u_s, state_s5_re[o], state_s5_im[o], bb, cc, a_blk, s5_D[o], bsz=bs,
                                       length=ls, tt=ls, row0=0, interleave=False)
            ya_s = ya_s.reshape(ls, bs, -1).transpose(1, 0, 2).reshape(ms, -1)
            out_re.append((re_p, re_s))
            out_im.append((im_p, im_s))
            y = _glu((ya_p.reshape(mp, -1), ya_s), s5_w_glu, s5_b_glu, y, o, tm=TOKEN_TILE, tn=PROJ_TILE)
        ys = (ffn((y,), norm_ffn2, ffn2_w_gate, ffn2_w_up, ffn2_w_down, layer),)
    y = ys[0]
    y_p = _final_norm(y, norm_final, row0=0, m=mp, tm=TOKEN_TILE)
    y_s = _final_norm(y, norm_final, row0=mp, m=ms, tm=TOKEN_TILE)

    def both(pairs):
        return jnp.stack([p for p, _ in pairs]), jnp.stack([s for _, s in pairs])

    pc, sc = both(out_mc)
    pn, sn = both(out_mn)
    pm, sm = both(out_mm)
    pconv, sconv = both(out_conv)
    ps, ss = both(out_rs)
    pre, sre = both(out_re)
    pim, sim = both(out_im)
    return (y_p.reshape(bp, lp, d), y_s.reshape(bs, ls, d), pc, sc, pn, sn, pm, sm, pconv, sconv, ps, ss,
            pre, sre, pim, sim)
```

```python
import functools

import jax
import jax.numpy as jnp
from jax import lax
from jax.experimental import pallas as pl
from jax.experimental.pallas import tpu as pltpu

F32 = jnp.float32
BF16 = jnp.bfloat16

EPS = 1e-6
M_INIT = -1e30
CHUNK = 128
CONV_W = 4
ROPE_BASE = 10000.0
PAST_LEN = 16384
N_HEADS = 4
D_HEAD = 256
W_HEADS = N_HEADS * D_HEAD
S5_GROUP = 16
S5_STATES = 64
GROUPS_PER_BLOCK = 16
LANES = 128
SUBLANES = 8
GATE_COLS = LANES

VMEM_LIMIT_BYTES = 60 * 1024 * 1024

TOKEN_TILE = 1024
FFN_TILE = 256
PROJ_TILE = 512
AB_PROJ_ROWS = 1536
S5_TIME_TILE = 128
SAMPLE_BATCH_TILE = 4


def _params(*semantics):
    return pltpu.CompilerParams(dimension_semantics=semantics, vmem_limit_bytes=VMEM_LIMIT_BYTES)


def _single_buffered(shape, index_map):
    return pl.BlockSpec(shape, index_map, pipeline_mode=pl.Buffered(1))


def _rms_norm(x, g):
    return x * lax.rsqrt(jnp.mean(x * x, axis=-1, keepdims=True) + EPS) * g


def _two_group_specs(tm, width, n_a, single=False):
    mk = _single_buffered if single else pl.BlockSpec
    return [mk((tm, width), lambda i, *_: (jnp.minimum(i, n_a - 1), 0)),
            mk((tm, width), lambda i, *_: (jnp.maximum(i - n_a, 0), 0))]


def _for_row_group(i, n_a, a_ref, b_ref, fn):
    @pl.when(i < n_a)
    def _():
        fn(a_ref[...])

    @pl.when(i >= n_a)
    def _():
        fn(b_ref[...])


def _ffn_kernel(*refs, n_a):
    if n_a is None:
        x_ref, g_ref, wg_ref, wu_ref, wd_ref, o_ref, xn_ref = refs
    else:
        xa_ref, xb_ref, g_ref, wg_ref, wu_ref, wd_ref, o_ref, xn_ref = refs
    f = pl.program_id(1)

    def init(x):
        xn_ref[...] = _rms_norm(x, g_ref[...]).astype(BF16)
        o_ref[...] = 2.0 * x

    @pl.when(f == 0)
    def _():
        if n_a is None:
            init(x_ref[...])
        else:
            _for_row_group(pl.program_id(0), n_a, xa_ref, xb_ref, init)

    xn = xn_ref[...]
    a = jnp.dot(xn, wg_ref[...].astype(BF16), preferred_element_type=F32)
    b = jnp.dot(xn, wu_ref[...].astype(BF16), preferred_element_type=F32)
    h = (a * jax.nn.sigmoid(a) * b).astype(BF16)
    o_ref[...] += jnp.dot(h, wd_ref[...].astype(BF16), preferred_element_type=F32)

    @pl.when(f == pl.num_programs(1) - 1)
    def _():
        o_ref[...] = 0.5 * o_ref[...]


def _ffn(xs, g, w_gate, w_up, w_down, layer, *, tm, tf):
    d = xs[0].shape[1]
    m = sum(x.shape[0] for x in xs)
    dff = w_gate.shape[2]
    if len(xs) == 1:
        n_a = None
        x_specs = [pl.BlockSpec((tm, d), lambda i, f: (i, 0))]
    else:
        n_a = xs[0].shape[0] // tm
        x_specs = _two_group_specs(tm, d, n_a, single=True)
    return pl.pallas_call(
        functools.partial(_ffn_kernel, n_a=n_a),
        grid=(m // tm, dff // tf),
        in_specs=x_specs + [
            pl.BlockSpec((None, 1, d), lambda i, f: (layer, 0, 0)),
            pl.BlockSpec((None, d, tf), lambda i, f: (layer, 0, f)),
            pl.BlockSpec((None, d, tf), lambda i, f: (layer, 0, f)),
            pl.BlockSpec((None, tf, d), lambda i, f: (layer, f, 0)),
        ],
        out_specs=pl.BlockSpec((tm, d), lambda i, f: (i, 0)),
        out_shape=jax.ShapeDtypeStruct((m, d), F32),
        scratch_shapes=[pltpu.VMEM((tm, d), BF16)],
        compiler_params=_params("parallel", "arbitrary"),
        name="ffn",
    )(*xs, g.reshape(g.shape[0], 1, d), w_gate, w_up, w_down)


def _ab_proj_kernel(x_ref, g_ref, wlo_ref, whi_ref, wgate_ref, z_ref, zg_ref, xn_ref, *, n_lo):
    j = pl.program_id(1)

    @pl.when(j == 0)
    def _():
        xn_ref[...] = _rms_norm(x_ref[...], g_ref[...]).astype(BF16)
        zg_ref[...] = _dot_nt(xn_ref[...], wgate_ref[...].astype(BF16))

    @pl.when(j < n_lo)
    def _():
        z_ref[...] = _dot_nt(xn_ref[...], wlo_ref[...].astype(BF16))

    @pl.when(j >= n_lo)
    def _():
        z_ref[...] = _dot_nt(xn_ref[...], whi_ref[...].astype(BF16))


def _ab_proj(x, g, wt, wt_hi, layer, *, tm, tn):
    m, d = x.shape
    n_lo = 4 * W_HEADS // tn
    n_hi = wt_hi.shape[0] // tn
    gate_blk = 4 * W_HEADS // GATE_COLS
    return pl.pallas_call(
        functools.partial(_ab_proj_kernel, n_lo=n_lo),
        grid=(m // tm, n_lo + n_hi),
        in_specs=[
            pl.BlockSpec((tm, d), lambda i, j: (i, 0)),
            pl.BlockSpec((None, 1, d), lambda i, j: (layer, 0, 0)),
            pl.BlockSpec((tn, d), lambda i, j: (jnp.minimum(j, n_lo - 1), 0)),
            pl.BlockSpec((tn, d), lambda i, j: (jnp.maximum(j - n_lo, 0), 0)),
            pl.BlockSpec((GATE_COLS, d), lambda i, j: (gate_blk, 0)),
        ],
        out_specs=[
            pl.BlockSpec((tm, tn), lambda i, j: (i, j)),
            pl.BlockSpec((tm, GATE_COLS), lambda i, j: (i, 0)),
        ],
        out_shape=[
            jax.ShapeDtypeStruct((m, (n_lo + n_hi) * tn), F32),
            jax.ShapeDtypeStruct((m, GATE_COLS), F32),
        ],
        scratch_shapes=[pltpu.VMEM((tm, d), BF16)],
        compiler_params=_params("parallel", "arbitrary"),
        name="ab_proj",
    )(x, g.reshape(g.shape[0], 1, d), wt, wt_hi, wt)


def _norm_proj_kernel(x_ref, g_ref, w_ref, o_ref, wb_ref):
    @pl.when(pl.program_id(0) == 0)
    def _():
        wb_ref[...] = w_ref[...].astype(BF16)

    xn = _rms_norm(x_ref[...], g_ref[...]).astype(BF16)
    o_ref[...] = jnp.dot(xn, wb_ref[...], preferred_element_type=F32)


def _norm_proj(x, g, w, layer, *, tm):
    m, d = x.shape
    n = w.shape[1]
    return pl.pallas_call(
        _norm_proj_kernel,
        grid=(m // tm,),
        in_specs=[
            pl.BlockSpec((tm, d), lambda i: (i, 0)),
            pl.BlockSpec((None, 1, d), lambda i: (layer, 0, 0)),
            _single_buffered((d, n), lambda i: (0, 0)),
        ],
        out_specs=pl.BlockSpec((tm, n), lambda i: (i, 0)),
        out_shape=jax.ShapeDtypeStruct((m, n), F32),
        scratch_shapes=[pltpu.VMEM((d, n), BF16)],
        compiler_params=_params("arbitrary"),
        name="norm_proj",
    )(x, g.reshape(g.shape[0], 1, d), w)


def _proj_res_kernel(aa_ref, ab_ref, w_ref, r_ref, o_ref, wb_ref, *, n_a):
    i = pl.program_id(0)

    @pl.when(i == 0)
    def _():
        wb_ref[...] = w_ref[...].astype(BF16)

    def run(a):
        o_ref[...] = r_ref[...] + jnp.dot(a.astype(BF16), wb_ref[...], preferred_element_type=F32)

    _for_row_group(i, n_a, aa_ref, ab_ref, run)


def _proj_res(a_pair, w, res, *, tm):
    k = a_pair[0].shape[1]
    m = res.shape[0]
    n = w.shape[1]
    n_a = a_pair[0].shape[0] // tm
    return pl.pallas_call(
        functools.partial(_proj_res_kernel, n_a=n_a),
        grid=(m // tm,),
        in_specs=_two_group_specs(tm, k, n_a) + [
            _single_buffered((k, n), lambda i: (0, 0)),
            pl.BlockSpec((tm, n), lambda i: (i, 0)),
        ],
        out_specs=pl.BlockSpec((tm, n), lambda i: (i, 0)),
        out_shape=jax.ShapeDtypeStruct((m, n), F32),
        scratch_shapes=[pltpu.VMEM((k, n), BF16)],
        compiler_params=_params("arbitrary"),
        name="proj_res",
    )(*a_pair, w, res)


def _glu_kernel(aa_ref, ab_ref, wv_ref, wg_ref, bv_ref, bg_ref, r_ref, o_ref, *, n_a):
    def run(a):
        v = jnp.dot(a, wv_ref[...].astype(BF16), preferred_element_type=F32) + bv_ref[...]
        t = jnp.dot(a, wg_ref[...].astype(BF16), preferred_element_type=F32) + bg_ref[...]
        o_ref[...] = r_ref[...] + v * jax.nn.sigmoid(t)

    _for_row_group(pl.program_id(0), n_a, aa_ref, ab_ref, run)


def _glu(a_pair, w, b, res, layer, *, tm, tn):
    k = a_pair[0].shape[1]
    m = res.shape[0]
    n = w.shape[2] // 2
    nj = n // tn
    n_a = a_pair[0].shape[0] // tm
    b3 = b.reshape(b.shape[0], 1, 2 * n)
    return pl.pallas_call(
        functools.partial(_glu_kernel, n_a=n_a),
        grid=(m // tm, nj),
        in_specs=_two_group_specs(tm, k, n_a) + [
            pl.BlockSpec((None, k, tn), lambda i, j: (layer, 0, j)),
            pl.BlockSpec((None, k, tn), lambda i, j: (layer, 0, j + nj)),
            pl.BlockSpec((None, 1, tn), lambda i, j: (layer, 0, j)),
            pl.BlockSpec((None, 1, tn), lambda i, j: (layer, 0, j + nj)),
            pl.BlockSpec((tm, tn), lambda i, j: (i, j)),
        ],
        out_specs=pl.BlockSpec((tm, tn), lambda i, j: (i, j)),
        out_shape=jax.ShapeDtypeStruct((m, n), F32),
        compiler_params=_params("parallel", "arbitrary"),
        name="glu",
    )(*a_pair, w, w, b3, b3, res)


def _final_norm_kernel(x_ref, g_ref, o_ref):
    o_ref[...] = _rms_norm(x_ref[...], g_ref[...])


def _final_norm(x, g, *, row0, m, tm):
    d = x.shape[1]
    blk0 = row0 // tm
    return pl.pallas_call(
        _final_norm_kernel,
        grid=(m // tm,),
        in_specs=[pl.BlockSpec((tm, d), lambda i: (blk0 + i, 0)), pl.BlockSpec((1, d), lambda i: (0, 0))],
        out_specs=pl.BlockSpec((tm, d), lambda i: (i, 0)),
        out_shape=jax.ShapeDtypeStruct((m, d), F32),
        compiler_params=_params("parallel"),
        name="final_norm",
    )(x, g.reshape(1, d))


def _head_layer_norm(x, g):
    xc = x - jnp.mean(x, axis=-1, keepdims=True)
    var = jnp.mean(xc * xc, axis=-1, keepdims=True)
    return xc * lax.rsqrt(var + EPS) * g


def _cumsum_rows(x, n):
    row = lax.broadcasted_iota(jnp.int32, x.shape, 0)
    k = 1
    while k < n:
        x = x + jnp.where(row >= k, pltpu.roll(x, k, axis=0), 0.0)
        k *= 2
    return x


def _dot_nt(a, b):
    return lax.dot_general(a, b, (((1,), (1,)), ((), ())), preferred_element_type=F32)


def _dot_tn(a, b):
    return lax.dot_general(a, b, (((0,), (0,)), ((), ())), preferred_element_type=F32)


CONV_PAD = SUBLANES


def _ab_kernel(z_ref, zg_ref, cw_ref, cb_ref, gb_ref, nm_ref, nr_ref, cos_ref, sin_ref, dec_ref, rtab_ref, sdec_ref,
               c0_ref, n0_ref, m0_ref, conv0_ref, s0_ref,
               h_ref, c_ref, n_ref, m_ref, conv_ref, s_ref,
               xp_ref, *, lc, nb):
    @pl.when(pl.program_id(1) == 0)
    def _():
        c_ref[...] = c0_ref[...]
        n_ref[...] = n0_ref[...]
        m_ref[...] = m0_ref[...]
        s_ref[...] = s0_ref[...]
        xp_ref[:, CONV_PAD - (CONV_W - 1):CONV_PAD, :] = conv0_ref[...]

    for bi in range(nb):
        _ab_sequence(bi, z_ref, zg_ref, cw_ref, cb_ref, gb_ref, nm_ref, nr_ref, cos_ref, sin_ref, dec_ref, rtab_ref,
                     sdec_ref, h_ref, c_ref, n_ref, m_ref, conv_ref, s_ref, xp_ref, lc=lc)


def _ab_sequence(bi, z_ref, zg_ref, cw_ref, cb_ref, gb_ref, nm_ref, nr_ref, cos_ref, sin_ref, dec_ref, rtab_ref,
                 sdec_ref, h_ref, c_ref, n_ref, m_ref, conv_ref, s_ref, xp_ref, *, lc):
    rows = slice(bi * lc, (bi + 1) * lc)
    pad = CONV_PAD

    xp_ref[bi, pad:pad + lc, :] = z_ref[rows, 0:2 * W_HEADS]
    qk = cb_ref[...]
    for j in range(CONV_W):
        qk = qk + xp_ref[bi, pad - (CONV_W - 1) + j:pad - (CONV_W - 1) + j + lc, :] * cw_ref[j:j + 1, :]
    new_buf = xp_ref[bi, pad + lc - (CONV_W - 1):pad + lc, :]
    xp_ref[bi, pad - (CONV_W - 1):pad, :] = new_buf
    conv_ref[bi] = new_buf
    qk = qk * jax.nn.sigmoid(qk)

    row = lax.broadcasted_iota(jnp.int32, (lc, lc), 0)
    col = lax.broadcasted_iota(jnp.int32, (lc, lc), 1)
    causal = row >= col
    eye = row == col
    short = 2 * N_HEADS * lc <= LANES

    def row_of(c):
        return jnp.sum(jnp.where(eye, c, 0.0), axis=0, keepdims=True)

    gates = zg_ref[rows, :] + gb_ref[...]
    lf = jax.nn.log_sigmoid(gates)
    if short:
        bcum = _cumsum_rows(lf, lc)
    else:
        bcum = jnp.dot(causal.astype(F32), lf, preferred_element_type=F32, precision=lax.Precision.HIGHEST)
    updates = []

    scale = D_HEAD ** -0.5
    cos = cos_ref[...]
    sin = sin_ref[...]

    def rope(x):
        x1 = x[:, :D_HEAD // 2]
        x2 = x[:, D_HEAD // 2:]
        return jnp.concatenate([x1 * cos - x2 * sin, x1 * sin + x2 * cos], axis=-1)

    for h in range(N_HEADS):
        lo, hi = h * D_HEAD, (h + 1) * D_HEAD
        q = qk[:, lo:hi]
        k = qk[:, W_HEADS + lo:W_HEADS + hi] * scale
        v = z_ref[rows,2 * W_HEADS + lo:2 * W_HEADS + hi]
        og = z_ref[rows,3 * W_HEADS + lo:3 * W_HEADS + hi]
        qb, kb, vb = q.astype(BF16), k.astype(BF16), v.astype(BF16)
        c_prev = c_ref[bi, h]
        n_prev = n_ref[bi, h:h + 1, :]
        m_prev = m_ref[bi, :, h:h + 1]
        ig_c = gates[:, h:h + 1]
        b_c = bcum[:, N_HEADS + h:N_HEADS + h + 1]
        dlog = jnp.where(causal, b_c - row_of(b_c) + row_of(ig_c), -jnp.inf)
        s_log = b_c + m_prev
        m_row = jnp.maximum(s_log, jnp.max(dlog, axis=-1, keepdims=True))
        w = jnp.exp(dlog - m_row) * _dot_nt(qb, kb)
        sc = jnp.exp(s_log - m_row)
        num = sc * jnp.dot(qb, c_prev.astype(BF16), preferred_element_type=F32) + jnp.dot(
            w.astype(BF16), vb, preferred_element_type=F32)
        den = sc * jnp.sum(q * n_prev, axis=-1, keepdims=True) + jnp.sum(w, axis=-1, keepdims=True)
        hm = num / jnp.maximum(jnp.abs(den), jnp.exp(-m_row))
        m_new = m_row[lc - 1:lc, :]
        b_last = b_c[lc - 1:lc, :]
        w_state = jnp.exp(b_last + m_prev - m_new)
        kw = k * jnp.exp(b_last - b_c + ig_c - m_new)
        updates.append((c_ref, h, w_state * c_prev, kw, v))
        n_ref[bi, h:h + 1, :] = w_state * n_prev + jnp.sum(kw, axis=0, keepdims=True)
        m_ref[bi, :, h:h + 1] = m_new
        hm = _head_layer_norm(hm, nm_ref[:, lo:hi]) * jax.nn.sigmoid(og)
        h_ref[rows,lo:hi] = hm
        qr = rope(z_ref[rows,4 * W_HEADS + lo:4 * W_HEADS + hi])
        kr = rope(z_ref[rows,5 * W_HEADS + lo:5 * W_HEADS + hi]) * scale
        vr = z_ref[rows,6 * W_HEADS + lo:6 * W_HEADS + hi]
        gr = z_ref[rows,7 * W_HEADS + lo:7 * W_HEADS + hi]
        qrb, krb, vrb = qr.astype(BF16), kr.astype(BF16), vr.astype(BF16)
        s_prev = s_ref[bi, h]
        att = _dot_nt(qrb, krb) * dec_ref[h]
        inner = jnp.dot(att.astype(BF16), vrb, preferred_element_type=F32)
        cross = jnp.dot(qrb, s_prev.astype(BF16), preferred_element_type=F32) * rtab_ref[:, h:h + 1]
        krw = kr * rtab_ref[:, N_HEADS + h:N_HEADS + h + 1]
        updates.append((s_ref, h, sdec_ref[0:1, h:h + 1] * s_prev, krw, vr))
        yr = _head_layer_norm(inner + cross, nr_ref[:, lo:hi]) * (gr * jax.nn.sigmoid(gr))
        h_ref[rows,W_HEADS + lo:W_HEADS + hi] = yr

    if not short:
        for ref, h, decayed, kwt, val in updates:
            ref[bi, h] = decayed + _dot_tn(kwt.astype(BF16), val.astype(BF16))
    else:
        pad_rows = LANES - len(updates) * lc
        keys = jnp.concatenate([u[3] for u in updates] + [jnp.zeros((pad_rows, D_HEAD), F32)], axis=0)
        keys_t = keys.T.astype(BF16)
        for idx, (ref, h, decayed, _, val) in enumerate(updates):
            band = jnp.concatenate(
                [jnp.zeros((idx * lc, D_HEAD), F32)] * (idx > 0) + [val]
                + [jnp.zeros((LANES - (idx + 1) * lc, D_HEAD), F32)], axis=0)
            ref[bi, h] = decayed + jnp.dot(keys_t, band.astype(BF16), preferred_element_type=F32)


def _retention_tables(lc):
    lg = jnp.log1p(-(2.0 ** (-5.0 - jnp.arange(N_HEADS, dtype=F32))))
    idx = jnp.arange(lc, dtype=F32)
    diff = idx[:, None] - idx[None, :]
    decay = jnp.where(diff >= 0, jnp.exp(jnp.maximum(diff, 0.0)[None] * lg[:, None, None]), 0.0)
    cross = jnp.exp((idx[:, None] + 1.0) * lg[None, :])
    w_k = jnp.exp((lc - 1.0 - idx)[:, None] * lg[None, :])
    rtab = jnp.zeros((lc, GATE_COLS), F32).at[:, :N_HEADS].set(cross).at[:, N_HEADS:2 * N_HEADS].set(w_k)
    sdec = jnp.zeros((SUBLANES, GATE_COLS), F32).at[0, :N_HEADS].set(jnp.exp(lc * lg))
    return decay, rtab, sdec


def _rope_tables(pos):
    half = D_HEAD // 2
    freqs = ROPE_BASE ** (-jnp.arange(half, dtype=F32) / half)
    ang = pos[:, None] * freqs[None, :]
    return jnp.cos(ang), jnp.sin(ang)


def _mix_ab(z, zg, pos, conv0, c0, n0, m0, s0, conv_w, conv_b, gate_bias, norm_m, norm_r, *, bsz, length, row0, nb=1):
    lc = min(CHUNK, length)
    nc = length // lc
    assert nb == 1 or nc == 1
    rows = nb * lc
    blk0 = row0 // rows
    zw = z.shape[1]
    cos, sin = _rope_tables(pos)
    decay, rtab, sdec = _retention_tables(lc)
    full = lambda *shape: pl.BlockSpec(shape, lambda b, c: (0,) * len(shape))
    state4 = pl.BlockSpec((nb, N_HEADS, D_HEAD, D_HEAD), lambda b, c: (b, 0, 0, 0))
    state_n = pl.BlockSpec((nb, N_HEADS, D_HEAD), lambda b, c: (b, 0, 0))
    state_m = pl.BlockSpec((nb, 1, N_HEADS), lambda b, c: (b, 0, 0))
    state_conv = pl.BlockSpec((nb, CONV_W - 1, 2 * W_HEADS), lambda b, c: (b, 0, 0))
    outs = pl.pallas_call(
        functools.partial(_ab_kernel, lc=lc, nb=nb),
        grid=(bsz // nb, nc),
        in_specs=[
            pl.BlockSpec((rows, zw), lambda b, c: (blk0 + b * nc + c, 0)),
            pl.BlockSpec((rows, GATE_COLS), lambda b, c: (blk0 + b * nc + c, 0)),
            full(CONV_W, 2 * W_HEADS), full(1, 2 * W_HEADS), full(1, GATE_COLS),
            full(1, W_HEADS), full(1, W_HEADS),
            pl.BlockSpec((lc, D_HEAD // 2), lambda b, c: (c, 0)),
            pl.BlockSpec((lc, D_HEAD // 2), lambda b, c: (c, 0)),
            full(N_HEADS, lc, lc), full(lc, GATE_COLS), full(SUBLANES, GATE_COLS),
            state4, state_n, state_m, state_conv, state4,
        ],
        out_specs=[
            pl.BlockSpec((rows, 2 * W_HEADS), lambda b, c: (b * nc + c, 0)),
            state4, state_n, state_m, state_conv, state4,
        ],
        out_shape=[
            jax.ShapeDtypeStruct((bsz * length, 2 * W_HEADS), F32),
            jax.ShapeDtypeStruct((bsz, N_HEADS, D_HEAD, D_HEAD), F32),
            jax.ShapeDtypeStruct((bsz, N_HEADS, D_HEAD), F32),
            jax.ShapeDtypeStruct((bsz, 1, N_HEADS), F32),
            jax.ShapeDtypeStruct((bsz, CONV_W - 1, 2 * W_HEADS), F32),
            jax.ShapeDtypeStruct((bsz, N_HEADS, D_HEAD, D_HEAD), F32),
        ],
        scratch_shapes=[pltpu.VMEM((nb, lc + CONV_PAD, 2 * W_HEADS), F32)],
        compiler_params=_params("parallel", "arbitrary"),
        name="mix_ab",
    )(z, zg, conv_w, conv_b.reshape(1, -1), gate_bias, norm_m.reshape(1, -1), norm_r.reshape(1, -1),
      cos, sin, decay, rtab, sdec, c0, n0, m0.reshape(bsz, 1, N_HEADS), conv0, s0)
    h, c_new, n_new, m_new, conv_new, s_new = outs
    return h, conv_new, c_new, n_new, m_new.reshape(bsz, N_HEADS), s_new


def _s5_prep_kernel(lr_ref, li_ref, ldt_ref, bre_ref, bim_ref, expand_ref, are_ref, aim_ref, bbre_ref, bbim_ref):
    lr = lr_ref[...]
    li = li_ref[...]
    dt = jnp.exp(ldt_ref[...])
    mag = jnp.exp(lr * dt)
    a_re = mag * jnp.cos(li * dt)
    a_im = mag * jnp.sin(li * dt)
    denom = lr * lr + li * li
    g_re = ((a_re - 1.0) * lr + a_im * li) / denom
    g_im = (a_im * lr - (a_re - 1.0) * li) / denom
    are_ref[...] = a_re
    aim_ref[...] = a_im
    ge_re = jnp.dot(g_re, expand_ref[...], preferred_element_type=F32, precision=lax.Precision.HIGHEST)
    ge_im = jnp.dot(g_im, expand_ref[...], preferred_element_type=F32, precision=lax.Precision.HIGHEST)
    bre = bre_ref[...]
    bim = bim_ref[...]
    bbre_ref[...] = ge_re * bre - ge_im * bim
    bbim_ref[...] = ge_re * bim + ge_im * bre


def _s5_prep(lam_re, lam_im, log_dt, bt_re, bt_im):
    g, p = lam_re.shape
    expand = jnp.tile(jnp.eye(p, dtype=F32), (1, S5_GROUP))
    return pl.pallas_call(
        _s5_prep_kernel,
        out_shape=[
            jax.ShapeDtypeStruct((g, p), F32), jax.ShapeDtypeStruct((g, p), F32),
            jax.ShapeDtypeStruct((g, p * S5_GROUP), F32), jax.ShapeDtypeStruct((g, p * S5_GROUP), F32),
        ],
        compiler_params=pltpu.CompilerParams(vmem_limit_bytes=VMEM_LIMIT_BYTES),
        name="s5_prep",
    )(lam_re, lam_im, log_dt.reshape(g, 1), bt_re.reshape(g, p * S5_GROUP), bt_im.reshape(g, p * S5_GROUP), expand)


S5_COLS = GROUPS_PER_BLOCK * S5_GROUP
S5_HALF = GROUPS_PER_BLOCK * S5_STATES
S5_TILES = S5_HALF // LANES
S5_COL_TILES = S5_COLS // LANES
S5_TILE_COLS = 2 * LANES


def _tile_major(re, im):
    lead = re.shape[:-1]
    pair = jnp.stack([re.reshape(lead + (S5_TILES, LANES)), im.reshape(lead + (S5_TILES, LANES))], axis=-2)
    return pair.reshape(lead + (2 * S5_HALF,))


def _s5_scan_dense(bu_ref, xn_ref, a_ref, nb, tt):
    a_re = [jnp.broadcast_to(a_ref[0, :, (2 * j) * LANES:(2 * j + 1) * LANES], (SUBLANES, LANES))
            for j in range(S5_TILES)]
    a_im = [jnp.broadcast_to(a_ref[0, :, (2 * j + 1) * LANES:(2 * j + 2) * LANES], (SUBLANES, LANES))
            for j in range(S5_TILES)]

    def group(s, _):
        r0 = pl.multiple_of(s * SUBLANES, SUBLANES)
        for j in range(S5_TILES):
            re_cols = slice((2 * j) * LANES, (2 * j + 1) * LANES)
            im_cols = slice((2 * j + 1) * LANES, (2 * j + 2) * LANES)
            xr = xn_ref[pl.ds(r0, SUBLANES), re_cols]
            xi = xn_ref[pl.ds(r0, SUBLANES), im_cols]
            for t in range(tt):
                rows = pl.ds(t * nb + r0, SUBLANES)
                xr, xi = (a_re[j] * xr - a_im[j] * xi + bu_ref[2 * j, rows, :],
                          a_re[j] * xi + a_im[j] * xr + bu_ref[2 * j + 1, rows, :])
                bu_ref[2 * j, rows, :] = xr
                bu_ref[2 * j + 1, rows, :] = xi
            xn_ref[pl.ds(r0, SUBLANES), re_cols] = xr
            xn_ref[pl.ds(r0, SUBLANES), im_cols] = xi
        return 0

    lax.fori_loop(0, nb // SUBLANES, group, 0)


def _s5_scan_pairs(bu, state_re, state_im, a_re, a_im, nb):
    upper = lax.broadcasted_iota(jnp.int32, (SUBLANES, LANES), 0) >= nb
    ar = jnp.broadcast_to(a_re, (SUBLANES, LANES))
    ai = jnp.broadcast_to(a_im, (SUBLANES, LANES))
    c1_re = jnp.where(upper, ar * ar - ai * ai, ar)
    c1_im = jnp.where(upper, 2.0 * ar * ai, ai)
    c2_re = jnp.where(upper, ar, 0.0)
    c2_im = jnp.where(upper, ai, 0.0)

    def both_halves(x):
        return jnp.where(upper, x, pltpu.roll(x, nb, axis=0))

    pr = jnp.where(upper, pltpu.roll(state_re, nb, axis=0), state_re)
    pi = jnp.where(upper, pltpu.roll(state_im, nb, axis=0), state_im)
    out = []
    for i in range(bu.shape[0] // SUBLANES):
        vr = bu[i * SUBLANES:(i + 1) * SUBLANES, :LANES]
        vi = bu[i * SUBLANES:(i + 1) * SUBLANES, LANES:]
        rr = pltpu.roll(vr, nb, axis=0)
        ri = pltpu.roll(vi, nb, axis=0)
        yr = c1_re * pr - c1_im * pi + (c2_re * rr - c2_im * ri) + vr
        yi = c1_re * pi + c1_im * pr + (c2_re * ri + c2_im * rr) + vi
        out.append(jnp.concatenate([yr, yi], axis=-1))
        pr, pi = both_halves(yr), both_halves(yi)
    return jnp.concatenate(out, axis=0), pr, pi


def _s5_kernel(*refs, nb, tt, interleave):
    n_u = nb * S5_COL_TILES if interleave else 1
    u_refs = refs[:n_u]
    bb_ref, cc_ref, a_ref, d_ref, x0_ref, y_ref, xn_ref = refs[n_u:n_u + 7]

    @pl.when(pl.program_id(1) == 0)
    def _():
        xn_ref[...] = x0_ref[...]

    if interleave:
        assert 2 * nb == SUBLANES
        ut_ref, yt_ref = refs[n_u + 7:]
        for b in range(nb):
            for c in range(S5_COL_TILES):
                ut_ref[c, pl.ds(b, tt, stride=nb), :] = u_refs[b * S5_COL_TILES + c][...]
        u = jnp.concatenate([ut_ref[c] for c in range(S5_COL_TILES)], axis=-1)
        ub = u.astype(BF16)
        y = d_ref[...] * u
        for j in range(S5_TILES):
            cols = slice(j * S5_TILE_COLS, (j + 1) * S5_TILE_COLS)
            re_cols = slice(j * S5_TILE_COLS, j * S5_TILE_COLS + LANES)
            im_cols = slice(j * S5_TILE_COLS + LANES, (j + 1) * S5_TILE_COLS)
            bu = jnp.dot(ub, bb_ref[0, :, cols], preferred_element_type=F32)
            x, fin_re, fin_im = _s5_scan_pairs(bu, xn_ref[:, re_cols], xn_ref[:, im_cols],
                                               a_ref[0, :, re_cols], a_ref[0, :, im_cols], nb)
            xn_ref[:, re_cols] = fin_re
            xn_ref[:, im_cols] = fin_im
            y = y + jnp.dot(x.astype(BF16), cc_ref[0, cols, :], preferred_element_type=F32)
    else:
        assert nb % SUBLANES == 0
        (bu_ref,) = refs[n_u + 7:]
        u = u_refs[0][...]
        bu = jnp.dot(u.astype(BF16), bb_ref[0], preferred_element_type=F32)
        for j in range(2 * S5_TILES):
            bu_ref[j] = bu[:, j * LANES:(j + 1) * LANES]
        _s5_scan_dense(bu_ref, xn_ref, a_ref, nb, tt)
        x = jnp.concatenate([bu_ref[j] for j in range(2 * S5_TILES)], axis=-1)
        y = jnp.dot(x.astype(BF16), cc_ref[0], preferred_element_type=F32) + d_ref[...] * u
    y = jax.nn.gelu(y)
    if interleave:
        for c in range(S5_COL_TILES):
            yt_ref[c] = y[:, c * LANES:(c + 1) * LANES]
        for b in range(nb):
            for c in range(S5_COL_TILES):
                y_ref[b, :, c * LANES:(c + 1) * LANES] = yt_ref[c, pl.ds(b, tt, stride=nb), :].astype(BF16)
    else:
        y_ref[...] = y.astype(BF16)


def _mix_s5(u, x0_re, x0_im, bb_blk, cc_blk, a_blk, d_skip, *, bsz, length, tt, row0, interleave):
    w = u.shape[-1]
    nblk = w // S5_COLS
    nt = length // tt
    x0 = _tile_major(x0_re.reshape(bsz, nblk, S5_HALF), x0_im.reshape(bsz, nblk, S5_HALF))
    srows = max(bsz, SUBLANES)
    x0 = jnp.pad(x0.transpose(1, 0, 2), ((0, 0), (0, srows - bsz), (0, 0)))
    rows = bsz * tt
    if interleave:
        blk0 = row0 // tt
        u_specs = [pl.BlockSpec((tt, LANES), functools.partial(
            lambda g, t, b, c: (blk0 + b * nt + t, g * S5_COL_TILES + c), b=b, c=c))
            for b in range(bsz) for c in range(S5_COL_TILES)]
        u_args = [u] * (bsz * S5_COL_TILES)
        y_spec = pl.BlockSpec((bsz, tt, S5_COLS), lambda g, t: (0, t, g))
        y_shape = jax.ShapeDtypeStruct((bsz, length, w), BF16)
        scratch = [pltpu.VMEM((S5_COL_TILES, rows, LANES), F32), pltpu.VMEM((S5_COL_TILES, rows, LANES), F32)]
    else:
        assert nt == 1
        blk0 = row0 // rows
        u_specs = [pl.BlockSpec((rows, S5_COLS), lambda g, t: (blk0, g))]
        u_args = [u]
        y_spec = pl.BlockSpec((rows, S5_COLS), lambda g, t: (0, g))
        y_shape = jax.ShapeDtypeStruct((rows, w), BF16)
        scratch = [pltpu.VMEM((2 * S5_TILES, rows, LANES), F32)]
    y, xn = pl.pallas_call(
        functools.partial(_s5_kernel, nb=bsz, tt=tt, interleave=interleave),
        grid=(nblk, nt),
        in_specs=u_specs + [
            pl.BlockSpec((1, S5_COLS, 2 * S5_HALF), lambda g, t: (g, 0, 0)),
            pl.BlockSpec((1, 2 * S5_HALF, S5_COLS), lambda g, t: (g, 0, 0)),
            pl.BlockSpec((1, 1, 2 * S5_HALF), lambda g, t: (g, 0, 0)),
            pl.BlockSpec((1, S5_COLS), lambda g, t: (0, g)),
            pl.BlockSpec((None, srows, 2 * S5_HALF), lambda g, t: (g, 0, 0)),
        ],
        out_specs=[y_spec, pl.BlockSpec((None, srows, 2 * S5_HALF), lambda g, t: (g, 0, 0))],
        out_shape=[y_shape, jax.ShapeDtypeStruct((nblk, srows, 2 * S5_HALF), F32)],
        scratch_shapes=scratch,
        compiler_params=_params("parallel", "arbitrary"),
        name="mix_s5",
    )(*u_args, bb_blk, cc_blk, a_blk, d_skip.reshape(1, w), x0)
    xn = xn[:, :bsz].transpose(1, 0, 2).reshape(bsz, nblk, S5_TILES, 2, LANES)
    g_total = nblk * GROUPS_PER_BLOCK
    xr = xn[..., 0, :].reshape(bsz, g_total, S5_STATES)
    xi = xn[..., 1, :].reshape(bsz, g_total, S5_STATES)
    return y, xr, xi


def _block_bands(blk_re, blk_im):
    g = blk_re.shape[0]
    nblk = g // GROUPS_PER_BLOCK
    per_tile = LANES // S5_STATES
    shape = (nblk, S5_TILES, per_tile, S5_GROUP, S5_STATES)
    pair = jnp.stack([blk_re.reshape(shape), blk_im.reshape(shape)], axis=2)
    band = jnp.einsum('ntigcp,gh->ntigchp', pair, jnp.eye(per_tile, dtype=F32))
    band = band.reshape(nblk, S5_TILES, 2, per_tile * S5_GROUP, LANES)
    full = jnp.einsum('ntirl,st->nsrtil', band, jnp.eye(S5_TILES, dtype=F32))
    return full.reshape(nblk, S5_COLS, 2 * S5_HALF).astype(BF16)


def _s5_block_weights(a_re, a_im, bb_re, bb_im, c_re, c_im):
    g, p = a_re.shape
    nblk = g // GROUPS_PER_BLOCK
    bb = _block_bands(bb_re.reshape(g, S5_GROUP, p), bb_im.reshape(g, S5_GROUP, p))
    cc = jnp.swapaxes(_block_bands(c_re, -c_im), 1, 2)
    a_blk = _tile_major(a_re.reshape(nblk, S5_HALF), a_im.reshape(nblk, S5_HALF)).reshape(nblk, 1, 2 * S5_HALF)
    return bb, cc, a_blk


def kernel(x_prompt, x_sample, state_mlstm_C, state_mlstm_n, state_mlstm_m, state_mlstm_conv, state_ret_S, state_s5_re, state_s5_im, norm_ffn1, norm_mix, norm_ffn2, norm_final, ffn1_w_gate, ffn1_w_up, ffn1_w_down, ffn2_w_gate, ffn2_w_up, ffn2_w_down, ab_w_in, mlstm_b_i, mlstm_b_f, mlstm_conv_w, mlstm_conv_b, mlstm_norm, ret_norm, ab_w_out, s5_w_in, s5_lambda_re, s5_lambda_im, s5_log_dt, s5_B_re, s5_B_im, s5_C_re, s5_C_im, s5_D, s5_w_glu, s5_b_glu):
    bp, lp, d = x_prompt.shape
    bs, ls, _ = x_sample.shape
    mp, ms = bp * lp, bs * ls
    depth = norm_ffn1.shape[0]
    assert mp % TOKEN_TILE == 0 and ms % TOKEN_TILE == 0

    ffn = functools.partial(_ffn, tm=TOKEN_TILE, tf=FFN_TILE)
    pos_p = jnp.arange(lp, dtype=F32)
    pos_s = PAST_LEN + jnp.arange(ls, dtype=F32)

    ys = (x_prompt.reshape(mp, d), x_sample.reshape(ms, d))
    out_mc, out_mn, out_mm, out_conv, out_rs, out_re, out_im = ([] for _ in range(7))
    for layer in range(depth):
        y = ffn(ys, norm_ffn1, ffn1_w_gate, ffn1_w_up, ffn1_w_down, layer)
        if layer % 2 == 0:
            e = layer // 2
            n_gate = 2 * N_HEADS
            wt = jnp.swapaxes(ab_w_in, 1, 2)[e]
            wt_hi = wt[4 * W_HEADS + n_gate:]
            z, zg = _ab_proj(y, norm_mix, wt, wt_hi, layer, tm=AB_PROJ_ROWS, tn=PROJ_TILE)
            gate_bias = jnp.zeros((1, GATE_COLS), F32).at[0, :N_HEADS].set(mlstm_b_i[e]).at[
                0, N_HEADS:n_gate].set(mlstm_b_f[e])
            common = (mlstm_conv_w[e], mlstm_conv_b[e], gate_bias, mlstm_norm[e], ret_norm[e])
            hp, conv_p, c_p, n_p, m_p, s_p = _mix_ab(
                z, zg, pos_p, jnp.zeros((bp, CONV_W - 1, 2 * W_HEADS), F32),
                jnp.zeros((bp, N_HEADS, D_HEAD, D_HEAD), F32), jnp.zeros((bp, N_HEADS, D_HEAD), F32),
                jnp.full((bp, N_HEADS), M_INIT, F32), jnp.zeros((bp, N_HEADS, D_HEAD, D_HEAD), F32),
                *common, bsz=bp, length=lp, row0=0)
            hs, conv_s, c_s, n_s, m_s, s_s = _mix_ab(
                z, zg, pos_s, state_mlstm_conv[e], state_mlstm_C[e], state_mlstm_n[e], state_mlstm_m[e],
                state_ret_S[e], *common, bsz=bs, length=ls, row0=mp, nb=SAMPLE_BATCH_TILE)
            out_mc.append((c_p, c_s))
            out_mn.append((n_p, n_s))
            out_mm.append((m_p, m_s))
            out_conv.append((conv_p, conv_s))
            out_rs.append((s_p, s_s))
            y = _proj_res((hp, hs), ab_w_out[e], y, tm=TOKEN_TILE // 4)
        else:
            o = layer // 2
            u = _norm_proj(y, norm_mix, s5_w_in[o], layer, tm=TOKEN_TILE // 2)
            a_re, a_im, bb_re, bb_im = _s5_prep(s5_lambda_re[o], s5_lambda_im[o], s5_log_dt[o],
                                                jnp.swapaxes(s5_B_re[o], 1, 2), jnp.swapaxes(s5_B_im[o], 1, 2))
            bb, cc, a_blk = _s5_block_weights(a_re, a_im, bb_re, bb_im, s5_C_re[o], s5_C_im[o])
            zeros_p = jnp.zeros((bp,) + a_re.shape, F32)
            ya_p, re_p, im_p = _mix_s5(u, zeros_p, zeros_p, bb, cc, a_blk, s5_D[o], bsz=bp, length=lp,
                                       tt=S5_TIME_TILE, row0=0, interleave=True)
            u_s = u[mp:].reshape(bs, ls, -1).transpose(1, 0, 2).reshape(ms, -1)
            ya_s, re_s, im_s = _mix_s5(u_s, state_s5_re[o], state_s5_im[o], bb, cc, a_blk, s5_D[o], bsz=bs,
                                       length=ls, tt=ls, row0=0, interleave=False)
            ya_s = ya_s.reshape(ls, bs, -1).transpose(1, 0, 2).reshape(ms, -1)
            out_re.append((re_p, re_s))
            out_im.append((im_p, im_s))
            y = _glu((ya_p.reshape(mp, -1), ya_s), s5_w_glu, s5_b_glu, y, o, tm=TOKEN_TILE, tn=PROJ_TILE)
        ys = (ffn((y,), norm_ffn2, ffn2_w_gate, ffn2_w_up, ffn2_w_down, layer),)
    y = ys[0]
    y_p = _final_norm(y, norm_final, row0=0, m=mp, tm=TOKEN_TILE)
    y_s = _final_norm(y, norm_final, row0=mp, m=ms, tm=TOKEN_TILE)

    def both(pairs):
        return jnp.stack([p for p, _ in pairs]), jnp.stack([s for _, s in pairs])

    pc, sc = both(out_mc)
    pn, sn = both(out_mn)
    pm, sm = both(out_mm)
    pconv, sconv = both(out_conv)
    ps, ss = both(out_rs)
    pre, sre = both(out_re)
    pim, sim = both(out_im)
    return (y_p.reshape(bp, lp, d), y_s.reshape(bs, ls, d), pc, sc, pn, sn, pm, sm, pconv, sconv, ps, ss,
            pre, sre, pim, sim)
```

```python
import functools

import jax
import jax.numpy as jnp
from jax import lax
from jax.experimental import pallas as pl
from jax.experimental.pallas import tpu as pltpu

F32 = jnp.float32
BF16 = jnp.bfloat16

EPS = 1e-6
M_INIT = -1e30
CHUNK = 128
CONV_W = 4
ROPE_BASE = 10000.0
PAST_LEN = 16384
N_HEADS = 4
D_HEAD = 256
W_HEADS = N_HEADS * D_HEAD
S5_GROUP = 16
S5_STATES = 64
GROUPS_PER_BLOCK = 16
LANES = 128
SUBLANES = 8
GATE_COLS = LANES

VMEM_LIMIT_BYTES = 60 * 1024 * 1024

TOKEN_TILE = 1024
FFN_TILE = 256
PROJ_TILE = 512
AB_PROJ_ROWS = 1536
S5_TIME_TILE = 128
SAMPLE_BATCH_TILE = 4


def _params(*semantics):
    return pltpu.CompilerParams(dimension_semantics=semantics, vmem_limit_bytes=VMEM_LIMIT_BYTES)


def _single_buffered(shape, index_map):
    return pl.BlockSpec(shape, index_map, pipeline_mode=pl.Buffered(1))


def _rms_norm(x, g):
    return x * lax.rsqrt(jnp.mean(x * x, axis=-1, keepdims=True) + EPS) * g


def _two_group_specs(tm, width, n_a, single=False):
    mk = _single_buffered if single else pl.BlockSpec
    return [mk((tm, width), lambda i, *_: (jnp.minimum(i, n_a - 1), 0)),
            mk((tm, width), lambda i, *_: (jnp.maximum(i - n_a, 0), 0))]


def _for_row_group(i, n_a, a_ref, b_ref, fn):
    @pl.when(i < n_a)
    def _():
        fn(a_ref[...])

    @pl.when(i >= n_a)
    def _():
        fn(b_ref[...])


def _ffn_kernel(*refs, n_a):
    if n_a is None:
        x_ref, g_ref, wg_ref, wu_ref, wd_ref, o_ref, xn_ref = refs
    else:
        xa_ref, xb_ref, g_ref, wg_ref, wu_ref, wd_ref, o_ref, xn_ref = refs
    f = pl.program_id(1)

    def init(x):
        xn_ref[...] = _rms_norm(x, g_ref[...]).astype(BF16)
        o_ref[...] = 2.0 * x

    @pl.when(f == 0)
    def _():
        if n_a is None:
            init(x_ref[...])
        else:
            _for_row_group(pl.program_id(0), n_a, xa_ref, xb_ref, init)

    xn = xn_ref[...]
    a = jnp.dot(xn, wg_ref[...].astype(BF16), preferred_element_type=F32)
    b = jnp.dot(xn, wu_ref[...].astype(BF16), preferred_element_type=F32)
    h = (a * jax.nn.sigmoid(a) * b).astype(BF16)
    o_ref[...] += jnp.dot(h, wd_ref[...].astype(BF16), preferred_element_type=F32)

    @pl.when(f == pl.num_programs(1) - 1)
    def _():
        o_ref[...] = 0.5 * o_ref[...]


def _ffn(xs, g, w_gate, w_up, w_down, layer, *, tm, tf):
    d = xs[0].shape[1]
    m = sum(x.shape[0] for x in xs)
    dff = w_gate.shape[2]
    if len(xs) == 1:
        n_a = None
        x_specs = [pl.BlockSpec((tm, d), lambda i, f: (i, 0))]
    else:
        n_a = xs[0].shape[0] // tm
        x_specs = _two_group_specs(tm, d, n_a, single=True)
    return pl.pallas_call(
        functools.partial(_ffn_kernel, n_a=n_a),
        grid=(m // tm, dff // tf),
        in_specs=x_specs + [
            pl.BlockSpec((None, 1, d), lambda i, f: (layer, 0, 0)),
            pl.BlockSpec((None, d, tf), lambda i, f: (layer, 0, f)),
            pl.BlockSpec((None, d, tf), lambda i, f: (layer, 0, f)),
            pl.BlockSpec((None, tf, d), lambda i, f: (layer, f, 0)),
        ],
        out_specs=pl.BlockSpec((tm, d), lambda i, f: (i, 0)),
        out_shape=jax.ShapeDtypeStruct((m, d), F32),
        scratch_shapes=[pltpu.VMEM((tm, d), BF16)],
        compiler_params=_params("parallel", "arbitrary"),
        name="ffn",
    )(*xs, g.reshape(g.shape[0], 1, d), w_gate, w_up, w_down)


def _ab_proj_kernel(x_ref, g_ref, wlo_ref, whi_ref, wgate_ref, z_ref, zg_ref, xn_ref, *, n_lo):
    j = pl.program_id(1)

    @pl.when(j == 0)
    def _():
        xn_ref[...] = _rms_norm(x_ref[...], g_ref[...]).astype(BF16)
        zg_ref[...] = _dot_nt(xn_ref[...], wgate_ref[...].astype(BF16))

    @pl.when(j < n_lo)
    def _():
        z_ref[...] = _dot_nt(xn_ref[...], wlo_ref[...].astype(BF16))

    @pl.when(j >= n_lo)
    def _():
        z_ref[...] = _dot_nt(xn_ref[...], whi_ref[...].astype(BF16))


def _ab_proj(x, g, wt, layer, *, tm, tn):
    m, d = x.shape
    n_lo = n_hi = 4 * W_HEADS // tn
    hi0 = 4 * W_HEADS + 2 * N_HEADS
    gate_blk = 4 * W_HEADS // GATE_COLS
    return pl.pallas_call(
        functools.partial(_ab_proj_kernel, n_lo=n_lo),
        grid=(m // tm, n_lo + n_hi),
        in_specs=[
            pl.BlockSpec((tm, d), lambda i, j: (i, 0)),
            pl.BlockSpec((None, 1, d), lambda i, j: (layer, 0, 0)),
            pl.BlockSpec((tn, d), lambda i, j: (jnp.minimum(j, n_lo - 1), 0)),
            pl.BlockSpec((pl.Element(tn), pl.Element(d)),
                         lambda i, j: (pl.multiple_of(hi0 + jnp.maximum(j - n_lo, 0) * tn, SUBLANES), 0)),
            pl.BlockSpec((GATE_COLS, d), lambda i, j: (gate_blk, 0)),
        ],
        out_specs=[
            pl.BlockSpec((tm, tn), lambda i, j: (i, j)),
            pl.BlockSpec((tm, GATE_COLS), lambda i, j: (i, 0)),
        ],
        out_shape=[
            jax.ShapeDtypeStruct((m, (n_lo + n_hi) * tn), F32),
            jax.ShapeDtypeStruct((m, GATE_COLS), F32),
        ],
        scratch_shapes=[pltpu.VMEM((tm, d), BF16)],
        compiler_params=_params("parallel", "arbitrary"),
        name="ab_proj",
    )(x, g.reshape(g.shape[0], 1, d), wt, wt, wt)


def _norm_proj_kernel(x_ref, g_ref, w_ref, o_ref, wb_ref):
    @pl.when(pl.program_id(0) == 0)
    def _():
        wb_ref[...] = w_ref[...].astype(BF16)

    xn = _rms_norm(x_ref[...], g_ref[...]).astype(BF16)
    o_ref[...] = jnp.dot(xn, wb_ref[...], preferred_element_type=F32)


def _norm_proj(x, g, w, layer, *, tm):
    m, d = x.shape
    n = w.shape[1]
    return pl.pallas_call(
        _norm_proj_kernel,
        grid=(m // tm,),
        in_specs=[
            pl.BlockSpec((tm, d), lambda i: (i, 0)),
            pl.BlockSpec((None, 1, d), lambda i: (layer, 0, 0)),
            _single_buffered((d, n), lambda i: (0, 0)),
        ],
        out_specs=pl.BlockSpec((tm, n), lambda i: (i, 0)),
        out_shape=jax.ShapeDtypeStruct((m, n), F32),
        scratch_shapes=[pltpu.VMEM((d, n), BF16)],
        compiler_params=_params("arbitrary"),
        name="norm_proj",
    )(x, g.reshape(g.shape[0], 1, d), w)


def _proj_res_kernel(aa_ref, ab_ref, w_ref, r_ref, o_ref, wb_ref, *, n_a):
    i = pl.program_id(0)

    @pl.when(i == 0)
    def _():
        wb_ref[...] = w_ref[...].astype(BF16)

    def run(a):
        o_ref[...] = r_ref[...] + jnp.dot(a.astype(BF16), wb_ref[...], preferred_element_type=F32)

    _for_row_group(i, n_a, aa_ref, ab_ref, run)


def _proj_res(a_pair, w, res, *, tm):
    k = a_pair[0].shape[1]
    m = res.shape[0]
    n = w.shape[1]
    n_a = a_pair[0].shape[0] // tm
    return pl.pallas_call(
        functools.partial(_proj_res_kernel, n_a=n_a),
        grid=(m // tm,),
        in_specs=_two_group_specs(tm, k, n_a) + [
            _single_buffered((k, n), lambda i: (0, 0)),
            pl.BlockSpec((tm, n), lambda i: (i, 0)),
        ],
        out_specs=pl.BlockSpec((tm, n), lambda i: (i, 0)),
        out_shape=jax.ShapeDtypeStruct((m, n), F32),
        scratch_shapes=[pltpu.VMEM((k, n), BF16)],
        compiler_params=_params("arbitrary"),
        name="proj_res",
    )(*a_pair, w, res)


def _glu_kernel(aa_ref, ab_ref, wv_ref, wg_ref, bv_ref, bg_ref, r_ref, o_ref, *, n_a):
    def run(a):
        v = jnp.dot(a, wv_ref[...].astype(BF16), preferred_element_type=F32) + bv_ref[...]
        t = jnp.dot(a, wg_ref[...].astype(BF16), preferred_element_type=F32) + bg_ref[...]
        o_ref[...] = r_ref[...] + v * jax.nn.sigmoid(t)

    _for_row_group(pl.program_id(0), n_a, aa_ref, ab_ref, run)


def _glu(a_pair, w, b, res, layer, *, tm, tn):
    k = a_pair[0].shape[1]
    m = res.shape[0]
    n = w.shape[2] // 2
    nj = n // tn
    n_a = a_pair[0].shape[0] // tm
    b3 = b.reshape(b.shape[0], 1, 2 * n)
    return pl.pallas_call(
        functools.partial(_glu_kernel, n_a=n_a),
        grid=(m // tm, nj),
        in_specs=_two_group_specs(tm, k, n_a) + [
            pl.BlockSpec((None, k, tn), lambda i, j: (layer, 0, j)),
            pl.BlockSpec((None, k, tn), lambda i, j: (layer, 0, j + nj)),
            pl.BlockSpec((None, 1, tn), lambda i, j: (layer, 0, j)),
            pl.BlockSpec((None, 1, tn), lambda i, j: (layer, 0, j + nj)),
            pl.BlockSpec((tm, tn), lambda i, j: (i, j)),
        ],
        out_specs=pl.BlockSpec((tm, tn), lambda i, j: (i, j)),
        out_shape=jax.ShapeDtypeStruct((m, n), F32),
        compiler_params=_params("parallel", "arbitrary"),
        name="glu",
    )(*a_pair, w, w, b3, b3, res)


def _final_norm_kernel(x_ref, g_ref, o_ref):
    o_ref[...] = _rms_norm(x_ref[...], g_ref[...])


def _final_norm(x, g, *, row0, m, tm):
    d = x.shape[1]
    blk0 = row0 // tm
    return pl.pallas_call(
        _final_norm_kernel,
        grid=(m // tm,),
        in_specs=[pl.BlockSpec((tm, d), lambda i: (blk0 + i, 0)), pl.BlockSpec((1, d), lambda i: (0, 0))],
        out_specs=pl.BlockSpec((tm, d), lambda i: (i, 0)),
        out_shape=jax.ShapeDtypeStruct((m, d), F32),
        compiler_params=_params("parallel"),
        name="final_norm",
    )(x, g.reshape(1, d))


def _head_layer_norm(x, g):
    xc = x - jnp.mean(x, axis=-1, keepdims=True)
    var = jnp.mean(xc * xc, axis=-1, keepdims=True)
    return xc * lax.rsqrt(var + EPS) * g


def _cumsum_rows(x, n):
    row = lax.broadcasted_iota(jnp.int32, x.shape, 0)
    k = 1
    while k < n:
        x = x + jnp.where(row >= k, pltpu.roll(x, k, axis=0), 0.0)
        k *= 2
    return x


def _dot_nt(a, b):
    return lax.dot_general(a, b, (((1,), (1,)), ((), ())), preferred_element_type=F32)


def _dot_tn(a, b):
    return lax.dot_general(a, b, (((0,), (0,)), ((), ())), preferred_element_type=F32)


CONV_PAD = SUBLANES


def _ab_kernel(z_ref, zg_ref, cw_ref, cb_ref, gb_ref, nm_ref, nr_ref, cos_ref, sin_ref, dec_ref, rtab_ref, sdec_ref,
               c0_ref, n0_ref, m0_ref, conv0_ref, s0_ref,
               h_ref, c_ref, n_ref, m_ref, conv_ref, s_ref,
               xp_ref, *, lc, nb):
    @pl.when(pl.program_id(1) == 0)
    def _():
        c_ref[...] = c0_ref[...]
        n_ref[...] = n0_ref[...]
        m_ref[...] = m0_ref[...]
        s_ref[...] = s0_ref[...]
        xp_ref[:, CONV_PAD - (CONV_W - 1):CONV_PAD, :] = conv0_ref[...]

    for bi in range(nb):
        _ab_sequence(bi, z_ref, zg_ref, cw_ref, cb_ref, gb_ref, nm_ref, nr_ref, cos_ref, sin_ref, dec_ref, rtab_ref,
                     sdec_ref, h_ref, c_ref, n_ref, m_ref, conv_ref, s_ref, xp_ref, lc=lc)


def _ab_sequence(bi, z_ref, zg_ref, cw_ref, cb_ref, gb_ref, nm_ref, nr_ref, cos_ref, sin_ref, dec_ref, rtab_ref,
                 sdec_ref, h_ref, c_ref, n_ref, m_ref, conv_ref, s_ref, xp_ref, *, lc):
    rows = slice(bi * lc, (bi + 1) * lc)
    pad = CONV_PAD

    xp_ref[bi, pad:pad + lc, :] = z_ref[rows, 0:2 * W_HEADS]
    qk = cb_ref[...]
    for j in range(CONV_W):
        qk = qk + xp_ref[bi, pad - (CONV_W - 1) + j:pad - (CONV_W - 1) + j + lc, :] * cw_ref[j:j + 1, :]
    new_buf = xp_ref[bi, pad + lc - (CONV_W - 1):pad + lc, :]
    xp_ref[bi, pad - (CONV_W - 1):pad, :] = new_buf
    conv_ref[bi] = new_buf
    qk = qk * jax.nn.sigmoid(qk)

    row = lax.broadcasted_iota(jnp.int32, (lc, lc), 0)
    col = lax.broadcasted_iota(jnp.int32, (lc, lc), 1)
    causal = row >= col
    eye = row == col
    short = 2 * N_HEADS * lc <= LANES

    def row_of(c):
        return jnp.sum(jnp.where(eye, c, 0.0), axis=0, keepdims=True)

    gates = zg_ref[rows, :] + gb_ref[...]
    lf = jax.nn.log_sigmoid(gates)
    if short:
        bcum = _cumsum_rows(lf, lc)
    else:
        bcum = jnp.dot(causal.astype(F32), lf, preferred_element_type=F32, precision=lax.Precision.HIGHEST)
    updates = []

    scale = D_HEAD ** -0.5
    cos = cos_ref[...]
    sin = sin_ref[...]

    def rope(x):
        x1 = x[:, :D_HEAD // 2]
        x2 = x[:, D_HEAD // 2:]
        return jnp.concatenate([x1 * cos - x2 * sin, x1 * sin + x2 * cos], axis=-1)

    for h in range(N_HEADS):
        lo, hi = h * D_HEAD, (h + 1) * D_HEAD
        q = qk[:, lo:hi]
        k = qk[:, W_HEADS + lo:W_HEADS + hi] * scale
        v = z_ref[rows,2 * W_HEADS + lo:2 * W_HEADS + hi]
        og = z_ref[rows,3 * W_HEADS + lo:3 * W_HEADS + hi]
        qb, kb, vb = q.astype(BF16), k.astype(BF16), v.astype(BF16)
        c_prev = c_ref[bi, h]
        n_prev = n_ref[bi, h:h + 1, :]
        m_prev = m_ref[bi, :, h:h + 1]
        ig_c = gates[:, h:h + 1]
        b_c = bcum[:, N_HEADS + h:N_HEADS + h + 1]
        dlog = jnp.where(causal, b_c - row_of(b_c) + row_of(ig_c), -jnp.inf)
        s_log = b_c + m_prev
        m_row = jnp.maximum(s_log, jnp.max(dlog, axis=-1, keepdims=True))
        w = jnp.exp(dlog - m_row) * _dot_nt(qb, kb)
        sc = jnp.exp(s_log - m_row)
        num = sc * jnp.dot(qb, c_prev.astype(BF16), preferred_element_type=F32) + jnp.dot(
            w.astype(BF16), vb, preferred_element_type=F32)
        den = sc * jnp.sum(q * n_prev, axis=-1, keepdims=True) + jnp.sum(w, axis=-1, keepdims=True)
        hm = num / jnp.maximum(jnp.abs(den), jnp.exp(-m_row))
        m_new = m_row[lc - 1:lc, :]
        b_last = b_c[lc - 1:lc, :]
        w_state = jnp.exp(b_last + m_prev - m_new)
        kw = k * jnp.exp(b_last - b_c + ig_c - m_new)
        updates.append((c_ref, h, w_state * c_prev, kw, v))
        n_ref[bi, h:h + 1, :] = w_state * n_prev + jnp.sum(kw, axis=0, keepdims=True)
        m_ref[bi, :, h:h + 1] = m_new
        hm = _head_layer_norm(hm, nm_ref[:, lo:hi]) * jax.nn.sigmoid(og)
        h_ref[rows,lo:hi] = hm
    for h in range(N_HEADS):
        lo, hi = h * D_HEAD, (h + 1) * D_HEAD
        qr = rope(z_ref[rows,4 * W_HEADS + lo:4 * W_HEADS + hi])
        kr = rope(z_ref[rows,5 * W_HEADS + lo:5 * W_HEADS + hi]) * scale
        vr = z_ref[rows,6 * W_HEADS + lo:6 * W_HEADS + hi]
        gr = z_ref[rows,7 * W_HEADS + lo:7 * W_HEADS + hi]
        qrb, krb, vrb = qr.astype(BF16), kr.astype(BF16), vr.astype(BF16)
        s_prev = s_ref[bi, h]
        att = _dot_nt(qrb, krb) * dec_ref[h]
        inner = jnp.dot(att.astype(BF16), vrb, preferred_element_type=F32)
        cross = jnp.dot(qrb, s_prev.astype(BF16), preferred_element_type=F32) * rtab_ref[:, h:h + 1]
        krw = kr * rtab_ref[:, N_HEADS + h:N_HEADS + h + 1]
        updates.append((s_ref, h, sdec_ref[0:1, h:h + 1] * s_prev, krw, vr))
        yr = _head_layer_norm(inner + cross, nr_ref[:, lo:hi]) * (gr * jax.nn.sigmoid(gr))
        h_ref[rows,W_HEADS + lo:W_HEADS + hi] = yr

    if not short:
        for ref, h, decayed, kwt, val in updates:
            ref[bi, h] = decayed + _dot_tn(kwt.astype(BF16), val.astype(BF16))
    else:
        pad_rows = LANES - len(updates) * lc
        keys = jnp.concatenate([u[3] for u in updates] + [jnp.zeros((pad_rows, D_HEAD), F32)], axis=0)
        keys_t = keys.T.astype(BF16)
        for idx, (ref, h, decayed, _, val) in enumerate(updates):
            band = jnp.concatenate(
                [jnp.zeros((idx * lc, D_HEAD), F32)] * (idx > 0) + [val]
                + [jnp.zeros((LANES - (idx + 1) * lc, D_HEAD), F32)], axis=0)
            ref[bi, h] = decayed + jnp.dot(keys_t, band.astype(BF16), preferred_element_type=F32)


def _retention_tables(lc):
    lg = jnp.log1p(-(2.0 ** (-5.0 - jnp.arange(N_HEADS, dtype=F32))))
    idx = jnp.arange(lc, dtype=F32)
    diff = idx[:, None] - idx[None, :]
    decay = jnp.where(diff >= 0, jnp.exp(jnp.maximum(diff, 0.0)[None] * lg[:, None, None]), 0.0)
    cross = jnp.exp((idx[:, None] + 1.0) * lg[None, :])
    w_k = jnp.exp((lc - 1.0 - idx)[:, None] * lg[None, :])
    rtab = jnp.zeros((lc, GATE_COLS), F32).at[:, :N_HEADS].set(cross).at[:, N_HEADS:2 * N_HEADS].set(w_k)
    sdec = jnp.zeros((SUBLANES, GATE_COLS), F32).at[0, :N_HEADS].set(jnp.exp(lc * lg))
    return decay, rtab, sdec


def _rope_tables(pos):
    half = D_HEAD // 2
    freqs = ROPE_BASE ** (-jnp.arange(half, dtype=F32) / half)
    ang = pos[:, None] * freqs[None, :]
    return jnp.cos(ang), jnp.sin(ang)


def _mix_ab(z, zg, pos, conv0, c0, n0, m0, s0, conv_w, conv_b, gate_bias, norm_m, norm_r, *, bsz, length, row0, nb=1):
    lc = min(CHUNK, length)
    nc = length // lc
    assert nb == 1 or nc == 1
    rows = nb * lc
    blk0 = row0 // rows
    zw = z.shape[1]
    cos, sin = _rope_tables(pos)
    decay, rtab, sdec = _retention_tables(lc)
    full = lambda *shape: pl.BlockSpec(shape, lambda b, c: (0,) * len(shape))
    state4 = pl.BlockSpec((nb, N_HEADS, D_HEAD, D_HEAD), lambda b, c: (b, 0, 0, 0))
    state_n = pl.BlockSpec((nb, N_HEADS, D_HEAD), lambda b, c: (b, 0, 0))
    state_m = pl.BlockSpec((nb, 1, N_HEADS), lambda b, c: (b, 0, 0))
    state_conv = pl.BlockSpec((nb, CONV_W - 1, 2 * W_HEADS), lambda b, c: (b, 0, 0))
    outs = pl.pallas_call(
        functools.partial(_ab_kernel, lc=lc, nb=nb),
        grid=(bsz // nb, nc),
        in_specs=[
            pl.BlockSpec((rows, zw), lambda b, c: (blk0 + b * nc + c, 0)),
            pl.BlockSpec((rows, GATE_COLS), lambda b, c: (blk0 + b * nc + c, 0)),
            full(CONV_W, 2 * W_HEADS), full(1, 2 * W_HEADS), full(1, GATE_COLS),
            full(1, W_HEADS), full(1, W_HEADS),
            pl.BlockSpec((lc, D_HEAD // 2), lambda b, c: (c, 0)),
            pl.BlockSpec((lc, D_HEAD // 2), lambda b, c: (c, 0)),
            full(N_HEADS, lc, lc), full(lc, GATE_COLS), full(SUBLANES, GATE_COLS),
            state4, state_n, state_m, state_conv, state4,
        ],
        out_specs=[
            pl.BlockSpec((rows, 2 * W_HEADS), lambda b, c: (b * nc + c, 0)),
            state4, state_n, state_m, state_conv, state4,
        ],
        out_shape=[
            jax.ShapeDtypeStruct((bsz * length, 2 * W_HEADS), F32),
            jax.ShapeDtypeStruct((bsz, N_HEADS, D_HEAD, D_HEAD), F32),
            jax.ShapeDtypeStruct((bsz, N_HEADS, D_HEAD), F32),
            jax.ShapeDtypeStruct((bsz, 1, N_HEADS), F32),
            jax.ShapeDtypeStruct((bsz, CONV_W - 1, 2 * W_HEADS), F32),
            jax.ShapeDtypeStruct((bsz, N_HEADS, D_HEAD, D_HEAD), F32),
        ],
        scratch_shapes=[pltpu.VMEM((nb, lc + CONV_PAD, 2 * W_HEADS), F32)],
        compiler_params=_params("parallel", "arbitrary"),
        name="mix_ab",
    )(z, zg, conv_w, conv_b.reshape(1, -1), gate_bias, norm_m.reshape(1, -1), norm_r.reshape(1, -1),
      cos, sin, decay, rtab, sdec, c0, n0, m0.reshape(bsz, 1, N_HEADS), conv0, s0)
    h, c_new, n_new, m_new, conv_new, s_new = outs
    return h, conv_new, c_new, n_new, m_new.reshape(bsz, N_HEADS), s_new


def _s5_prep_kernel(lr_ref, li_ref, ldt_ref, bre_ref, bim_ref, expand_ref, are_ref, aim_ref, bbre_ref, bbim_ref):
    lr = lr_ref[...]
    li = li_ref[...]
    dt = jnp.exp(ldt_ref[...])
    mag = jnp.exp(lr * dt)
    a_re = mag * jnp.cos(li * dt)
    a_im = mag * jnp.sin(li * dt)
    denom = lr * lr + li * li
    g_re = ((a_re - 1.0) * lr + a_im * li) / denom
    g_im = (a_im * lr - (a_re - 1.0) * li) / denom
    are_ref[...] = a_re
    aim_ref[...] = a_im
    ge_re = jnp.dot(g_re, expand_ref[...], preferred_element_type=F32, precision=lax.Precision.HIGHEST)
    ge_im = jnp.dot(g_im, expand_ref[...], preferred_element_type=F32, precision=lax.Precision.HIGHEST)
    bre = bre_ref[...]
    bim = bim_ref[...]
    bbre_ref[...] = ge_re * bre - ge_im * bim
    bbim_ref[...] = ge_re * bim + ge_im * bre


def _s5_prep(lam_re, lam_im, log_dt, bt_re, bt_im):
    g, p = lam_re.shape
    expand = jnp.tile(jnp.eye(p, dtype=F32), (1, S5_GROUP))
    return pl.pallas_call(
        _s5_prep_kernel,
        out_shape=[
            jax.ShapeDtypeStruct((g, p), F32), jax.ShapeDtypeStruct((g, p), F32),
            jax.ShapeDtypeStruct((g, p * S5_GROUP), F32), jax.ShapeDtypeStruct((g, p * S5_GROUP), F32),
        ],
        compiler_params=pltpu.CompilerParams(vmem_limit_bytes=VMEM_LIMIT_BYTES),
        name="s5_prep",
    )(lam_re, lam_im, log_dt.reshape(g, 1), bt_re.reshape(g, p * S5_GROUP), bt_im.reshape(g, p * S5_GROUP), expand)


S5_COLS = GROUPS_PER_BLOCK * S5_GROUP
S5_HALF = GROUPS_PER_BLOCK * S5_STATES
S5_TILES = S5_HALF // LANES
S5_COL_TILES = S5_COLS // LANES
S5_TILE_COLS = 2 * LANES


def _tile_major(re, im):
    lead = re.shape[:-1]
    pair = jnp.stack([re.reshape(lead + (S5_TILES, LANES)), im.reshape(lead + (S5_TILES, LANES))], axis=-2)
    return pair.reshape(lead + (2 * S5_HALF,))


def _s5_scan_dense(bu_ref, xn_ref, a_ref, nb, tt):
    a_re = [jnp.broadcast_to(a_ref[0, :, (2 * j) * LANES:(2 * j + 1) * LANES], (SUBLANES, LANES))
            for j in range(S5_TILES)]
    a_im = [jnp.broadcast_to(a_ref[0, :, (2 * j + 1) * LANES:(2 * j + 2) * LANES], (SUBLANES, LANES))
            for j in range(S5_TILES)]

    def group(s, _):
        r0 = pl.multiple_of(s * SUBLANES, SUBLANES)
        for j in range(S5_TILES):
            re_cols = slice((2 * j) * LANES, (2 * j + 1) * LANES)
            im_cols = slice((2 * j + 1) * LANES, (2 * j + 2) * LANES)
            xr = xn_ref[pl.ds(r0, SUBLANES), re_cols]
            xi = xn_ref[pl.ds(r0, SUBLANES), im_cols]
            for t in range(tt):
                rows = pl.ds(t * nb + r0, SUBLANES)
                xr, xi = (a_re[j] * xr - a_im[j] * xi + bu_ref[2 * j, rows, :],
                          a_re[j] * xi + a_im[j] * xr + bu_ref[2 * j + 1, rows, :])
                bu_ref[2 * j, rows, :] = xr
                bu_ref[2 * j + 1, rows, :] = xi
            xn_ref[pl.ds(r0, SUBLANES), re_cols] = xr
            xn_ref[pl.ds(r0, SUBLANES), im_cols] = xi
        return 0

    lax.fori_loop(0, nb // SUBLANES, group, 0)


def _s5_scan_pairs(bu, state_re, state_im, a_re, a_im, nb):
    upper = lax.broadcasted_iota(jnp.int32, (SUBLANES, LANES), 0) >= nb
    ar = jnp.broadcast_to(a_re, (SUBLANES, LANES))
    ai = jnp.broadcast_to(a_im, (SUBLANES, LANES))
    c1_re = jnp.where(upper, ar * ar - ai * ai, ar)
    c1_im = jnp.where(upper, 2.0 * ar * ai, ai)
    c2_re = jnp.where(upper, ar, 0.0)
    c2_im = jnp.where(upper, ai, 0.0)

    def both_halves(x):
        return jnp.where(upper, x, pltpu.roll(x, nb, axis=0))

    pr = jnp.where(upper, pltpu.roll(state_re, nb, axis=0), state_re)
    pi = jnp.where(upper, pltpu.roll(state_im, nb, axis=0), state_im)
    out = []
    for i in range(bu.shape[0] // SUBLANES):
        vr = bu[i * SUBLANES:(i + 1) * SUBLANES, :LANES]
        vi = bu[i * SUBLANES:(i + 1) * SUBLANES, LANES:]
        rr = pltpu.roll(vr, nb, axis=0)
        ri = pltpu.roll(vi, nb, axis=0)
        yr = c1_re * pr - c1_im * pi + (c2_re * rr - c2_im * ri) + vr
        yi = c1_re * pi + c1_im * pr + (c2_re * ri + c2_im * rr) + vi
        out.append(jnp.concatenate([yr, yi], axis=-1))
        pr, pi = both_halves(yr), both_halves(yi)
    return jnp.concatenate(out, axis=0), pr, pi


def _s5_kernel(*refs, nb, tt, interleave):
    n_u = nb * S5_COL_TILES if interleave else 1
    u_refs = refs[:n_u]
    bb_ref, cc_ref, a_ref, d_ref, x0_ref, y_ref, xn_ref = refs[n_u:n_u + 7]

    @pl.when(pl.program_id(1) == 0)
    def _():
        xn_ref[...] = x0_ref[...]

    if interleave:
        assert 2 * nb == SUBLANES
        ut_ref, yt_ref = refs[n_u + 7:]
        for b in range(nb):
            for c in range(S5_COL_TILES):
                ut_ref[c, pl.ds(b, tt, stride=nb), :] = u_refs[b * S5_COL_TILES + c][...]
        u = jnp.concatenate([ut_ref[c] for c in range(S5_COL_TILES)], axis=-1)
        ub = u.astype(BF16)
        y = d_ref[...] * u
        for j in range(S5_TILES):
            cols = slice(j * S5_TILE_COLS, (j + 1) * S5_TILE_COLS)
            re_cols = slice(j * S5_TILE_COLS, j * S5_TILE_COLS + LANES)
            im_cols = slice(j * S5_TILE_COLS + LANES, (j + 1) * S5_TILE_COLS)
            bb_j = jnp.concatenate([bb_ref[0, 2 * j], bb_ref[0, 2 * j + 1]], axis=-1)
            cc_j = jnp.concatenate([cc_ref[0, 2 * j], cc_ref[0, 2 * j + 1]], axis=0)
            bu = jnp.dot(ub, bb_j, preferred_element_type=F32)
            x, fin_re, fin_im = _s5_scan_pairs(bu, xn_ref[:, re_cols], xn_ref[:, im_cols],
                                               a_ref[0, :, re_cols], a_ref[0, :, im_cols], nb)
            xn_ref[:, re_cols] = fin_re
            xn_ref[:, im_cols] = fin_im
            y = y + jnp.dot(x.astype(BF16), cc_j, preferred_element_type=F32)
    else:
        assert nb % SUBLANES == 0
        (bu_ref,) = refs[n_u + 7:]
        u = u_refs[0][...]
        bb_all = jnp.concatenate([bb_ref[0, k] for k in range(2 * S5_TILES)], axis=-1)
        cc_all = jnp.concatenate([cc_ref[0, k] for k in range(2 * S5_TILES)], axis=0)
        bu = jnp.dot(u.astype(BF16), bb_all, preferred_element_type=F32)
        for j in range(2 * S5_TILES):
            bu_ref[j] = bu[:, j * LANES:(j + 1) * LANES]
        _s5_scan_dense(bu_ref, xn_ref, a_ref, nb, tt)
        x = jnp.concatenate([bu_ref[j] for j in range(2 * S5_TILES)], axis=-1)
        y = jnp.dot(x.astype(BF16), cc_all, preferred_element_type=F32) + d_ref[...] * u
    y = jax.nn.gelu(y)
    if interleave:
        for c in range(S5_COL_TILES):
            yt_ref[c] = y[:, c * LANES:(c + 1) * LANES]
        for b in range(nb):
            for c in range(S5_COL_TILES):
                y_ref[b, :, c * LANES:(c + 1) * LANES] = yt_ref[c, pl.ds(b, tt, stride=nb), :].astype(BF16)
    else:
        y_ref[...] = y.astype(BF16)


def _mix_s5(u, x0_re, x0_im, bb_blk, cc_blk, a_blk, d_skip, *, bsz, length, tt, row0, interleave):
    w = u.shape[-1]
    nblk = w // S5_COLS
    nt = length // tt
    x0 = _tile_major(x0_re.reshape(bsz, nblk, S5_HALF), x0_im.reshape(bsz, nblk, S5_HALF))
    srows = max(bsz, SUBLANES)
    x0 = jnp.pad(x0.transpose(1, 0, 2), ((0, 0), (0, srows - bsz), (0, 0)))
    rows = bsz * tt
    if interleave:
        blk0 = row0 // tt
        u_specs = [pl.BlockSpec((tt, LANES), functools.partial(
            lambda g, t, b, c: (blk0 + b * nt + t, g * S5_COL_TILES + c), b=b, c=c))
            for b in range(bsz) for c in range(S5_COL_TILES)]
        u_args = [u] * (bsz * S5_COL_TILES)
        y_spec = pl.BlockSpec((bsz, tt, S5_COLS), lambda g, t: (0, t, g))
        y_shape = jax.ShapeDtypeStruct((bsz, length, w), BF16)
        scratch = [pltpu.VMEM((S5_COL_TILES, rows, LANES), F32), pltpu.VMEM((S5_COL_TILES, rows, LANES), F32)]
    else:
        assert nt == 1
        blk0 = row0 // rows
        u_specs = [pl.BlockSpec((rows, S5_COLS), lambda g, t: (blk0, g))]
        u_args = [u]
        y_spec = pl.BlockSpec((rows, S5_COLS), lambda g, t: (0, g))
        y_shape = jax.ShapeDtypeStruct((rows, w), BF16)
        scratch = [pltpu.VMEM((2 * S5_TILES, rows, LANES), F32)]
    y, xn = pl.pallas_call(
        functools.partial(_s5_kernel, nb=bsz, tt=tt, interleave=interleave),
        grid=(nblk, nt),
        in_specs=u_specs + [
            pl.BlockSpec((1, 2 * S5_TILES, S5_COLS, LANES), lambda g, t: (g, 0, 0, 0)),
            pl.BlockSpec((1, 2 * S5_TILES, LANES, S5_COLS), lambda g, t: (g, 0, 0, 0)),
            pl.BlockSpec((1, 1, 2 * S5_HALF), lambda g, t: (g, 0, 0)),
            pl.BlockSpec((1, S5_COLS), lambda g, t: (0, g)),
            pl.BlockSpec((None, srows, 2 * S5_HALF), lambda g, t: (g, 0, 0)),
        ],
        out_specs=[y_spec, pl.BlockSpec((None, srows, 2 * S5_HALF), lambda g, t: (g, 0, 0))],
        out_shape=[y_shape, jax.ShapeDtypeStruct((nblk, srows, 2 * S5_HALF), F32)],
        scratch_shapes=scratch,
        compiler_params=_params("parallel", "arbitrary"),
        name="mix_s5",
    )(*u_args, bb_blk, cc_blk, a_blk, d_skip.reshape(1, w), x0)
    xn = xn[:, :bsz].transpose(1, 0, 2).reshape(bsz, nblk, S5_TILES, 2, LANES)
    g_total = nblk * GROUPS_PER_BLOCK
    xr = xn[..., 0, :].reshape(bsz, g_total, S5_STATES)
    xi = xn[..., 1, :].reshape(bsz, g_total, S5_STATES)
    return y, xr, xi


def _block_bands(blk_re, blk_im):
    g = blk_re.shape[0]
    nblk = g // GROUPS_PER_BLOCK
    assert LANES == 2 * S5_STATES
    shape = (nblk, S5_TILES, 2, S5_GROUP, S5_STATES)
    pair = jnp.stack([blk_re.reshape(shape), blk_im.reshape(shape)], axis=2)
    zero = jnp.zeros(pair.shape[:3] + (S5_GROUP, S5_STATES), F32)
    band = jnp.concatenate([jnp.concatenate([pair[:, :, :, 0], zero], axis=-1),
                            jnp.concatenate([zero, pair[:, :, :, 1]], axis=-1)], axis=-2)
    full = jnp.einsum('ntirl,st->ntisrl', band, jnp.eye(S5_TILES, dtype=F32))
    return full.reshape(nblk, 2 * S5_TILES, S5_COLS, LANES).astype(BF16)


def _s5_block_weights(a_re, a_im, bb_re, bb_im, c_re, c_im):
    g, p = a_re.shape
    nblk = g // GROUPS_PER_BLOCK
    bb = _block_bands(bb_re.reshape(g, S5_GROUP, p), bb_im.reshape(g, S5_GROUP, p))
    cc = jnp.swapaxes(_block_bands(c_re, -c_im), 2, 3)
    a_blk = _tile_major(a_re.reshape(nblk, S5_HALF), a_im.reshape(nblk, S5_HALF)).reshape(nblk, 1, 2 * S5_HALF)
    return bb, cc, a_blk


def kernel(x_prompt, x_sample, state_mlstm_C, state_mlstm_n, state_mlstm_m, state_mlstm_conv, state_ret_S, state_s5_re, state_s5_im, norm_ffn1, norm_mix, norm_ffn2, norm_final, ffn1_w_gate, ffn1_w_up, ffn1_w_down, ffn2_w_gate, ffn2_w_up, ffn2_w_down, ab_w_in, mlstm_b_i, mlstm_b_f, mlstm_conv_w, mlstm_conv_b, mlstm_norm, ret_norm, ab_w_out, s5_w_in, s5_lambda_re, s5_lambda_im, s5_log_dt, s5_B_re, s5_B_im, s5_C_re, s5_C_im, s5_D, s5_w_glu, s5_b_glu):
    bp, lp, d = x_prompt.shape
    bs, ls, _ = x_sample.shape
    mp, ms = bp * lp, bs * ls
    depth = norm_ffn1.shape[0]
    assert mp % TOKEN_TILE == 0 and ms % TOKEN_TILE == 0

    ffn = functools.partial(_ffn, tm=TOKEN_TILE, tf=FFN_TILE)
    pos_p = jnp.arange(lp, dtype=F32)
    pos_s = PAST_LEN + jnp.arange(ls, dtype=F32)

    ys = (x_prompt.reshape(mp, d), x_sample.reshape(ms, d))
    out_mc, out_mn, out_mm, out_conv, out_rs, out_re, out_im = ([] for _ in range(7))
    for layer in range(depth):
        y = ffn(ys, norm_ffn1, ffn1_w_gate, ffn1_w_up, ffn1_w_down, layer)
        if layer % 2 == 0:
            e = layer // 2
            n_gate = 2 * N_HEADS
            wt = jnp.swapaxes(ab_w_in, 1, 2)[e]
            z, zg = _ab_proj(y, norm_mix, wt, layer, tm=AB_PROJ_ROWS, tn=PROJ_TILE)
            gate_bias = jnp.zeros((1, GATE_COLS), F32).at[0, :N_HEADS].set(mlstm_b_i[e]).at[
                0, N_HEADS:n_gate].set(mlstm_b_f[e])
            common = (mlstm_conv_w[e], mlstm_conv_b[e], gate_bias, mlstm_norm[e], ret_norm[e])
            hp, conv_p, c_p, n_p, m_p, s_p = _mix_ab(
                z, zg, pos_p, jnp.zeros((bp, CONV_W - 1, 2 * W_HEADS), F32),
                jnp.zeros((bp, N_HEADS, D_HEAD, D_HEAD), F32), jnp.zeros((bp, N_HEADS, D_HEAD), F32),
                jnp.full((bp, N_HEADS), M_INIT, F32), jnp.zeros((bp, N_HEADS, D_HEAD, D_HEAD), F32),
                *common, bsz=bp, length=lp, row0=0)
            hs, conv_s, c_s, n_s, m_s, s_s = _mix_ab(
                z, zg, pos_s, state_mlstm_conv[e], state_mlstm_C[e], state_mlstm_n[e], state_mlstm_m[e],
                state_ret_S[e], *common, bsz=bs, length=ls, row0=mp, nb=SAMPLE_BATCH_TILE)
            out_mc.append((c_p, c_s))
            out_mn.append((n_p, n_s))
            out_mm.append((m_p, m_s))
            out_conv.append((conv_p, conv_s))
            out_rs.append((s_p, s_s))
            y = _proj_res((hp, hs), ab_w_out[e], y, tm=TOKEN_TILE // 4)
        else:
            o = layer // 2
            u = _norm_proj(y, norm_mix, s5_w_in[o], layer, tm=TOKEN_TILE // 2)
            a_re, a_im, bb_re, bb_im = _s5_prep(s5_lambda_re[o], s5_lambda_im[o], s5_log_dt[o],
                                                jnp.swapaxes(s5_B_re[o], 1, 2), jnp.swapaxes(s5_B_im[o], 1, 2))
            bb, cc, a_blk = _s5_block_weights(a_re, a_im, bb_re, bb_im, s5_C_re[o], s5_C_im[o])
            zeros_p = jnp.zeros((bp,) + a_re.shape, F32)
            ya_p, re_p, im_p = _mix_s5(u, zeros_p, zeros_p, bb, cc, a_blk, s5_D[o], bsz=bp, length=lp,
                                       tt=S5_TIME_TILE, row0=0, interleave=True)
            u_s = u[mp:].reshape(bs, ls, -1).transpose(1, 0, 2).reshape(ms, -1)
            ya_s, re_s, im_s = _mix_s5(u_s, state_s5_re[o], state_s5_im[o], bb, cc, a_blk, s5_D[o], bsz=bs,
                                       length=ls, tt=ls, row0=0, interleave=False)
            ya_s = ya_s.reshape(ls, bs, -1).transpose(1, 0, 2).reshape(ms, -1)
            out_re.append((re_p, re_s))
            out_im.append((im_p, im_s))
            y = _glu((ya_p.reshape(mp, -1), ya_s), s5_w_glu, s5_b_glu, y, o, tm=TOKEN_TILE, tn=PROJ_TILE)
        ys = (ffn((y,), norm_ffn2, ffn2_w_gate, ffn2_w_up, ffn2_w_down, layer),)
    y = ys[0]
    y_p = _final_norm(y, norm_final, row0=0, m=mp, tm=TOKEN_TILE)
    y_s = _final_norm(y, norm_final, row0=mp, m=ms, tm=TOKEN_TILE)

    def both(pairs):
        return jnp.stack([p for p, _ in pairs]), jnp.stack([s for _, s in pairs])

    pc, sc = both(out_mc)
    pn, sn = both(out_mn)
    pm, sm = both(out_mm)
    pconv, sconv = both(out_conv)
    ps, ss = both(out_rs)
    pre, sre = both(out_re)
    pim, sim = both(out_im)
    return (y_p.reshape(bp, lp, d), y_s.reshape(bs, ls, d), pc, sc, pn, sn, pm, sm, pconv, sconv, ps, ss,
            pre, sre, pim, sim)
```

```python
import functools

import jax
import jax.numpy as jnp
from jax import lax
from jax.experimental import pallas as pl
from jax.experimental.pallas import tpu as pltpu

F32 = jnp.float32
BF16 = jnp.bfloat16

EPS = 1e-6
M_INIT = -1e30
CHUNK = 128
CONV_W = 4
ROPE_BASE = 10000.0
PAST_LEN = 16384
N_HEADS = 4
D_HEAD = 256
W_HEADS = N_HEADS * D_HEAD
S5_GROUP = 16
S5_STATES = 64
GROUPS_PER_BLOCK = 16
LANES = 128
SUBLANES = 8
GATE_COLS = LANES

VMEM_LIMIT_BYTES = 60 * 1024 * 1024

TOKEN_TILE = 1024
FFN_TILE = 256
PROJ_TILE = 512
AB_PROJ_ROWS = 1536
S5_TIME_TILE = 128
SAMPLE_BATCH_TILE = 4


def _params(*semantics):
    return pltpu.CompilerParams(dimension_semantics=semantics, vmem_limit_bytes=VMEM_LIMIT_BYTES)


def _single_buffered(shape, index_map):
    return pl.BlockSpec(shape, index_map, pipeline_mode=pl.Buffered(1))


def _rms_norm(x, g):
    return x * lax.rsqrt(jnp.mean(x * x, axis=-1, keepdims=True) + EPS) * g


def _two_group_specs(tm, width, n_a, single=False):
    mk = _single_buffered if single else pl.BlockSpec
    return [mk((tm, width), lambda i, *_: (jnp.minimum(i, n_a - 1), 0)),
            mk((tm, width), lambda i, *_: (jnp.maximum(i - n_a, 0), 0))]


def _for_row_group(i, n_a, a_ref, b_ref, fn):
    @pl.when(i < n_a)
    def _():
        fn(a_ref[...])

    @pl.when(i >= n_a)
    def _():
        fn(b_ref[...])


def _ffn_kernel(*refs, n_a):
    if n_a is None:
        x_ref, g_ref, wg_ref, wu_ref, wd_ref, o_ref, xn_ref = refs
    else:
        xa_ref, xb_ref, g_ref, wg_ref, wu_ref, wd_ref, o_ref, xn_ref = refs
    f = pl.program_id(1)

    def init(x):
        xn_ref[...] = _rms_norm(x, g_ref[...]).astype(BF16)
        o_ref[...] = x

    @pl.when(f == 0)
    def _():
        if n_a is None:
            init(x_ref[...])
        else:
            _for_row_group(pl.program_id(0), n_a, xa_ref, xb_ref, init)

    xn = xn_ref[...]
    a = jnp.dot(xn, wg_ref[...].astype(BF16), preferred_element_type=F32)
    b = jnp.dot(xn, wu_ref[...].astype(BF16), preferred_element_type=F32)
    h = (0.5 * (a * jax.nn.sigmoid(a)) * b).astype(BF16)
    o_ref[...] += jnp.dot(h, wd_ref[...].astype(BF16), preferred_element_type=F32)


def _ffn(xs, g, w_gate, w_up, w_down, layer, *, tm, tf):
    d = xs[0].shape[1]
    m = sum(x.shape[0] for x in xs)
    dff = w_gate.shape[2]
    if len(xs) == 1:
        n_a = None
        x_specs = [pl.BlockSpec((tm, d), lambda i, f: (i, 0))]
    else:
        n_a = xs[0].shape[0] // tm
        x_specs = _two_group_specs(tm, d, n_a, single=True)
    return pl.pallas_call(
        functools.partial(_ffn_kernel, n_a=n_a),
        grid=(m // tm, dff // tf),
        in_specs=x_specs + [
            pl.BlockSpec((None, 1, d), lambda i, f: (layer, 0, 0)),
            pl.BlockSpec((None, d, tf), lambda i, f: (layer, 0, f)),
            pl.BlockSpec((None, d, tf), lambda i, f: (layer, 0, f)),
            pl.BlockSpec((None, tf, d), lambda i, f: (layer, f, 0)),
        ],
        out_specs=pl.BlockSpec((tm, d), lambda i, f: (i, 0)),
        out_shape=jax.ShapeDtypeStruct((m, d), F32),
        scratch_shapes=[pltpu.VMEM((tm, d), BF16)],
        compiler_params=_params("parallel", "arbitrary"),
        name="ffn",
    )(*xs, g.reshape(g.shape[0], 1, d), w_gate, w_up, w_down)


def _ab_proj_kernel(x_ref, g_ref, wlo_ref, whi_ref, wgate_ref, z_ref, zg_ref, xn_ref, *, n_lo):
    j = pl.program_id(1)

    @pl.when(j == 0)
    def _():
        xn_ref[...] = _rms_norm(x_ref[...], g_ref[...]).astype(BF16)
        zg_ref[...] = _dot_nt(xn_ref[...], wgate_ref[...].astype(BF16))

    @pl.when(j < n_lo)
    def _():
        z_ref[...] = _dot_nt(xn_ref[...], wlo_ref[...].astype(BF16))

    @pl.when(j >= n_lo)
    def _():
        z_ref[...] = _dot_nt(xn_ref[...], whi_ref[...].astype(BF16))


def _ab_proj(x, g, wt, layer, *, tm, tn):
    m, d = x.shape
    n_lo = n_hi = 4 * W_HEADS // tn
    hi0 = 4 * W_HEADS + 2 * N_HEADS
    gate_blk = 4 * W_HEADS // GATE_COLS
    return pl.pallas_call(
        functools.partial(_ab_proj_kernel, n_lo=n_lo),
        grid=(m // tm, n_lo + n_hi),
        in_specs=[
            pl.BlockSpec((tm, d), lambda i, j: (i, 0)),
            pl.BlockSpec((None, 1, d), lambda i, j: (layer, 0, 0)),
            pl.BlockSpec((tn, d), lambda i, j: (jnp.minimum(j, n_lo - 1), 0)),
            pl.BlockSpec((pl.Element(tn), pl.Element(d)),
                         lambda i, j: (pl.multiple_of(hi0 + jnp.maximum(j - n_lo, 0) * tn, SUBLANES), 0)),
            pl.BlockSpec((GATE_COLS, d), lambda i, j: (gate_blk, 0)),
        ],
        out_specs=[
            pl.BlockSpec((tm, tn), lambda i, j: (i, j)),
            pl.BlockSpec((tm, GATE_COLS), lambda i, j: (i, 0)),
        ],
        out_shape=[
            jax.ShapeDtypeStruct((m, (n_lo + n_hi) * tn), F32),
            jax.ShapeDtypeStruct((m, GATE_COLS), F32),
        ],
        scratch_shapes=[pltpu.VMEM((tm, d), BF16)],
        compiler_params=_params("parallel", "arbitrary"),
        name="ab_proj",
    )(x, g.reshape(g.shape[0], 1, d), wt, wt, wt)


def _norm_proj_kernel(x_ref, g_ref, w_ref, o_ref, wb_ref):
    @pl.when(pl.program_id(0) == 0)
    def _():
        wb_ref[...] = w_ref[...].astype(BF16)

    xn = _rms_norm(x_ref[...], g_ref[...]).astype(BF16)
    o_ref[...] = jnp.dot(xn, wb_ref[...], preferred_element_type=F32)


def _norm_proj(x, g, w, layer, *, tm):
    m, d = x.shape
    n = w.shape[1]
    return pl.pallas_call(
        _norm_proj_kernel,
        grid=(m // tm,),
        in_specs=[
            pl.BlockSpec((tm, d), lambda i: (i, 0)),
            pl.BlockSpec((None, 1, d), lambda i: (layer, 0, 0)),
            _single_buffered((d, n), lambda i: (0, 0)),
        ],
        out_specs=pl.BlockSpec((tm, n), lambda i: (i, 0)),
        out_shape=jax.ShapeDtypeStruct((m, n), F32),
        scratch_shapes=[pltpu.VMEM((d, n), BF16)],
        compiler_params=_params("arbitrary"),
        name="norm_proj",
    )(x, g.reshape(g.shape[0], 1, d), w)


def _proj_res_kernel(aa_ref, ab_ref, w_ref, r_ref, o_ref, wb_ref, *, n_a):
    i = pl.program_id(0)

    @pl.when(i == 0)
    def _():
        wb_ref[...] = w_ref[...].astype(BF16)

    def run(a):
        o_ref[...] = r_ref[...] + jnp.dot(a.astype(BF16), wb_ref[...], preferred_element_type=F32)

    _for_row_group(i, n_a, aa_ref, ab_ref, run)


def _proj_res(a_pair, w, res, *, tm):
    k = a_pair[0].shape[1]
    m = res.shape[0]
    n = w.shape[1]
    n_a = a_pair[0].shape[0] // tm
    return pl.pallas_call(
        functools.partial(_proj_res_kernel, n_a=n_a),
        grid=(m // tm,),
        in_specs=_two_group_specs(tm, k, n_a) + [
            _single_buffered((k, n), lambda i: (0, 0)),
            pl.BlockSpec((tm, n), lambda i: (i, 0)),
        ],
        out_specs=pl.BlockSpec((tm, n), lambda i: (i, 0)),
        out_shape=jax.ShapeDtypeStruct((m, n), F32),
        scratch_shapes=[pltpu.VMEM((k, n), BF16)],
        compiler_params=_params("arbitrary"),
        name="proj_res",
    )(*a_pair, w, res)


def _glu_kernel(aa_ref, ab_ref, wv_ref, wg_ref, bv_ref, bg_ref, r_ref, o_ref, *, n_a):
    def run(a):
        v = jnp.dot(a, wv_ref[...].astype(BF16), preferred_element_type=F32) + bv_ref[...]
        t = jnp.dot(a, wg_ref[...].astype(BF16), preferred_element_type=F32) + bg_ref[...]
        o_ref[...] = r_ref[...] + v * jax.nn.sigmoid(t)

    _for_row_group(pl.program_id(0), n_a, aa_ref, ab_ref, run)


def _glu(a_pair, w, b, res, layer, *, tm, tn):
    k = a_pair[0].shape[1]
    m = res.shape[0]
    n = w.shape[2] // 2
    nj = n // tn
    n_a = a_pair[0].shape[0] // tm
    b3 = b.reshape(b.shape[0], 1, 2 * n)
    return pl.pallas_call(
        functools.partial(_glu_kernel, n_a=n_a),
        grid=(m // tm, nj),
        in_specs=_two_group_specs(tm, k, n_a) + [
            pl.BlockSpec((None, k, tn), lambda i, j: (layer, 0, j)),
            pl.BlockSpec((None, k, tn), lambda i, j: (layer, 0, j + nj)),
            pl.BlockSpec((None, 1, tn), lambda i, j: (layer, 0, j)),
            pl.BlockSpec((None, 1, tn), lambda i, j: (layer, 0, j + nj)),
            pl.BlockSpec((tm, tn), lambda i, j: (i, j)),
        ],
        out_specs=pl.BlockSpec((tm, tn), lambda i, j: (i, j)),
        out_shape=jax.ShapeDtypeStruct((m, n), F32),
        compiler_params=_params("parallel", "arbitrary"),
        name="glu",
    )(*a_pair, w, w, b3, b3, res)


def _final_norm_kernel(x_ref, g_ref, o_ref):
    o_ref[...] = _rms_norm(x_ref[...], g_ref[...])


def _final_norm(x, g, *, row0, m, tm):
    d = x.shape[1]
    blk0 = row0 // tm
    return pl.pallas_call(
        _final_norm_kernel,
        grid=(m // tm,),
        in_specs=[pl.BlockSpec((tm, d), lambda i: (blk0 + i, 0)), pl.BlockSpec((1, d), lambda i: (0, 0))],
        out_specs=pl.BlockSpec((tm, d), lambda i: (i, 0)),
        out_shape=jax.ShapeDtypeStruct((m, d), F32),
        compiler_params=_params("parallel"),
        name="final_norm",
    )(x, g.reshape(1, d))


def _head_layer_norm(x, g):
    xc = x - jnp.mean(x, axis=-1, keepdims=True)
    var = jnp.mean(xc * xc, axis=-1, keepdims=True)
    return xc * lax.rsqrt(var + EPS) * g


def _cumsum_rows(x, n):
    row = lax.broadcasted_iota(jnp.int32, x.shape, 0)
    k = 1
    while k < n:
        x = x + jnp.where(row >= k, pltpu.roll(x, k, axis=0), 0.0)
        k *= 2
    return x


def _dot_nt(a, b):
    return lax.dot_general(a, b, (((1,), (1,)), ((), ())), preferred_element_type=F32)


def _dot_tn(a, b):
    return lax.dot_general(a, b, (((0,), (0,)), ((), ())), preferred_element_type=F32)


CONV_PAD = SUBLANES


def _ab_kernel(z_ref, zg_ref, cw_ref, cb_ref, gb_ref, nm_ref, nr_ref, cos_ref, sin_ref, dec_ref, rtab_ref, sdec_ref,
               c0_ref, n0_ref, m0_ref, conv0_ref, s0_ref,
               h_ref, c_ref, n_ref, m_ref, conv_ref, s_ref,
               xp_ref, *, lc, nb):
    @pl.when(pl.program_id(1) == 0)
    def _():
        c_ref[...] = c0_ref[...]
        n_ref[...] = n0_ref[...]
        m_ref[...] = m0_ref[...]
        s_ref[...] = s0_ref[...]
        xp_ref[:, CONV_PAD - (CONV_W - 1):CONV_PAD, :] = conv0_ref[...]

    for bi in range(nb):
        _ab_sequence(bi, z_ref, zg_ref, cw_ref, cb_ref, gb_ref, nm_ref, nr_ref, cos_ref, sin_ref, dec_ref, rtab_ref,
                     sdec_ref, h_ref, c_ref, n_ref, m_ref, conv_ref, s_ref, xp_ref, lc=lc)


def _ab_sequence(bi, z_ref, zg_ref, cw_ref, cb_ref, gb_ref, nm_ref, nr_ref, cos_ref, sin_ref, dec_ref, rtab_ref,
                 sdec_ref, h_ref, c_ref, n_ref, m_ref, conv_ref, s_ref, xp_ref, *, lc):
    rows = slice(bi * lc, (bi + 1) * lc)
    pad = CONV_PAD

    xp_ref[bi, pad:pad + lc, :] = z_ref[rows, 0:2 * W_HEADS]
    qk = cb_ref[...]
    for j in range(CONV_W):
        qk = qk + xp_ref[bi, pad - (CONV_W - 1) + j:pad - (CONV_W - 1) + j + lc, :] * cw_ref[j:j + 1, :]
    new_buf = xp_ref[bi, pad + lc - (CONV_W - 1):pad + lc, :]
    xp_ref[bi, pad - (CONV_W - 1):pad, :] = new_buf
    conv_ref[bi] = new_buf
    qk = qk * jax.nn.sigmoid(qk)

    row = lax.broadcasted_iota(jnp.int32, (lc, lc), 0)
    col = lax.broadcasted_iota(jnp.int32, (lc, lc), 1)
    causal = row >= col
    eye = row == col
    short = 2 * N_HEADS * lc <= LANES

    def row_of(c):
        return jnp.sum(jnp.where(eye, c, 0.0), axis=0, keepdims=True)

    gates = zg_ref[rows, :] + gb_ref[...]
    lf = jax.nn.log_sigmoid(gates)
    if short:
        bcum = _cumsum_rows(lf, lc)
    else:
        bcum = jnp.dot(causal.astype(F32), lf, preferred_element_type=F32, precision=lax.Precision.HIGHEST)
    updates = []

    scale = D_HEAD ** -0.5
    cos = cos_ref[...]
    sin = sin_ref[...]

    def rope(x):
        x1 = x[:, :D_HEAD // 2]
        x2 = x[:, D_HEAD // 2:]
        return jnp.concatenate([x1 * cos - x2 * sin, x1 * sin + x2 * cos], axis=-1)

    for h in range(N_HEADS):
        lo, hi = h * D_HEAD, (h + 1) * D_HEAD
        q = qk[:, lo:hi]
        k = qk[:, W_HEADS + lo:W_HEADS + hi] * scale
        v = z_ref[rows,2 * W_HEADS + lo:2 * W_HEADS + hi]
        og = z_ref[rows,3 * W_HEADS + lo:3 * W_HEADS + hi]
        qb, kb, vb = q.astype(BF16), k.astype(BF16), v.astype(BF16)
        c_prev = c_ref[bi, h]
        n_prev = n_ref[bi, h:h + 1, :]
        m_prev = m_ref[bi, :, h:h + 1]
        ig_c = gates[:, h:h + 1]
        b_c = bcum[:, N_HEADS + h:N_HEADS + h + 1]
        dlog = jnp.where(causal, b_c - row_of(b_c) + row_of(ig_c), -jnp.inf)
        s_log = b_c + m_prev
        m_row = jnp.maximum(s_log, jnp.max(dlog, axis=-1, keepdims=True))
        w = jnp.exp(dlog - m_row) * _dot_nt(qb, kb)
        sc = jnp.exp(s_log - m_row)
        num = sc * jnp.dot(qb, c_prev.astype(BF16), preferred_element_type=F32) + jnp.dot(
            w.astype(BF16), vb, preferred_element_type=F32)
        den = sc * jnp.sum(q * n_prev, axis=-1, keepdims=True) + jnp.sum(w, axis=-1, keepdims=True)
        hm = num / jnp.maximum(jnp.abs(den), jnp.exp(-m_row))
        m_new = m_row[lc - 1:lc, :]
        b_last = b_c[lc - 1:lc, :]
        w_state = jnp.exp(b_last + m_prev - m_new)
        kw = k * jnp.exp(b_last - b_c + ig_c - m_new)
        updates.append((c_ref, h, w_state * c_prev, kw, v))
        n_ref[bi, h:h + 1, :] = w_state * n_prev + jnp.sum(kw, axis=0, keepdims=True)
        m_ref[bi, :, h:h + 1] = m_new
        hm = _head_layer_norm(hm, nm_ref[:, lo:hi]) * jax.nn.sigmoid(og)
        h_ref[rows,lo:hi] = hm
    for h in range(N_HEADS):
        lo, hi = h * D_HEAD, (h + 1) * D_HEAD
        qr = rope(z_ref[rows,4 * W_HEADS + lo:4 * W_HEADS + hi])
        kr = rope(z_ref[rows,5 * W_HEADS + lo:5 * W_HEADS + hi]) * scale
        vr = z_ref[rows,6 * W_HEADS + lo:6 * W_HEADS + hi]
        gr = z_ref[rows,7 * W_HEADS + lo:7 * W_HEADS + hi]
        qrb, krb, vrb = qr.astype(BF16), kr.astype(BF16), vr.astype(BF16)
        s_prev = s_ref[bi, h]
        att = _dot_nt(qrb, krb) * dec_ref[h]
        inner = jnp.dot(att.astype(BF16), vrb, preferred_element_type=F32)
        cross = jnp.dot(qrb, s_prev.astype(BF16), preferred_element_type=F32) * rtab_ref[:, h:h + 1]
        krw = kr * rtab_ref[:, N_HEADS + h:N_HEADS + h + 1]
        updates.append((s_ref, h, sdec_ref[0:1, h:h + 1] * s_prev, krw, vr))
        yr = _head_layer_norm(inner + cross, nr_ref[:, lo:hi]) * (gr * jax.nn.sigmoid(gr))
        h_ref[rows,W_HEADS + lo:W_HEADS + hi] = yr

    if not short:
        for ref, h, decayed, kwt, val in updates:
            ref[bi, h] = decayed + _dot_tn(kwt.astype(BF16), val.astype(BF16))
    else:
        pad_rows = LANES - len(updates) * lc
        keys = jnp.concatenate([u[3] for u in updates] + [jnp.zeros((pad_rows, D_HEAD), F32)], axis=0)
        keys_t = keys.T.astype(BF16)
        for idx, (ref, h, decayed, _, val) in enumerate(updates):
            band = jnp.concatenate(
                [jnp.zeros((idx * lc, D_HEAD), F32)] * (idx > 0) + [val]
                + [jnp.zeros((LANES - (idx + 1) * lc, D_HEAD), F32)], axis=0)
            ref[bi, h] = decayed + jnp.dot(keys_t, band.astype(BF16), preferred_element_type=F32)


def _retention_tables(lc):
    lg = jnp.log1p(-(2.0 ** (-5.0 - jnp.arange(N_HEADS, dtype=F32))))
    idx = jnp.arange(lc, dtype=F32)
    diff = idx[:, None] - idx[None, :]
    decay = jnp.where(diff >= 0, jnp.exp(jnp.maximum(diff, 0.0)[None] * lg[:, None, None]), 0.0)
    cross = jnp.exp((idx[:, None] + 1.0) * lg[None, :])
    w_k = jnp.exp((lc - 1.0 - idx)[:, None] * lg[None, :])
    rtab = jnp.zeros((lc, GATE_COLS), F32).at[:, :N_HEADS].set(cross).at[:, N_HEADS:2 * N_HEADS].set(w_k)
    sdec = jnp.zeros((SUBLANES, GATE_COLS), F32).at[0, :N_HEADS].set(jnp.exp(lc * lg))
    return decay, rtab, sdec


def _rope_tables(pos):
    half = D_HEAD // 2
    freqs = ROPE_BASE ** (-jnp.arange(half, dtype=F32) / half)
    ang = pos[:, None] * freqs[None, :]
    return jnp.cos(ang), jnp.sin(ang)


def _mix_ab(z, zg, pos, conv0, c0, n0, m0, s0, conv_w, conv_b, gate_bias, norm_m, norm_r, *, bsz, length, row0, nb=1):
    lc = min(CHUNK, length)
    nc = length // lc
    assert nb == 1 or nc == 1
    rows = nb * lc
    blk0 = row0 // rows
    zw = z.shape[1]
    cos, sin = _rope_tables(pos)
    decay, rtab, sdec = _retention_tables(lc)
    full = lambda *shape: pl.BlockSpec(shape, lambda b, c: (0,) * len(shape))
    state4 = pl.BlockSpec((nb, N_HEADS, D_HEAD, D_HEAD), lambda b, c: (b, 0, 0, 0))
    state_n = pl.BlockSpec((nb, N_HEADS, D_HEAD), lambda b, c: (b, 0, 0))
    state_m = pl.BlockSpec((nb, 1, N_HEADS), lambda b, c: (b, 0, 0))
    state_conv = pl.BlockSpec((nb, CONV_W - 1, 2 * W_HEADS), lambda b, c: (b, 0, 0))
    outs = pl.pallas_call(
        functools.partial(_ab_kernel, lc=lc, nb=nb),
        grid=(bsz // nb, nc),
        in_specs=[
            pl.BlockSpec((rows, zw), lambda b, c: (blk0 + b * nc + c, 0)),
            pl.BlockSpec((rows, GATE_COLS), lambda b, c: (blk0 + b * nc + c, 0)),
            full(CONV_W, 2 * W_HEADS), full(1, 2 * W_HEADS), full(1, GATE_COLS),
            full(1, W_HEADS), full(1, W_HEADS),
            pl.BlockSpec((lc, D_HEAD // 2), lambda b, c: (c, 0)),
            pl.BlockSpec((lc, D_HEAD // 2), lambda b, c: (c, 0)),
            full(N_HEADS, lc, lc), full(lc, GATE_COLS), full(SUBLANES, GATE_COLS),
            state4, state_n, state_m, state_conv, state4,
        ],
        out_specs=[
            pl.BlockSpec((rows, 2 * W_HEADS), lambda b, c: (b * nc + c, 0)),
            state4, state_n, state_m, state_conv, state4,
        ],
        out_shape=[
            jax.ShapeDtypeStruct((bsz * length, 2 * W_HEADS), F32),
            jax.ShapeDtypeStruct((bsz, N_HEADS, D_HEAD, D_HEAD), F32),
            jax.ShapeDtypeStruct((bsz, N_HEADS, D_HEAD), F32),
            jax.ShapeDtypeStruct((bsz, 1, N_HEADS), F32),
            jax.ShapeDtypeStruct((bsz, CONV_W - 1, 2 * W_HEADS), F32),
            jax.ShapeDtypeStruct((bsz, N_HEADS, D_HEAD, D_HEAD), F32),
        ],
        scratch_shapes=[pltpu.VMEM((nb, lc + CONV_PAD, 2 * W_HEADS), F32)],
        compiler_params=_params("parallel", "arbitrary"),
        name="mix_ab",
    )(z, zg, conv_w, conv_b.reshape(1, -1), gate_bias, norm_m.reshape(1, -1), norm_r.reshape(1, -1),
      cos, sin, decay, rtab, sdec, c0, n0, m0.reshape(bsz, 1, N_HEADS), conv0, s0)
    h, c_new, n_new, m_new, conv_new, s_new = outs
    return h, conv_new, c_new, n_new, m_new.reshape(bsz, N_HEADS), s_new


def _s5_prep_kernel(lr_ref, li_ref, ldt_ref, bre_ref, bim_ref, expand_ref, are_ref, aim_ref, bbre_ref, bbim_ref):
    lr = lr_ref[...]
    li = li_ref[...]
    dt = jnp.exp(ldt_ref[...])
    mag = jnp.exp(lr * dt)
    a_re = mag * jnp.cos(li * dt)
    a_im = mag * jnp.sin(li * dt)
    denom = lr * lr + li * li
    g_re = ((a_re - 1.0) * lr + a_im * li) / denom
    g_im = (a_im * lr - (a_re - 1.0) * li) / denom
    are_ref[...] = a_re
    aim_ref[...] = a_im
    ge_re = jnp.dot(g_re, expand_ref[...], preferred_element_type=F32, precision=lax.Precision.HIGHEST)
    ge_im = jnp.dot(g_im, expand_ref[...], preferred_element_type=F32, precision=lax.Precision.HIGHEST)
    bre = bre_ref[...]
    bim = bim_ref[...]
    bbre_ref[...] = ge_re * bre - ge_im * bim
    bbim_ref[...] = ge_re * bim + ge_im * bre


def _s5_prep(lam_re, lam_im, log_dt, bt_re, bt_im):
    g, p = lam_re.shape
    expand = jnp.tile(jnp.eye(p, dtype=F32), (1, S5_GROUP))
    return pl.pallas_call(
        _s5_prep_kernel,
        out_shape=[
            jax.ShapeDtypeStruct((g, p), F32), jax.ShapeDtypeStruct((g, p), F32),
            jax.ShapeDtypeStruct((g, p * S5_GROUP), F32), jax.ShapeDtypeStruct((g, p * S5_GROUP), F32),
        ],
        compiler_params=pltpu.CompilerParams(vmem_limit_bytes=VMEM_LIMIT_BYTES),
        name="s5_prep",
    )(lam_re, lam_im, log_dt.reshape(g, 1), bt_re.reshape(g, p * S5_GROUP), bt_im.reshape(g, p * S5_GROUP), expand)


S5_COLS = GROUPS_PER_BLOCK * S5_GROUP
S5_HALF = GROUPS_PER_BLOCK * S5_STATES
S5_TILES = S5_HALF // LANES
S5_COL_TILES = S5_COLS // LANES
S5_TILE_COLS = 2 * LANES


def _tile_major(re, im):
    lead = re.shape[:-1]
    pair = jnp.stack([re.reshape(lead + (S5_TILES, LANES)), im.reshape(lead + (S5_TILES, LANES))], axis=-2)
    return pair.reshape(lead + (2 * S5_HALF,))


def _s5_scan_dense(bu_ref, xn_ref, a_ref, nb, tt):
    a_re = [jnp.broadcast_to(a_ref[0, :, (2 * j) * LANES:(2 * j + 1) * LANES], (SUBLANES, LANES))
            for j in range(S5_TILES)]
    a_im = [jnp.broadcast_to(a_ref[0, :, (2 * j + 1) * LANES:(2 * j + 2) * LANES], (SUBLANES, LANES))
            for j in range(S5_TILES)]

    def group(s, _):
        r0 = pl.multiple_of(s * SUBLANES, SUBLANES)
        for j in range(S5_TILES):
            re_cols = slice((2 * j) * LANES, (2 * j + 1) * LANES)
            im_cols = slice((2 * j + 1) * LANES, (2 * j + 2) * LANES)
            xr = xn_ref[pl.ds(r0, SUBLANES), re_cols]
            xi = xn_ref[pl.ds(r0, SUBLANES), im_cols]
            for t in range(tt):
                rows = pl.ds(t * nb + r0, SUBLANES)
                xr, xi = (a_re[j] * xr - a_im[j] * xi + bu_ref[2 * j, rows, :],
                          a_re[j] * xi + a_im[j] * xr + bu_ref[2 * j + 1, rows, :])
                bu_ref[2 * j, rows, :] = xr
                bu_ref[2 * j + 1, rows, :] = xi
            xn_ref[pl.ds(r0, SUBLANES), re_cols] = xr
            xn_ref[pl.ds(r0, SUBLANES), im_cols] = xi
        return 0

    lax.fori_loop(0, nb // SUBLANES, group, 0)


def _s5_scan_pairs(bu, state_re, state_im, a_re, a_im, nb):
    upper = lax.broadcasted_iota(jnp.int32, (SUBLANES, LANES), 0) >= nb
    ar = jnp.broadcast_to(a_re, (SUBLANES, LANES))
    ai = jnp.broadcast_to(a_im, (SUBLANES, LANES))
    c1_re = jnp.where(upper, ar * ar - ai * ai, ar)
    c1_im = jnp.where(upper, 2.0 * ar * ai, ai)
    c2_re = jnp.where(upper, ar, 0.0)
    c2_im = jnp.where(upper, ai, 0.0)

    def both_halves(x):
        return jnp.where(upper, x, pltpu.roll(x, nb, axis=0))

    pr = jnp.where(upper, pltpu.roll(state_re, nb, axis=0), state_re)
    pi = jnp.where(upper, pltpu.roll(state_im, nb, axis=0), state_im)
    out = []
    for i in range(bu.shape[0] // SUBLANES):
        vr = bu[i * SUBLANES:(i + 1) * SUBLANES, :LANES]
        vi = bu[i * SUBLANES:(i + 1) * SUBLANES, LANES:]
        rr = pltpu.roll(vr, nb, axis=0)
        ri = pltpu.roll(vi, nb, axis=0)
        yr = c1_re * pr - c1_im * pi + (c2_re * rr - c2_im * ri) + vr
        yi = c1_re * pi + c1_im * pr + (c2_re * ri + c2_im * rr) + vi
        out.append(jnp.concatenate([yr, yi], axis=-1))
        pr, pi = both_halves(yr), both_halves(yi)
    return jnp.concatenate(out, axis=0), pr, pi


def _s5_kernel(*refs, nb, tt, interleave):
    n_u = nb * S5_COL_TILES if interleave else 1
    u_refs = refs[:n_u]
    bb_ref, cc_ref, a_ref, d_ref, x0_ref, y_ref, xn_ref = refs[n_u:n_u + 7]

    @pl.when(pl.program_id(1) == 0)
    def _():
        xn_ref[...] = x0_ref[...]

    if interleave:
        assert 2 * nb == SUBLANES
        ut_ref, yt_ref = refs[n_u + 7:]
        for b in range(nb):
            for c in range(S5_COL_TILES):
                ut_ref[c, pl.ds(b, tt, stride=nb), :] = u_refs[b * S5_COL_TILES + c][...]
        u = jnp.concatenate([ut_ref[c] for c in range(S5_COL_TILES)], axis=-1)
        ub = u.astype(BF16)
        y = d_ref[...] * u
        for j in range(S5_TILES):
            cols = slice(j * S5_TILE_COLS, (j + 1) * S5_TILE_COLS)
            re_cols = slice(j * S5_TILE_COLS, j * S5_TILE_COLS + LANES)
            im_cols = slice(j * S5_TILE_COLS + LANES, (j + 1) * S5_TILE_COLS)
            bb_j = jnp.concatenate([bb_ref[0, 2 * j], bb_ref[0, 2 * j + 1]], axis=-1)
            cc_j = jnp.concatenate([cc_ref[0, 2 * j], cc_ref[0, 2 * j + 1]], axis=0)
            bu = jnp.dot(ub, bb_j, preferred_element_type=F32)
            x, fin_re, fin_im = _s5_scan_pairs(bu, xn_ref[:, re_cols], xn_ref[:, im_cols],
                                               a_ref[0, :, re_cols], a_ref[0, :, im_cols], nb)
            xn_ref[:, re_cols] = fin_re
            xn_ref[:, im_cols] = fin_im
            y = y + jnp.dot(x.astype(BF16), cc_j, preferred_element_type=F32)
    else:
        assert nb % SUBLANES == 0
        (bu_ref,) = refs[n_u + 7:]
        u = u_refs[0][...]
        bb_all = jnp.concatenate([bb_ref[0, k] for k in range(2 * S5_TILES)], axis=-1)
        cc_all = jnp.concatenate([cc_ref[0, k] for k in range(2 * S5_TILES)], axis=0)
        bu = jnp.dot(u.astype(BF16), bb_all, preferred_element_type=F32)
        for j in range(2 * S5_TILES):
            bu_ref[j] = bu[:, j * LANES:(j + 1) * LANES]
        _s5_scan_dense(bu_ref, xn_ref, a_ref, nb, tt)
        x = jnp.concatenate([bu_ref[j] for j in range(2 * S5_TILES)], axis=-1)
        y = jnp.dot(x.astype(BF16), cc_all, preferred_element_type=F32) + d_ref[...] * u
    y = jax.nn.gelu(y)
    if interleave:
        for c in range(S5_COL_TILES):
            yt_ref[c] = y[:, c * LANES:(c + 1) * LANES]
        for b in range(nb):
            for c in range(S5_COL_TILES):
                y_ref[b, :, c * LANES:(c + 1) * LANES] = yt_ref[c, pl.ds(b, tt, stride=nb), :].astype(BF16)
    else:
        y_ref[...] = y.astype(BF16)


def _mix_s5(u, x0_re, x0_im, bb_blk, cc_blk, a_blk, d_skip, *, bsz, length, tt, row0, interleave):
    w = u.shape[-1]
    nblk = w // S5_COLS
    nt = length // tt
    x0 = _tile_major(x0_re.reshape(bsz, nblk, S5_HALF), x0_im.reshape(bsz, nblk, S5_HALF))
    srows = max(bsz, SUBLANES)
    x0 = jnp.pad(x0.transpose(1, 0, 2), ((0, 0), (0, srows - bsz), (0, 0)))
    rows = bsz * tt
    if interleave:
        blk0 = row0 // tt
        u_specs = [pl.BlockSpec((tt, LANES), functools.partial(
            lambda g, t, b, c: (blk0 + b * nt + t, g * S5_COL_TILES + c), b=b, c=c))
            for b in range(bsz) for c in range(S5_COL_TILES)]
        u_args = [u] * (bsz * S5_COL_TILES)
        y_spec = pl.BlockSpec((bsz, tt, S5_COLS), lambda g, t: (0, t, g))
        y_shape = jax.ShapeDtypeStruct((bsz, length, w), BF16)
        scratch = [pltpu.VMEM((S5_COL_TILES, rows, LANES), F32), pltpu.VMEM((S5_COL_TILES, rows, LANES), F32)]
    else:
        assert nt == 1
        blk0 = row0 // rows
        u_specs = [pl.BlockSpec((rows, S5_COLS), lambda g, t: (blk0, g))]
        u_args = [u]
        y_spec = pl.BlockSpec((rows, S5_COLS), lambda g, t: (0, g))
        y_shape = jax.ShapeDtypeStruct((rows, w), BF16)
        scratch = [pltpu.VMEM((2 * S5_TILES, rows, LANES), F32)]
    y, xn = pl.pallas_call(
        functools.partial(_s5_kernel, nb=bsz, tt=tt, interleave=interleave),
        grid=(nblk, nt),
        in_specs=u_specs + [
            pl.BlockSpec((1, 2 * S5_TILES, S5_COLS, LANES), lambda g, t: (g, 0, 0, 0)),
            pl.BlockSpec((1, 2 * S5_TILES, LANES, S5_COLS), lambda g, t: (g, 0, 0, 0)),
            pl.BlockSpec((1, 1, 2 * S5_HALF), lambda g, t: (g, 0, 0)),
            pl.BlockSpec((1, S5_COLS), lambda g, t: (0, g)),
            pl.BlockSpec((None, srows, 2 * S5_HALF), lambda g, t: (g, 0, 0)),
        ],
        out_specs=[y_spec, pl.BlockSpec((None, srows, 2 * S5_HALF), lambda g, t: (g, 0, 0))],
        out_shape=[y_shape, jax.ShapeDtypeStruct((nblk, srows, 2 * S5_HALF), F32)],
        scratch_shapes=scratch,
        compiler_params=_params("parallel", "arbitrary"),
        name="mix_s5",
    )(*u_args, bb_blk, cc_blk, a_blk, d_skip.reshape(1, w), x0)
    xn = xn[:, :bsz].transpose(1, 0, 2).reshape(bsz, nblk, S5_TILES, 2, LANES)
    g_total = nblk * GROUPS_PER_BLOCK
    xr = xn[..., 0, :].reshape(bsz, g_total, S5_STATES)
    xi = xn[..., 1, :].reshape(bsz, g_total, S5_STATES)
    return y, xr, xi


def _block_bands(blk_re, blk_im):
    g = blk_re.shape[0]
    nblk = g // GROUPS_PER_BLOCK
    assert LANES == 2 * S5_STATES
    shape = (nblk, S5_TILES, 2, S5_GROUP, S5_STATES)
    pair = jnp.stack([blk_re.reshape(shape), blk_im.reshape(shape)], axis=2)
    zero = jnp.zeros(pair.shape[:3] + (S5_GROUP, S5_STATES), F32)
    band = jnp.concatenate([jnp.concatenate([pair[:, :, :, 0], zero], axis=-1),
                            jnp.concatenate([zero, pair[:, :, :, 1]], axis=-1)], axis=-2)
    full = jnp.einsum('ntirl,st->ntisrl', band, jnp.eye(S5_TILES, dtype=F32))
    return full.reshape(nblk, 2 * S5_TILES, S5_COLS, LANES).astype(BF16)


def _s5_block_weights(a_re, a_im, bb_re, bb_im, c_re, c_im):
    g, p = a_re.shape
    nblk = g // GROUPS_PER_BLOCK
    bb = _block_bands(bb_re.reshape(g, S5_GROUP, p), bb_im.reshape(g, S5_GROUP, p))
    cc = jnp.swapaxes(_block_bands(c_re, -c_im), 2, 3)
    a_blk = _tile_major(a_re.reshape(nblk, S5_HALF), a_im.reshape(nblk, S5_HALF)).reshape(nblk, 1, 2 * S5_HALF)
    return bb, cc, a_blk


def kernel(x_prompt, x_sample, state_mlstm_C, state_mlstm_n, state_mlstm_m, state_mlstm_conv, state_ret_S, state_s5_re, state_s5_im, norm_ffn1, norm_mix, norm_ffn2, norm_final, ffn1_w_gate, ffn1_w_up, ffn1_w_down, ffn2_w_gate, ffn2_w_up, ffn2_w_down, ab_w_in, mlstm_b_i, mlstm_b_f, mlstm_conv_w, mlstm_conv_b, mlstm_norm, ret_norm, ab_w_out, s5_w_in, s5_lambda_re, s5_lambda_im, s5_log_dt, s5_B_re, s5_B_im, s5_C_re, s5_C_im, s5_D, s5_w_glu, s5_b_glu):
    bp, lp, d = x_prompt.shape
    bs, ls, _ = x_sample.shape
    mp, ms = bp * lp, bs * ls
    depth = norm_ffn1.shape[0]
    assert mp % TOKEN_TILE == 0 and ms % TOKEN_TILE == 0

    ffn = functools.partial(_ffn, tm=TOKEN_TILE, tf=FFN_TILE)
    pos_p = jnp.arange(lp, dtype=F32)
    pos_s = PAST_LEN + jnp.arange(ls, dtype=F32)

    ys = (x_prompt.reshape(mp, d), x_sample.reshape(ms, d))
    out_mc, out_mn, out_mm, out_conv, out_rs, out_re, out_im = ([] for _ in range(7))
    for layer in range(depth):
        y = ffn(ys, norm_ffn1, ffn1_w_gate, ffn1_w_up, ffn1_w_down, layer)
        if layer % 2 == 0:
            e = layer // 2
            n_gate = 2 * N_HEADS
            wt = jnp.swapaxes(ab_w_in, 1, 2)[e]
            z, zg = _ab_proj(y, norm_mix, wt, layer, tm=AB_PROJ_ROWS, tn=PROJ_TILE)
            gate_bias = jnp.zeros((1, GATE_COLS), F32).at[0, :N_HEADS].set(mlstm_b_i[e]).at[
                0, N_HEADS:n_gate].set(mlstm_b_f[e])
            common = (mlstm_conv_w[e], mlstm_conv_b[e], gate_bias, mlstm_norm[e], ret_norm[e])
            hp, conv_p, c_p, n_p, m_p, s_p = _mix_ab(
                z, zg, pos_p, jnp.zeros((bp, CONV_W - 1, 2 * W_HEADS), F32),
                jnp.zeros((bp, N_HEADS, D_HEAD, D_HEAD), F32), jnp.zeros((bp, N_HEADS, D_HEAD), F32),
                jnp.full((bp, N_HEADS), M_INIT, F32), jnp.zeros((bp, N_HEADS, D_HEAD, D_HEAD), F32),
                *common, bsz=bp, length=lp, row0=0)
            hs, conv_s, c_s, n_s, m_s, s_s = _mix_ab(
                z, zg, pos_s, state_mlstm_conv[e], state_mlstm_C[e], state_mlstm_n[e], state_mlstm_m[e],
                state_ret_S[e], *common, bsz=bs, length=ls, row0=mp, nb=SAMPLE_BATCH_TILE)
            out_mc.append((c_p, c_s))
            out_mn.append((n_p, n_s))
            out_mm.append((m_p, m_s))
            out_conv.append((conv_p, conv_s))
            out_rs.append((s_p, s_s))
            y = _proj_res((hp, hs), ab_w_out[e], y, tm=TOKEN_TILE // 4)
        else:
            o = layer // 2
            u = _norm_proj(y, norm_mix, s5_w_in[o], layer, tm=TOKEN_TILE // 2)
            a_re, a_im, bb_re, bb_im = _s5_prep(s5_lambda_re[o], s5_lambda_im[o], s5_log_dt[o],
                                                jnp.swapaxes(s5_B_re[o], 1, 2), jnp.swapaxes(s5_B_im[o], 1, 2))
            bb, cc, a_blk = _s5_block_weights(a_re, a_im, bb_re, bb_im, s5_C_re[o], s5_C_im[o])
            zeros_p = jnp.zeros((bp,) + a_re.shape, F32)
            ya_p, re_p, im_p = _mix_s5(u, zeros_p, zeros_p, bb, cc, a_blk, s5_D[o], bsz=bp, length=lp,
                                       tt=S5_TIME_TILE, row0=0, interleave=True)
            u_s = u[mp:].reshape(bs, ls, -1).transpose(1, 0, 2).reshape(ms, -1)
            ya_s, re_s, im_s = _mix_s5(u_s, state_s5_re[o], state_s5_im[o], bb, cc, a_blk, s5_D[o], bsz=bs,
                                       length=ls, tt=ls, row0=0, interleave=False)
            ya_s = ya_s.reshape(ls, bs, -1).transpose(1, 0, 2).reshape(ms, -1)
            out_re.append((re_p, re_s))
            out_im.append((im_p, im_s))
            y = _glu((ya_p.reshape(mp, -1), ya_s), s5_w_glu, s5_b_glu, y, o, tm=TOKEN_TILE, tn=PROJ_TILE)
        ys = (ffn((y,), norm_ffn2, ffn2_w_gate, ffn2_w_up, ffn2_w_down, layer),)
    y = ys[0]
    y_p = _final_norm(y, norm_final, row0=0, m=mp, tm=TOKEN_TILE)
    y_s = _final_norm(y, norm_final, row0=mp, m=ms, tm=TOKEN_TILE)

    def both(pairs):
        return jnp.stack([p for p, _ in pairs]), jnp.stack([s for _, s in pairs])

    pc, sc = both(out_mc)
    pn, sn = both(out_mn)
    pm, sm = both(out_mm)
    pconv, sconv = both(out_conv)
    ps, ss = both(out_rs)
    pre, sre = both(out_re)
    pim, sim = both(out_im)
    return (y_p.reshape(bp, lp, d), y_s.reshape(bs, ls, d), pc, sc, pn, sn, pm, sm, pconv, sconv, ps, ss,
            pre, sre, pim, sim)
```

```python
import functools

import jax
import jax.numpy as jnp
from jax import lax
from jax.experimental import pallas as pl
from jax.experimental.pallas import tpu as pltpu

F32 = jnp.float32
BF16 = jnp.bfloat16

EPS = 1e-6
M_INIT = -1e30
CHUNK = 128
CONV_W = 4
ROPE_BASE = 10000.0
PAST_LEN = 16384
N_HEADS = 4
D_HEAD = 256
W_HEADS = N_HEADS * D_HEAD
S5_GROUP = 16
S5_STATES = 64
GROUPS_PER_BLOCK = 16
LANES = 128
SUBLANES = 8
BF16_ROWS = 16
GATE_COLS = LANES

VMEM_LIMIT_BYTES = 60 * 1024 * 1024

TOKEN_TILE = 1024
FFN_TILE = 256
PROJ_TILE = 512
AB_PROJ_ROWS = 1536
S5_TIME_TILE = 128
SAMPLE_BATCH_TILE = 4


def _params(*semantics):
    return pltpu.CompilerParams(dimension_semantics=semantics, vmem_limit_bytes=VMEM_LIMIT_BYTES)


def _single_buffered(shape, index_map):
    return pl.BlockSpec(shape, index_map, pipeline_mode=pl.Buffered(1))


def _rms_norm(x, g):
    return x * lax.rsqrt(jnp.mean(x * x, axis=-1, keepdims=True) + EPS) * g


def _two_group_specs(tm, width, n_a, single=False):
    mk = _single_buffered if single else pl.BlockSpec
    return [mk((tm, width), lambda i, *_: (jnp.minimum(i, n_a - 1), 0)),
            mk((tm, width), lambda i, *_: (jnp.maximum(i - n_a, 0), 0))]


def _for_row_group(i, n_a, a_ref, b_ref, fn):
    @pl.when(i < n_a)
    def _():
        fn(a_ref[...])

    @pl.when(i >= n_a)
    def _():
        fn(b_ref[...])


def _ffn_kernel(*refs, n_a):
    if n_a is None:
        x_ref, g_ref, wg_ref, wu_ref, wd_ref, o_ref, xn_ref = refs
    else:
        xa_ref, xb_ref, g_ref, wg_ref, wu_ref, wd_ref, o_ref, xn_ref = refs
    f = pl.program_id(1)

    def init(x):
        xn_ref[...] = _rms_norm(x, g_ref[...]).astype(BF16)
        o_ref[...] = x

    @pl.when(f == 0)
    def _():
        if n_a is None:
            init(x_ref[...])
        else:
            _for_row_group(pl.program_id(0), n_a, xa_ref, xb_ref, init)

    xn = xn_ref[...]
    a = jnp.dot(xn, wg_ref[...].astype(BF16), preferred_element_type=F32)
    b = jnp.dot(xn, wu_ref[...].astype(BF16), preferred_element_type=F32)
    h = (0.5 * (a * jax.nn.sigmoid(a)) * b).astype(BF16)
    o_ref[...] += jnp.dot(h, wd_ref[...].astype(BF16), preferred_element_type=F32)


def _ffn(xs, g, w_gate, w_up, w_down, layer, *, tm, tf):
    d = xs[0].shape[1]
    m = sum(x.shape[0] for x in xs)
    dff = w_gate.shape[2]
    if len(xs) == 1:
        n_a = None
        x_specs = [pl.BlockSpec((tm, d), lambda i, f: (i, 0))]
    else:
        n_a = xs[0].shape[0] // tm
        x_specs = _two_group_specs(tm, d, n_a, single=True)
    return pl.pallas_call(
        functools.partial(_ffn_kernel, n_a=n_a),
        grid=(m // tm, dff // tf),
        in_specs=x_specs + [
            pl.BlockSpec((None, 1, d), lambda i, f: (layer, 0, 0)),
            pl.BlockSpec((None, d, tf), lambda i, f: (layer, 0, f)),
            pl.BlockSpec((None, d, tf), lambda i, f: (layer, 0, f)),
            pl.BlockSpec((None, tf, d), lambda i, f: (layer, f, 0)),
        ],
        out_specs=pl.BlockSpec((tm, d), lambda i, f: (i, 0)),
        out_shape=jax.ShapeDtypeStruct((m, d), F32),
        scratch_shapes=[pltpu.VMEM((tm, d), BF16)],
        compiler_params=_params("parallel", "arbitrary"),
        name="ffn",
    )(*xs, g.reshape(g.shape[0], 1, d), w_gate, w_up, w_down)


def _ab_proj_kernel(x_ref, g_ref, wlo_ref, whi_ref, wgate_ref, z_ref, zg_ref, xn_ref, *, n_lo):
    j = pl.program_id(1)

    @pl.when(j == 0)
    def _():
        xn_ref[...] = _rms_norm(x_ref[...], g_ref[...]).astype(BF16)
        zg_ref[...] = _dot_nt(xn_ref[...], wgate_ref[...].astype(BF16))

    @pl.when(j < n_lo)
    def _():
        z_ref[...] = _dot_nt(xn_ref[...], wlo_ref[...].astype(BF16))

    @pl.when(j >= n_lo)
    def _():
        z_ref[...] = _dot_nt(xn_ref[...], whi_ref[...].astype(BF16))


def _ab_proj(x, g, wt, layer, *, tm, tn):
    m, d = x.shape
    n_lo = n_hi = 4 * W_HEADS // tn
    hi0 = 4 * W_HEADS + 2 * N_HEADS
    gate_blk = 4 * W_HEADS // GATE_COLS
    return pl.pallas_call(
        functools.partial(_ab_proj_kernel, n_lo=n_lo),
        grid=(m // tm, n_lo + n_hi),
        in_specs=[
            pl.BlockSpec((tm, d), lambda i, j: (i, 0)),
            pl.BlockSpec((None, 1, d), lambda i, j: (layer, 0, 0)),
            pl.BlockSpec((tn, d), lambda i, j: (jnp.minimum(j, n_lo - 1), 0)),
            pl.BlockSpec((pl.Element(tn), pl.Element(d)),
                         lambda i, j: (pl.multiple_of(hi0 + jnp.maximum(j - n_lo, 0) * tn, SUBLANES), 0)),
            pl.BlockSpec((GATE_COLS, d), lambda i, j: (gate_blk, 0)),
        ],
        out_specs=[
            pl.BlockSpec((tm, tn), lambda i, j: (i, j)),
            pl.BlockSpec((tm, GATE_COLS), lambda i, j: (i, 0)),
        ],
        out_shape=[
            jax.ShapeDtypeStruct((m, (n_lo + n_hi) * tn), F32),
            jax.ShapeDtypeStruct((m, GATE_COLS), F32),
        ],
        scratch_shapes=[pltpu.VMEM((tm, d), BF16)],
        compiler_params=_params("parallel", "arbitrary"),
        name="ab_proj",
    )(x, g.reshape(g.shape[0], 1, d), wt, wt, wt)


def _norm_proj_kernel(x_ref, g_ref, w_ref, o_ref, wb_ref):
    @pl.when(pl.program_id(0) == 0)
    def _():
        wb_ref[...] = w_ref[...].astype(BF16)

    xn = _rms_norm(x_ref[...], g_ref[...]).astype(BF16)
    o_ref[...] = jnp.dot(xn, wb_ref[...], preferred_element_type=F32)


def _norm_proj(x, g, w, layer, *, tm):
    m, d = x.shape
    n = w.shape[1]
    return pl.pallas_call(
        _norm_proj_kernel,
        grid=(m // tm,),
        in_specs=[
            pl.BlockSpec((tm, d), lambda i: (i, 0)),
            pl.BlockSpec((None, 1, d), lambda i: (layer, 0, 0)),
            _single_buffered((d, n), lambda i: (0, 0)),
        ],
        out_specs=pl.BlockSpec((tm, n), lambda i: (i, 0)),
        out_shape=jax.ShapeDtypeStruct((m, n), F32),
        scratch_shapes=[pltpu.VMEM((d, n), BF16)],
        compiler_params=_params("arbitrary"),
        name="norm_proj",
    )(x, g.reshape(g.shape[0], 1, d), w)


def _proj_res_kernel(aa_ref, ab_ref, w_ref, r_ref, o_ref, wb_ref, *, n_a):
    i = pl.program_id(0)

    @pl.when(i == 0)
    def _():
        wb_ref[...] = w_ref[...].astype(BF16)

    def run(a):
        o_ref[...] = r_ref[...] + jnp.dot(a.astype(BF16), wb_ref[...], preferred_element_type=F32)

    _for_row_group(i, n_a, aa_ref, ab_ref, run)


def _proj_res(a_pair, w, res, *, tm):
    k = a_pair[0].shape[1]
    m = res.shape[0]
    n = w.shape[1]
    n_a = a_pair[0].shape[0] // tm
    return pl.pallas_call(
        functools.partial(_proj_res_kernel, n_a=n_a),
        grid=(m // tm,),
        in_specs=_two_group_specs(tm, k, n_a) + [
            _single_buffered((k, n), lambda i: (0, 0)),
            pl.BlockSpec((tm, n), lambda i: (i, 0)),
        ],
        out_specs=pl.BlockSpec((tm, n), lambda i: (i, 0)),
        out_shape=jax.ShapeDtypeStruct((m, n), F32),
        scratch_shapes=[pltpu.VMEM((k, n), BF16)],
        compiler_params=_params("arbitrary"),
        name="proj_res",
    )(*a_pair, w, res)


def _glu_kernel(aa_ref, ab_ref, wv_ref, wg_ref, bv_ref, bg_ref, r_ref, o_ref, *, n_a):
    def run(a):
        v = jnp.dot(a, wv_ref[...].astype(BF16), preferred_element_type=F32) + bv_ref[...]
        t = jnp.dot(a, wg_ref[...].astype(BF16), preferred_element_type=F32) + bg_ref[...]
        o_ref[...] = r_ref[...] + v * jax.nn.sigmoid(t)

    _for_row_group(pl.program_id(0), n_a, aa_ref, ab_ref, run)


def _glu(a_pair, w, b, res, layer, *, tm, tn):
    k = a_pair[0].shape[1]
    m = res.shape[0]
    n = w.shape[2] // 2
    nj = n // tn
    n_a = a_pair[0].shape[0] // tm
    b3 = b.reshape(b.shape[0], 1, 2 * n)
    return pl.pallas_call(
        functools.partial(_glu_kernel, n_a=n_a),
        grid=(m // tm, nj),
        in_specs=_two_group_specs(tm, k, n_a) + [
            pl.BlockSpec((None, k, tn), lambda i, j: (layer, 0, j)),
            pl.BlockSpec((None, k, tn), lambda i, j: (layer, 0, j + nj)),
            pl.BlockSpec((None, 1, tn), lambda i, j: (layer, 0, j)),
            pl.BlockSpec((None, 1, tn), lambda i, j: (layer, 0, j + nj)),
            pl.BlockSpec((tm, tn), lambda i, j: (i, j)),
        ],
        out_specs=pl.BlockSpec((tm, tn), lambda i, j: (i, j)),
        out_shape=jax.ShapeDtypeStruct((m, n), F32),
        compiler_params=_params("parallel", "arbitrary"),
        name="glu",
    )(*a_pair, w, w, b3, b3, res)


def _final_norm_kernel(x_ref, g_ref, o_ref):
    o_ref[...] = _rms_norm(x_ref[...], g_ref[...])


def _final_norm(x, g, *, row0, m, tm):
    d = x.shape[1]
    blk0 = row0 // tm
    return pl.pallas_call(
        _final_norm_kernel,
        grid=(m // tm,),
        in_specs=[pl.BlockSpec((tm, d), lambda i: (blk0 + i, 0)), pl.BlockSpec((1, d), lambda i: (0, 0))],
        out_specs=pl.BlockSpec((tm, d), lambda i: (i, 0)),
        out_shape=jax.ShapeDtypeStruct((m, d), F32),
        compiler_params=_params("parallel"),
        name="final_norm",
    )(x, g.reshape(1, d))


def _head_layer_norm(x, g):
    xc = x - jnp.mean(x, axis=-1, keepdims=True)
    var = jnp.mean(xc * xc, axis=-1, keepdims=True)
    return xc * lax.rsqrt(var + EPS) * g


def _cumsum_rows(x, n):
    row = lax.broadcasted_iota(jnp.int32, x.shape, 0)
    k = 1
    while k < n:
        x = x + jnp.where(row >= k, pltpu.roll(x, k, axis=0), 0.0)
        k *= 2
    return x


def _dot_nt(a, b):
    return lax.dot_general(a, b, (((1,), (1,)), ((), ())), preferred_element_type=F32)


def _dot_tn(a, b):
    return lax.dot_general(a, b, (((0,), (0,)), ((), ())), preferred_element_type=F32)


CONV_PAD = SUBLANES


def _ab_kernel(z_ref, zg_ref, cw_ref, cb_ref, gb_ref, nm_ref, nr_ref, cos_ref, sin_ref, dec_ref, rtab_ref, sdec_ref,
               c0_ref, n0_ref, m0_ref, conv0_ref, s0_ref,
               h_ref, c_ref, n_ref, m_ref, conv_ref, s_ref,
               xp_ref, *, lc, nb):
    @pl.when(pl.program_id(1) == 0)
    def _():
        c_ref[...] = c0_ref[...]
        n_ref[...] = n0_ref[...]
        m_ref[...] = m0_ref[...]
        s_ref[...] = s0_ref[...]
        xp_ref[:, CONV_PAD - (CONV_W - 1):CONV_PAD, :] = conv0_ref[...]

    for bi in range(nb):
        _ab_sequence(bi, z_ref, zg_ref, cw_ref, cb_ref, gb_ref, nm_ref, nr_ref, cos_ref, sin_ref, dec_ref, rtab_ref,
                     sdec_ref, h_ref, c_ref, n_ref, m_ref, conv_ref, s_ref, xp_ref, lc=lc)


def _ab_sequence(bi, z_ref, zg_ref, cw_ref, cb_ref, gb_ref, nm_ref, nr_ref, cos_ref, sin_ref, dec_ref, rtab_ref,
                 sdec_ref, h_ref, c_ref, n_ref, m_ref, conv_ref, s_ref, xp_ref, *, lc):
    rows = slice(bi * lc, (bi + 1) * lc)
    pad = CONV_PAD

    xp_ref[bi, pad:pad + lc, :] = z_ref[rows, 0:2 * W_HEADS]
    qk = cb_ref[...]
    for j in range(CONV_W):
        qk = qk + xp_ref[bi, pad - (CONV_W - 1) + j:pad - (CONV_W - 1) + j + lc, :] * cw_ref[j:j + 1, :]
    new_buf = xp_ref[bi, pad + lc - (CONV_W - 1):pad + lc, :]
    xp_ref[bi, pad - (CONV_W - 1):pad, :] = new_buf
    conv_ref[bi] = new_buf
    qk = qk * jax.nn.sigmoid(qk)

    row = lax.broadcasted_iota(jnp.int32, (lc, lc), 0)
    col = lax.broadcasted_iota(jnp.int32, (lc, lc), 1)
    causal = row >= col
    eye = row == col
    short = 2 * N_HEADS * lc <= LANES

    def row_of(c):
        return jnp.sum(jnp.where(eye, c, 0.0), axis=0, keepdims=True)

    gates = zg_ref[rows, :] + gb_ref[...]
    lf = jax.nn.log_sigmoid(gates)
    if short:
        bcum = _cumsum_rows(lf, lc)
    else:
        bcum = jnp.dot(causal.astype(F32), lf, preferred_element_type=F32, precision=lax.Precision.HIGHEST)
    updates = []

    scale = D_HEAD ** -0.5
    cos = cos_ref[...]
    sin = sin_ref[...]

    def rope(x):
        x1 = x[:, :D_HEAD // 2]
        x2 = x[:, D_HEAD // 2:]
        return jnp.concatenate([x1 * cos - x2 * sin, x1 * sin + x2 * cos], axis=-1)

    for h in range(N_HEADS):
        lo, hi = h * D_HEAD, (h + 1) * D_HEAD
        q = qk[:, lo:hi]
        k = qk[:, W_HEADS + lo:W_HEADS + hi] * scale
        v = z_ref[rows,2 * W_HEADS + lo:2 * W_HEADS + hi]
        og = z_ref[rows,3 * W_HEADS + lo:3 * W_HEADS + hi]
        qb, kb, vb = q.astype(BF16), k.astype(BF16), v.astype(BF16)
        c_prev = c_ref[bi, h]
        n_prev = n_ref[bi, h:h + 1, :]
        m_prev = m_ref[bi, :, h:h + 1]
        ig_c = gates[:, h:h + 1]
        b_c = bcum[:, N_HEADS + h:N_HEADS + h + 1]
        dlog = jnp.where(causal, b_c - row_of(b_c) + row_of(ig_c), -jnp.inf)
        s_log = b_c + m_prev
        m_row = jnp.maximum(s_log, jnp.max(dlog, axis=-1, keepdims=True))
        w = jnp.exp(dlog - m_row) * _dot_nt(qb, kb)
        sc = jnp.exp(s_log - m_row)
        num = sc * jnp.dot(qb, c_prev.astype(BF16), preferred_element_type=F32) + jnp.dot(
            w.astype(BF16), vb, preferred_element_type=F32)
        den = sc * jnp.sum(q * n_prev, axis=-1, keepdims=True) + jnp.sum(w, axis=-1, keepdims=True)
        hm = num / jnp.maximum(jnp.abs(den), jnp.exp(-m_row))
        m_new = m_row[lc - 1:lc, :]
        b_last = b_c[lc - 1:lc, :]
        w_state = jnp.exp(b_last + m_prev - m_new)
        kw = k * jnp.exp(b_last - b_c + ig_c - m_new)
        updates.append((c_ref, h, w_state * c_prev, kw, v))
        n_ref[bi, h:h + 1, :] = w_state * n_prev + jnp.sum(kw, axis=0, keepdims=True)
        m_ref[bi, :, h:h + 1] = m_new
        hm = _head_layer_norm(hm, nm_ref[:, lo:hi]) * jax.nn.sigmoid(og)
        h_ref[rows,lo:hi] = hm.astype(h_ref.dtype)
    for h in range(N_HEADS):
        lo, hi = h * D_HEAD, (h + 1) * D_HEAD
        qr = rope(z_ref[rows,4 * W_HEADS + lo:4 * W_HEADS + hi])
        kr = rope(z_ref[rows,5 * W_HEADS + lo:5 * W_HEADS + hi]) * scale
        vr = z_ref[rows,6 * W_HEADS + lo:6 * W_HEADS + hi]
        gr = z_ref[rows,7 * W_HEADS + lo:7 * W_HEADS + hi]
        qrb, krb, vrb = qr.astype(BF16), kr.astype(BF16), vr.astype(BF16)
        s_prev = s_ref[bi, h]
        att = _dot_nt(qrb, krb) * dec_ref[h]
        inner = jnp.dot(att.astype(BF16), vrb, preferred_element_type=F32)
        cross = jnp.dot(qrb, s_prev.astype(BF16), preferred_element_type=F32) * rtab_ref[:, h:h + 1]
        krw = kr * rtab_ref[:, N_HEADS + h:N_HEADS + h + 1]
        updates.append((s_ref, h, sdec_ref[0:1, h:h + 1] * s_prev, krw, vr))
        yr = _head_layer_norm(inner + cross, nr_ref[:, lo:hi]) * (gr * jax.nn.sigmoid(gr))
        h_ref[rows,W_HEADS + lo:W_HEADS + hi] = yr.astype(h_ref.dtype)

    if not short:
        for ref, h, decayed, kwt, val in updates:
            ref[bi, h] = decayed + _dot_tn(kwt.astype(BF16), val.astype(BF16))
    else:
        pad_rows = LANES - len(updates) * lc
        keys = jnp.concatenate([u[3] for u in updates] + [jnp.zeros((pad_rows, D_HEAD), F32)], axis=0)
        keys_t = keys.T.astype(BF16)
        for idx, (ref, h, decayed, _, val) in enumerate(updates):
            band = jnp.concatenate(
                [jnp.zeros((idx * lc, D_HEAD), F32)] * (idx > 0) + [val]
                + [jnp.zeros((LANES - (idx + 1) * lc, D_HEAD), F32)], axis=0)
            ref[bi, h] = decayed + jnp.dot(keys_t, band.astype(BF16), preferred_element_type=F32)


def _retention_tables(lc):
    lg = jnp.log1p(-(2.0 ** (-5.0 - jnp.arange(N_HEADS, dtype=F32))))
    idx = jnp.arange(lc, dtype=F32)
    diff = idx[:, None] - idx[None, :]
    decay = jnp.where(diff >= 0, jnp.exp(jnp.maximum(diff, 0.0)[None] * lg[:, None, None]), 0.0)
    cross = jnp.exp((idx[:, None] + 1.0) * lg[None, :])
    w_k = jnp.exp((lc - 1.0 - idx)[:, None] * lg[None, :])
    rtab = jnp.zeros((lc, GATE_COLS), F32).at[:, :N_HEADS].set(cross).at[:, N_HEADS:2 * N_HEADS].set(w_k)
    sdec = jnp.zeros((SUBLANES, GATE_COLS), F32).at[0, :N_HEADS].set(jnp.exp(lc * lg))
    return decay, rtab, sdec


def _rope_tables(pos):
    half = D_HEAD // 2
    freqs = ROPE_BASE ** (-jnp.arange(half, dtype=F32) / half)
    ang = pos[:, None] * freqs[None, :]
    return jnp.cos(ang), jnp.sin(ang)


def _mix_ab(z, zg, pos, conv0, c0, n0, m0, s0, conv_w, conv_b, gate_bias, norm_m, norm_r, *, bsz, length, row0, nb=1):
    lc = min(CHUNK, length)
    nc = length // lc
    assert nb == 1 or nc == 1
    rows = nb * lc
    blk0 = row0 // rows
    zw = z.shape[1]
    cos, sin = _rope_tables(pos)
    decay, rtab, sdec = _retention_tables(lc)
    full = lambda *shape: pl.BlockSpec(shape, lambda b, c: (0,) * len(shape))
    state4 = pl.BlockSpec((nb, N_HEADS, D_HEAD, D_HEAD), lambda b, c: (b, 0, 0, 0))
    state_n = pl.BlockSpec((nb, N_HEADS, D_HEAD), lambda b, c: (b, 0, 0))
    state_m = pl.BlockSpec((nb, 1, N_HEADS), lambda b, c: (b, 0, 0))
    state_conv = pl.BlockSpec((nb, CONV_W - 1, 2 * W_HEADS), lambda b, c: (b, 0, 0))
    outs = pl.pallas_call(
        functools.partial(_ab_kernel, lc=lc, nb=nb),
        grid=(bsz // nb, nc),
        in_specs=[
            pl.BlockSpec((rows, zw), lambda b, c: (blk0 + b * nc + c, 0)),
            pl.BlockSpec((rows, GATE_COLS), lambda b, c: (blk0 + b * nc + c, 0)),
            full(CONV_W, 2 * W_HEADS), full(1, 2 * W_HEADS), full(1, GATE_COLS),
            full(1, W_HEADS), full(1, W_HEADS),
            pl.BlockSpec((lc, D_HEAD // 2), lambda b, c: (c, 0)),
            pl.BlockSpec((lc, D_HEAD // 2), lambda b, c: (c, 0)),
            full(N_HEADS, lc, lc), full(lc, GATE_COLS), full(SUBLANES, GATE_COLS),
            state4, state_n, state_m, state_conv, state4,
        ],
        out_specs=[
            pl.BlockSpec((rows, 2 * W_HEADS), lambda b, c: (b * nc + c, 0)),
            state4, state_n, state_m, state_conv, state4,
        ],
        out_shape=[
            jax.ShapeDtypeStruct((bsz * length, 2 * W_HEADS), BF16 if lc % BF16_ROWS == 0 else F32),
            jax.ShapeDtypeStruct((bsz, N_HEADS, D_HEAD, D_HEAD), F32),
            jax.ShapeDtypeStruct((bsz, N_HEADS, D_HEAD), F32),
            jax.ShapeDtypeStruct((bsz, 1, N_HEADS), F32),
            jax.ShapeDtypeStruct((bsz, CONV_W - 1, 2 * W_HEADS), F32),
            jax.ShapeDtypeStruct((bsz, N_HEADS, D_HEAD, D_HEAD), F32),
        ],
        scratch_shapes=[pltpu.VMEM((nb, lc + CONV_PAD, 2 * W_HEADS), F32)],
        compiler_params=_params("parallel", "arbitrary"),
        name="mix_ab",
    )(z, zg, conv_w, conv_b.reshape(1, -1), gate_bias, norm_m.reshape(1, -1), norm_r.reshape(1, -1),
      cos, sin, decay, rtab, sdec, c0, n0, m0.reshape(bsz, 1, N_HEADS), conv0, s0)
    h, c_new, n_new, m_new, conv_new, s_new = outs
    return h, conv_new, c_new, n_new, m_new.reshape(bsz, N_HEADS), s_new


def _s5_prep_kernel(lr_ref, li_ref, ldt_ref, bre_ref, bim_ref, expand_ref, are_ref, aim_ref, bbre_ref, bbim_ref):
    lr = lr_ref[...]
    li = li_ref[...]
    dt = jnp.exp(ldt_ref[...])
    mag = jnp.exp(lr * dt)
    a_re = mag * jnp.cos(li * dt)
    a_im = mag * jnp.sin(li * dt)
    denom = lr * lr + li * li
    g_re = ((a_re - 1.0) * lr + a_im * li) / denom
    g_im = (a_im * lr - (a_re - 1.0) * li) / denom
    are_ref[...] = a_re
    aim_ref[...] = a_im
    ge_re = jnp.dot(g_re, expand_ref[...], preferred_element_type=F32, precision=lax.Precision.HIGHEST)
    ge_im = jnp.dot(g_im, expand_ref[...], preferred_element_type=F32, precision=lax.Precision.HIGHEST)
    bre = bre_ref[...]
    bim = bim_ref[...]
    bbre_ref[...] = ge_re * bre - ge_im * bim
    bbim_ref[...] = ge_re * bim + ge_im * bre


def _s5_prep(lam_re, lam_im, log_dt, bt_re, bt_im):
    g, p = lam_re.shape
    expand = jnp.tile(jnp.eye(p, dtype=F32), (1, S5_GROUP))
    return pl.pallas_call(
        _s5_prep_kernel,
        out_shape=[
            jax.ShapeDtypeStruct((g, p), F32), jax.ShapeDtypeStruct((g, p), F32),
            jax.ShapeDtypeStruct((g, p * S5_GROUP), F32), jax.ShapeDtypeStruct((g, p * S5_GROUP), F32),
        ],
        compiler_params=pltpu.CompilerParams(vmem_limit_bytes=VMEM_LIMIT_BYTES),
        name="s5_prep",
    )(lam_re, lam_im, log_dt.reshape(g, 1), bt_re.reshape(g, p * S5_GROUP), bt_im.reshape(g, p * S5_GROUP), expand)


S5_COLS = GROUPS_PER_BLOCK * S5_GROUP
S5_HALF = GROUPS_PER_BLOCK * S5_STATES
S5_TILES = S5_HALF // LANES
S5_COL_TILES = S5_COLS // LANES
S5_TILE_COLS = 2 * LANES


def _tile_major(re, im):
    lead = re.shape[:-1]
    pair = jnp.stack([re.reshape(lead + (S5_TILES, LANES)), im.reshape(lead + (S5_TILES, LANES))], axis=-2)
    return pair.reshape(lead + (2 * S5_HALF,))


def _s5_scan_dense(bu_ref, xn_ref, a_ref, nb, tt):
    a_re = [jnp.broadcast_to(a_ref[0, :, (2 * j) * LANES:(2 * j + 1) * LANES], (SUBLANES, LANES))
            for j in range(S5_TILES)]
    a_im = [jnp.broadcast_to(a_ref[0, :, (2 * j + 1) * LANES:(2 * j + 2) * LANES], (SUBLANES, LANES))
            for j in range(S5_TILES)]

    def group(s, _):
        r0 = pl.multiple_of(s * SUBLANES, SUBLANES)
        for j in range(S5_TILES):
            re_cols = slice((2 * j) * LANES, (2 * j + 1) * LANES)
            im_cols = slice((2 * j + 1) * LANES, (2 * j + 2) * LANES)
            xr = xn_ref[pl.ds(r0, SUBLANES), re_cols]
            xi = xn_ref[pl.ds(r0, SUBLANES), im_cols]
            for t in range(tt):
                rows = pl.ds(t * nb + r0, SUBLANES)
                xr, xi = (a_re[j] * xr - a_im[j] * xi + bu_ref[2 * j, rows, :],
                          a_re[j] * xi + a_im[j] * xr + bu_ref[2 * j + 1, rows, :])
                bu_ref[2 * j, rows, :] = xr
                bu_ref[2 * j + 1, rows, :] = xi
            xn_ref[pl.ds(r0, SUBLANES), re_cols] = xr
            xn_ref[pl.ds(r0, SUBLANES), im_cols] = xi
        return 0

    lax.fori_loop(0, nb // SUBLANES, group, 0)


def _s5_scan_pairs(bu, state_re, state_im, a_re, a_im, nb):
    upper = lax.broadcasted_iota(jnp.int32, (SUBLANES, LANES), 0) >= nb
    ar = jnp.broadcast_to(a_re, (SUBLANES, LANES))
    ai = jnp.broadcast_to(a_im, (SUBLANES, LANES))
    c1_re = jnp.where(upper, ar * ar - ai * ai, ar)
    c1_im = jnp.where(upper, 2.0 * ar * ai, ai)
    c2_re = jnp.where(upper, ar, 0.0)
    c2_im = jnp.where(upper, ai, 0.0)

    def both_halves(x):
        return jnp.where(upper, x, pltpu.roll(x, nb, axis=0))

    pr = jnp.where(upper, pltpu.roll(state_re, nb, axis=0), state_re)
    pi = jnp.where(upper, pltpu.roll(state_im, nb, axis=0), state_im)
    out = []
    for i in range(bu.shape[0] // SUBLANES):
        vr = bu[i * SUBLANES:(i + 1) * SUBLANES, :LANES]
        vi = bu[i * SUBLANES:(i + 1) * SUBLANES, LANES:]
        rr = pltpu.roll(vr, nb, axis=0)
        ri = pltpu.roll(vi, nb, axis=0)
        yr = c1_re * pr - c1_im * pi + (c2_re * rr - c2_im * ri) + vr
        yi = c1_re * pi + c1_im * pr + (c2_re * ri + c2_im * rr) + vi
        out.append(jnp.concatenate([yr, yi], axis=-1))
        pr, pi = both_halves(yr), both_halves(yi)
    return jnp.concatenate(out, axis=0), pr, pi


def _s5_kernel(*refs, nb, tt, interleave):
    n_u = nb * S5_COL_TILES if interleave else 1
    u_refs = refs[:n_u]
    bb_ref, cc_ref, a_ref, d_ref, x0_ref, y_ref, xn_ref = refs[n_u:n_u + 7]

    @pl.when(pl.program_id(1) == 0)
    def _():
        xn_ref[...] = x0_ref[...]

    if interleave:
        assert 2 * nb == SUBLANES
        ut_ref, yt_ref = refs[n_u + 7:]
        for b in range(nb):
            for c in range(S5_COL_TILES):
                ut_ref[c, pl.ds(b, tt, stride=nb), :] = u_refs[b * S5_COL_TILES + c][...]
        u = jnp.concatenate([ut_ref[c] for c in range(S5_COL_TILES)], axis=-1)
        ub = u.astype(BF16)
        y = d_ref[...] * u
        for j in range(S5_TILES):
            cols = slice(j * S5_TILE_COLS, (j + 1) * S5_TILE_COLS)
            re_cols = slice(j * S5_TILE_COLS, j * S5_TILE_COLS + LANES)
            im_cols = slice(j * S5_TILE_COLS + LANES, (j + 1) * S5_TILE_COLS)
            bb_j = jnp.concatenate([bb_ref[0, 2 * j], bb_ref[0, 2 * j + 1]], axis=-1)
            cc_j = jnp.concatenate([cc_ref[0, 2 * j], cc_ref[0, 2 * j + 1]], axis=0)
            bu = jnp.dot(ub, bb_j, preferred_element_type=F32)
            x, fin_re, fin_im = _s5_scan_pairs(bu, xn_ref[:, re_cols], xn_ref[:, im_cols],
                                               a_ref[0, :, re_cols], a_ref[0, :, im_cols], nb)
            xn_ref[:, re_cols] = fin_re
            xn_ref[:, im_cols] = fin_im
            y = y + jnp.dot(x.astype(BF16), cc_j, preferred_element_type=F32)
    else:
        assert nb % SUBLANES == 0
        (bu_ref,) = refs[n_u + 7:]
        u = u_refs[0][...]
        bb_all = jnp.concatenate([bb_ref[0, k] for k in range(2 * S5_TILES)], axis=-1)
        cc_all = jnp.concatenate([cc_ref[0, k] for k in range(2 * S5_TILES)], axis=0)
        bu = jnp.dot(u.astype(BF16), bb_all, preferred_element_type=F32)
        for j in range(2 * S5_TILES):
            bu_ref[j] = bu[:, j * LANES:(j + 1) * LANES]
        _s5_scan_dense(bu_ref, xn_ref, a_ref, nb, tt)
        x = jnp.concatenate([bu_ref[j] for j in range(2 * S5_TILES)], axis=-1)
        y = jnp.dot(x.astype(BF16), cc_all, preferred_element_type=F32) + d_ref[...] * u
    y = jax.nn.gelu(y)
    if interleave:
        for c in range(S5_COL_TILES):
            yt_ref[c] = y[:, c * LANES:(c + 1) * LANES]
        for b in range(nb):
            for c in range(S5_COL_TILES):
                y_ref[b, :, c * LANES:(c + 1) * LANES] = yt_ref[c, pl.ds(b, tt, stride=nb), :].astype(BF16)
    else:
        y_ref[...] = y.astype(BF16)


def _mix_s5(u, x0_re, x0_im, bb_blk, cc_blk, a_blk, d_skip, *, bsz, length, tt, row0, interleave):
    w = u.shape[-1]
    nblk = w // S5_COLS
    nt = length // tt
    x0 = _tile_major(x0_re.reshape(bsz, nblk, S5_HALF), x0_im.reshape(bsz, nblk, S5_HALF))
    srows = max(bsz, SUBLANES)
    x0 = jnp.pad(x0.transpose(1, 0, 2), ((0, 0), (0, srows - bsz), (0, 0)))
    rows = bsz * tt
    if interleave:
        blk0 = row0 // tt
        u_specs = [pl.BlockSpec((tt, LANES), functools.partial(
            lambda g, t, b, c: (blk0 + b * nt + t, g * S5_COL_TILES + c), b=b, c=c))
            for b in range(bsz) for c in range(S5_COL_TILES)]
        u_args = [u] * (bsz * S5_COL_TILES)
        y_spec = pl.BlockSpec((bsz, tt, S5_COLS), lambda g, t: (0, t, g))
        y_shape = jax.ShapeDtypeStruct((bsz, length, w), BF16)
        scratch = [pltpu.VMEM((S5_COL_TILES, rows, LANES), F32), pltpu.VMEM((S5_COL_TILES, rows, LANES), F32)]
    else:
        assert nt == 1
        blk0 = row0 // rows
        u_specs = [pl.BlockSpec((rows, S5_COLS), lambda g, t: (blk0, g))]
        u_args = [u]
        y_spec = pl.BlockSpec((rows, S5_COLS), lambda g, t: (0, g))
        y_shape = jax.ShapeDtypeStruct((rows, w), BF16)
        scratch = [pltpu.VMEM((2 * S5_TILES, rows, LANES), F32)]
    y, xn = pl.pallas_call(
        functools.partial(_s5_kernel, nb=bsz, tt=tt, interleave=interleave),
        grid=(nblk, nt),
        in_specs=u_specs + [
            pl.BlockSpec((1, 2 * S5_TILES, S5_COLS, LANES), lambda g, t: (g, 0, 0, 0)),
            pl.BlockSpec((1, 2 * S5_TILES, LANES, S5_COLS), lambda g, t: (g, 0, 0, 0)),
            pl.BlockSpec((1, 1, 2 * S5_HALF), lambda g, t: (g, 0, 0)),
            pl.BlockSpec((1, S5_COLS), lambda g, t: (0, g)),
            pl.BlockSpec((None, srows, 2 * S5_HALF), lambda g, t: (g, 0, 0)),
        ],
        out_specs=[y_spec, pl.BlockSpec((None, srows, 2 * S5_HALF), lambda g, t: (g, 0, 0))],
        out_shape=[y_shape, jax.ShapeDtypeStruct((nblk, srows, 2 * S5_HALF), F32)],
        scratch_shapes=scratch,
        compiler_params=_params("parallel", "arbitrary"),
        name="mix_s5",
    )(*u_args, bb_blk, cc_blk, a_blk, d_skip.reshape(1, w), x0)
    xn = xn[:, :bsz].transpose(1, 0, 2).reshape(bsz, nblk, S5_TILES, 2, LANES)
    g_total = nblk * GROUPS_PER_BLOCK
    xr = xn[..., 0, :].reshape(bsz, g_total, S5_STATES)
    xi = xn[..., 1, :].reshape(bsz, g_total, S5_STATES)
    return y, xr, xi


def _block_bands(blk_re, blk_im):
    g = blk_re.shape[0]
    nblk = g // GROUPS_PER_BLOCK
    assert LANES == 2 * S5_STATES
    shape = (nblk, S5_TILES, 2, S5_GROUP, S5_STATES)
    pair = jnp.stack([blk_re.reshape(shape), blk_im.reshape(shape)], axis=2)
    zero = jnp.zeros(pair.shape[:3] + (S5_GROUP, S5_STATES), F32)
    band = jnp.concatenate([jnp.concatenate([pair[:, :, :, 0], zero], axis=-1),
                            jnp.concatenate([zero, pair[:, :, :, 1]], axis=-1)], axis=-2)
    full = jnp.einsum('ntirl,st->ntisrl', band, jnp.eye(S5_TILES, dtype=F32))
    return full.reshape(nblk, 2 * S5_TILES, S5_COLS, LANES).astype(BF16)


def _s5_block_weights(a_re, a_im, bb_re, bb_im, c_re, c_im):
    g, p = a_re.shape
    nblk = g // GROUPS_PER_BLOCK
    bb = _block_bands(bb_re.reshape(g, S5_GROUP, p), bb_im.reshape(g, S5_GROUP, p))
    cc = jnp.swapaxes(_block_bands(c_re, -c_im), 2, 3)
    a_blk = _tile_major(a_re.reshape(nblk, S5_HALF), a_im.reshape(nblk, S5_HALF)).reshape(nblk, 1, 2 * S5_HALF)
    return bb, cc, a_blk


def kernel(x_prompt, x_sample, state_mlstm_C, state_mlstm_n, state_mlstm_m, state_mlstm_conv, state_ret_S, state_s5_re, state_s5_im, norm_ffn1, norm_mix, norm_ffn2, norm_final, ffn1_w_gate, ffn1_w_up, ffn1_w_down, ffn2_w_gate, ffn2_w_up, ffn2_w_down, ab_w_in, mlstm_b_i, mlstm_b_f, mlstm_conv_w, mlstm_conv_b, mlstm_norm, ret_norm, ab_w_out, s5_w_in, s5_lambda_re, s5_lambda_im, s5_log_dt, s5_B_re, s5_B_im, s5_C_re, s5_C_im, s5_D, s5_w_glu, s5_b_glu):
    bp, lp, d = x_prompt.shape
    bs, ls, _ = x_sample.shape
    mp, ms = bp * lp, bs * ls
    depth = norm_ffn1.shape[0]
    assert mp % TOKEN_TILE == 0 and ms % TOKEN_TILE == 0

    ffn = functools.partial(_ffn, tm=TOKEN_TILE, tf=FFN_TILE)
    pos_p = jnp.arange(lp, dtype=F32)
    pos_s = PAST_LEN + jnp.arange(ls, dtype=F32)

    ys = (x_prompt.reshape(mp, d), x_sample.reshape(ms, d))
    out_mc, out_mn, out_mm, out_conv, out_rs, out_re, out_im = ([] for _ in range(7))
    for layer in range(depth):
        y = ffn(ys, norm_ffn1, ffn1_w_gate, ffn1_w_up, ffn1_w_down, layer)
        if layer % 2 == 0:
            e = layer // 2
            n_gate = 2 * N_HEADS
            wt = jnp.swapaxes(ab_w_in, 1, 2)[e]
            z, zg = _ab_proj(y, norm_mix, wt, layer, tm=AB_PROJ_ROWS, tn=PROJ_TILE)
            gate_bias = jnp.zeros((1, GATE_COLS), F32).at[0, :N_HEADS].set(mlstm_b_i[e]).at[
                0, N_HEADS:n_gate].set(mlstm_b_f[e])
            common = (mlstm_conv_w[e], mlstm_conv_b[e], gate_bias, mlstm_norm[e], ret_norm[e])
            hp, conv_p, c_p, n_p, m_p, s_p = _mix_ab(
                z, zg, pos_p, jnp.zeros((bp, CONV_W - 1, 2 * W_HEADS), F32),
                jnp.zeros((bp, N_HEADS, D_HEAD, D_HEAD), F32), jnp.zeros((bp, N_HEADS, D_HEAD), F32),
                jnp.full((bp, N_HEADS), M_INIT, F32), jnp.zeros((bp, N_HEADS, D_HEAD, D_HEAD), F32),
                *common, bsz=bp, length=lp, row0=0)
            hs, conv_s, c_s, n_s, m_s, s_s = _mix_ab(
                z, zg, pos_s, state_mlstm_conv[e], state_mlstm_C[e], state_mlstm_n[e], state_mlstm_m[e],
                state_ret_S[e], *common, bsz=bs, length=ls, row0=mp, nb=SAMPLE_BATCH_TILE)
            out_mc.append((c_p, c_s))
            out_mn.append((n_p, n_s))
            out_mm.append((m_p, m_s))
            out_conv.append((conv_p, conv_s))
            out_rs.append((s_p, s_s))
            y = _proj_res((hp, hs), ab_w_out[e], y, tm=TOKEN_TILE // 2)
        else:
            o = layer // 2
            u = _norm_proj(y, norm_mix, s5_w_in[o], layer, tm=TOKEN_TILE // 2)
            a_re, a_im, bb_re, bb_im = _s5_prep(s5_lambda_re[o], s5_lambda_im[o], s5_log_dt[o],
                                                jnp.swapaxes(s5_B_re[o], 1, 2), jnp.swapaxes(s5_B_im[o], 1, 2))
            bb, cc, a_blk = _s5_block_weights(a_re, a_im, bb_re, bb_im, s5_C_re[o], s5_C_im[o])
            zeros_p = jnp.zeros((bp,) + a_re.shape, F32)
            ya_p, re_p, im_p = _mix_s5(u, zeros_p, zeros_p, bb, cc, a_blk, s5_D[o], bsz=bp, length=lp,
                                       tt=S5_TIME_TILE, row0=0, interleave=True)
            u_s = u[mp:].reshape(bs, ls, -1).transpose(1, 0, 2).reshape(ms, -1)
            ya_s, re_s, im_s = _mix_s5(u_s, state_s5_re[o], state_s5_im[o], bb, cc, a_blk, s5_D[o], bsz=bs,
                                       length=ls, tt=ls, row0=0, interleave=False)
            ya_s = ya_s.reshape(ls, bs, -1).transpose(1, 0, 2).reshape(ms, -1)
            out_re.append((re_p, re_s))
            out_im.append((im_p, im_s))
            y = _glu((ya_p.reshape(mp, -1), ya_s), s5_w_glu, s5_b_glu, y, o, tm=TOKEN_TILE, tn=PROJ_TILE)
        ys = (ffn((y,), norm_ffn2, ffn2_w_gate, ffn2_w_up, ffn2_w_down, layer),)
    y = ys[0]
    y_p = _final_norm(y, norm_final, row0=0, m=mp, tm=TOKEN_TILE)
    y_s = _final_norm(y, norm_final, row0=mp, m=ms, tm=TOKEN_TILE)

    def both(pairs):
        return jnp.stack([p for p, _ in pairs]), jnp.stack([s for _, s in pairs])

    pc, sc = both(out_mc)
    pn, sn = both(out_mn)
    pm, sm = both(out_mm)
    pconv, sconv = both(out_conv)
    ps, ss = both(out_rs)
    pre, sre = both(out_re)
    pim, sim = both(out_im)
    return (y_p.reshape(bp, lp, d), y_s.reshape(bs, ls, d), pc, sc, pn, sn, pm, sm, pconv, sconv, ps, ss,
            pre, sre, pim, sim)
```

```python
import functools

import jax
import jax.numpy as jnp
from jax import lax
from jax.experimental import pallas as pl
from jax.experimental.pallas import tpu as pltpu

F32 = jnp.float32
BF16 = jnp.bfloat16

EPS = 1e-6
M_INIT = -1e30
CHUNK = 128
CONV_W = 4
ROPE_BASE = 10000.0
PAST_LEN = 16384
N_HEADS = 4
D_HEAD = 256
W_HEADS = N_HEADS * D_HEAD
S5_GROUP = 16
S5_STATES = 64
GROUPS_PER_BLOCK = 16
LANES = 128
SUBLANES = 8
BF16_ROWS = 16
GATE_COLS = LANES

VMEM_LIMIT_BYTES = 60 * 1024 * 1024

TOKEN_TILE = 1024
FFN_TILE = 256
PROJ_TILE = 512
AB_PROJ_ROWS = 1536
S5_TIME_TILE = 256
SAMPLE_BATCH_TILE = 4


def _params(*semantics):
    return pltpu.CompilerParams(dimension_semantics=semantics, vmem_limit_bytes=VMEM_LIMIT_BYTES)


def _single_buffered(shape, index_map):
    return pl.BlockSpec(shape, index_map, pipeline_mode=pl.Buffered(1))


def _rms_norm(x, g):
    return x * lax.rsqrt(jnp.mean(x * x, axis=-1, keepdims=True) + EPS) * g


def _two_group_specs(tm, width, n_a, single=False):
    mk = _single_buffered if single else pl.BlockSpec
    return [mk((tm, width), lambda i, *_: (jnp.minimum(i, n_a - 1), 0)),
            mk((tm, width), lambda i, *_: (jnp.maximum(i - n_a, 0), 0))]


def _for_row_group(i, n_a, a_ref, b_ref, fn):
    @pl.when(i < n_a)
    def _():
        fn(a_ref[...])

    @pl.when(i >= n_a)
    def _():
        fn(b_ref[...])


def _ffn_kernel(*refs, n_a):
    if n_a is None:
        x_ref, g_ref, wg_ref, wu_ref, wd_ref, o_ref, xn_ref = refs
    else:
        xa_ref, xb_ref, g_ref, wg_ref, wu_ref, wd_ref, o_ref, xn_ref = refs
    f = pl.program_id(1)

    def init(x):
        xn_ref[...] = _rms_norm(x, g_ref[...]).astype(BF16)
        o_ref[...] = x

    @pl.when(f == 0)
    def _():
        if n_a is None:
            init(x_ref[...])
        else:
            _for_row_group(pl.program_id(0), n_a, xa_ref, xb_ref, init)

    xn = xn_ref[...]
    a = jnp.dot(xn, wg_ref[...].astype(BF16), preferred_element_type=F32)
    b = jnp.dot(xn, wu_ref[...].astype(BF16), preferred_element_type=F32)
    h = (0.5 * (a * jax.nn.sigmoid(a)) * b).astype(BF16)
    o_ref[...] += jnp.dot(h, wd_ref[...].astype(BF16), preferred_element_type=F32)


def _ffn(xs, g, w_gate, w_up, w_down, layer, *, tm, tf):
    d = xs[0].shape[1]
    m = sum(x.shape[0] for x in xs)
    dff = w_gate.shape[2]
    if len(xs) == 1:
        n_a = None
        x_specs = [pl.BlockSpec((tm, d), lambda i, f: (i, 0))]
    else:
        n_a = xs[0].shape[0] // tm
        x_specs = _two_group_specs(tm, d, n_a, single=True)
    return pl.pallas_call(
        functools.partial(_ffn_kernel, n_a=n_a),
        grid=(m // tm, dff // tf),
        in_specs=x_specs + [
            pl.BlockSpec((None, 1, d), lambda i, f: (layer, 0, 0)),
            pl.BlockSpec((None, d, tf), lambda i, f: (layer, 0, f)),
            pl.BlockSpec((None, d, tf), lambda i, f: (layer, 0, f)),
            pl.BlockSpec((None, tf, d), lambda i, f: (layer, f, 0)),
        ],
        out_specs=pl.BlockSpec((tm, d), lambda i, f: (i, 0)),
        out_shape=jax.ShapeDtypeStruct((m, d), F32),
        scratch_shapes=[pltpu.VMEM((tm, d), BF16)],
        compiler_params=_params("parallel", "arbitrary"),
        name="ffn",
    )(*xs, g.reshape(g.shape[0], 1, d), w_gate, w_up, w_down)


def _ab_proj_kernel(x_ref, g_ref, wlo_ref, whi_ref, wgate_ref, z_ref, zg_ref, xn_ref, *, n_lo):
    j = pl.program_id(1)

    @pl.when(j == 0)
    def _():
        xn_ref[...] = _rms_norm(x_ref[...], g_ref[...]).astype(BF16)
        zg_ref[...] = _dot_nt(xn_ref[...], wgate_ref[...].astype(BF16))

    @pl.when(j < n_lo)
    def _():
        z_ref[...] = _dot_nt(xn_ref[...], wlo_ref[...].astype(BF16))

    @pl.when(j >= n_lo)
    def _():
        z_ref[...] = _dot_nt(xn_ref[...], whi_ref[...].astype(BF16))


def _ab_proj(x, g, wt, layer, *, tm, tn):
    m, d = x.shape
    n_lo = n_hi = 4 * W_HEADS // tn
    hi0 = 4 * W_HEADS + 2 * N_HEADS
    gate_blk = 4 * W_HEADS // GATE_COLS
    return pl.pallas_call(
        functools.partial(_ab_proj_kernel, n_lo=n_lo),
        grid=(m // tm, n_lo + n_hi),
        in_specs=[
            pl.BlockSpec((tm, d), lambda i, j: (i, 0)),
            pl.BlockSpec((None, 1, d), lambda i, j: (layer, 0, 0)),
            pl.BlockSpec((tn, d), lambda i, j: (jnp.minimum(j, n_lo - 1), 0)),
            pl.BlockSpec((pl.Element(tn), pl.Element(d)),
                         lambda i, j: (pl.multiple_of(hi0 + jnp.maximum(j - n_lo, 0) * tn, SUBLANES), 0)),
            pl.BlockSpec((GATE_COLS, d), lambda i, j: (gate_blk, 0)),
        ],
        out_specs=[
            pl.BlockSpec((tm, tn), lambda i, j: (i, j)),
            pl.BlockSpec((tm, GATE_COLS), lambda i, j: (i, 0)),
        ],
        out_shape=[
            jax.ShapeDtypeStruct((m, (n_lo + n_hi) * tn), F32),
            jax.ShapeDtypeStruct((m, GATE_COLS), F32),
        ],
        scratch_shapes=[pltpu.VMEM((tm, d), BF16)],
        compiler_params=_params("parallel", "arbitrary"),
        name="ab_proj",
    )(x, g.reshape(g.shape[0], 1, d), wt, wt, wt)


def _norm_proj_kernel(x_ref, g_ref, w_ref, o_ref, wb_ref):
    @pl.when(pl.program_id(0) == 0)
    def _():
        wb_ref[...] = w_ref[...].astype(BF16)

    xn = _rms_norm(x_ref[...], g_ref[...]).astype(BF16)
    o_ref[...] = jnp.dot(xn, wb_ref[...], preferred_element_type=F32)


def _norm_proj(x, g, w, layer, *, tm):
    m, d = x.shape
    n = w.shape[1]
    return pl.pallas_call(
        _norm_proj_kernel,
        grid=(m // tm,),
        in_specs=[
            pl.BlockSpec((tm, d), lambda i: (i, 0)),
            pl.BlockSpec((None, 1, d), lambda i: (layer, 0, 0)),
            _single_buffered((d, n), lambda i: (0, 0)),
        ],
        out_specs=pl.BlockSpec((tm, n), lambda i: (i, 0)),
        out_shape=jax.ShapeDtypeStruct((m, n), F32),
        scratch_shapes=[pltpu.VMEM((d, n), BF16)],
        compiler_params=_params("arbitrary"),
        name="norm_proj",
    )(x, g.reshape(g.shape[0], 1, d), w)


def _proj_res_kernel(aa_ref, ab_ref, w_ref, r_ref, o_ref, wb_ref, *, n_a):
    i = pl.program_id(0)

    @pl.when(i == 0)
    def _():
        wb_ref[...] = w_ref[...].astype(BF16)

    def run(a):
        o_ref[...] = r_ref[...] + jnp.dot(a.astype(BF16), wb_ref[...], preferred_element_type=F32)

    _for_row_group(i, n_a, aa_ref, ab_ref, run)


def _proj_res(a_pair, w, res, *, tm):
    k = a_pair[0].shape[1]
    m = res.shape[0]
    n = w.shape[1]
    n_a = a_pair[0].shape[0] // tm
    return pl.pallas_call(
        functools.partial(_proj_res_kernel, n_a=n_a),
        grid=(m // tm,),
        in_specs=_two_group_specs(tm, k, n_a) + [
            _single_buffered((k, n), lambda i: (0, 0)),
            pl.BlockSpec((tm, n), lambda i: (i, 0)),
        ],
        out_specs=pl.BlockSpec((tm, n), lambda i: (i, 0)),
        out_shape=jax.ShapeDtypeStruct((m, n), F32),
        scratch_shapes=[pltpu.VMEM((k, n), BF16)],
        compiler_params=_params("arbitrary"),
        name="proj_res",
    )(*a_pair, w, res)


def _glu_kernel(aa_ref, ab_ref, wv_ref, wg_ref, bv_ref, bg_ref, r_ref, o_ref, *, n_a):
    def run(a):
        v = jnp.dot(a, wv_ref[...].astype(BF16), preferred_element_type=F32) + bv_ref[...]
        t = jnp.dot(a, wg_ref[...].astype(BF16), preferred_element_type=F32) + bg_ref[...]
        o_ref[...] = r_ref[...] + v * jax.nn.sigmoid(t)

    _for_row_group(pl.program_id(0), n_a, aa_ref, ab_ref, run)


def _glu(a_pair, w, b, res, layer, *, tm, tn):
    k = a_pair[0].shape[1]
    m = res.shape[0]
    n = w.shape[2] // 2
    nj = n // tn
    n_a = a_pair[0].shape[0] // tm
    b3 = b.reshape(b.shape[0], 1, 2 * n)
    return pl.pallas_call(
        functools.partial(_glu_kernel, n_a=n_a),
        grid=(m // tm, nj),
        in_specs=_two_group_specs(tm, k, n_a) + [
            pl.BlockSpec((None, k, tn), lambda i, j: (layer, 0, j)),
            pl.BlockSpec((None, k, tn), lambda i, j: (layer, 0, j + nj)),
            pl.BlockSpec((None, 1, tn), lambda i, j: (layer, 0, j)),
            pl.BlockSpec((None, 1, tn), lambda i, j: (layer, 0, j + nj)),
            pl.BlockSpec((tm, tn), lambda i, j: (i, j)),
        ],
        out_specs=pl.BlockSpec((tm, tn), lambda i, j: (i, j)),
        out_shape=jax.ShapeDtypeStruct((m, n), F32),
        compiler_params=_params("parallel", "arbitrary"),
        name="glu",
    )(*a_pair, w, w, b3, b3, res)


def _final_norm_kernel(x_ref, g_ref, o_ref):
    o_ref[...] = _rms_norm(x_ref[...], g_ref[...])


def _final_norm(x, g, *, row0, m, tm):
    d = x.shape[1]
    blk0 = row0 // tm
    return pl.pallas_call(
        _final_norm_kernel,
        grid=(m // tm,),
        in_specs=[pl.BlockSpec((tm, d), lambda i: (blk0 + i, 0)), pl.BlockSpec((1, d), lambda i: (0, 0))],
        out_specs=pl.BlockSpec((tm, d), lambda i: (i, 0)),
        out_shape=jax.ShapeDtypeStruct((m, d), F32),
        compiler_params=_params("parallel"),
        name="final_norm",
    )(x, g.reshape(1, d))


def _head_layer_norm(x, g):
    xc = x - jnp.mean(x, axis=-1, keepdims=True)
    var = jnp.mean(xc * xc, axis=-1, keepdims=True)
    return xc * lax.rsqrt(var + EPS) * g


def _cumsum_rows(x, n):
    row = lax.broadcasted_iota(jnp.int32, x.shape, 0)
    k = 1
    while k < n:
        x = x + jnp.where(row >= k, pltpu.roll(x, k, axis=0), 0.0)
        k *= 2
    return x


def _dot_nt(a, b):
    return lax.dot_general(a, b, (((1,), (1,)), ((), ())), preferred_element_type=F32)


def _dot_tn(a, b):
    return lax.dot_general(a, b, (((0,), (0,)), ((), ())), preferred_element_type=F32)


CONV_PAD = SUBLANES


def _ab_kernel(z_ref, zg_ref, cw_ref, cb_ref, gb_ref, nm_ref, nr_ref, cos_ref, sin_ref, dec_ref, rtab_ref, sdec_ref,
               c0_ref, n0_ref, m0_ref, conv0_ref, s0_ref,
               h_ref, c_ref, n_ref, m_ref, conv_ref, s_ref,
               xp_ref, *, lc, nb):
    @pl.when(pl.program_id(1) == 0)
    def _():
        c_ref[...] = c0_ref[...]
        n_ref[...] = n0_ref[...]
        m_ref[...] = m0_ref[...]
        s_ref[...] = s0_ref[...]
        xp_ref[:, CONV_PAD - (CONV_W - 1):CONV_PAD, :] = conv0_ref[...]

    for bi in range(nb):
        _ab_sequence(bi, z_ref, zg_ref, cw_ref, cb_ref, gb_ref, nm_ref, nr_ref, cos_ref, sin_ref, dec_ref, rtab_ref,
                     sdec_ref, h_ref, c_ref, n_ref, m_ref, conv_ref, s_ref, xp_ref, lc=lc)


def _ab_sequence(bi, z_ref, zg_ref, cw_ref, cb_ref, gb_ref, nm_ref, nr_ref, cos_ref, sin_ref, dec_ref, rtab_ref,
                 sdec_ref, h_ref, c_ref, n_ref, m_ref, conv_ref, s_ref, xp_ref, *, lc):
    rows = slice(bi * lc, (bi + 1) * lc)
    pad = CONV_PAD

    xp_ref[bi, pad:pad + lc, :] = z_ref[rows, 0:2 * W_HEADS]
    qk = cb_ref[...]
    for j in range(CONV_W):
        qk = qk + xp_ref[bi, pad - (CONV_W - 1) + j:pad - (CONV_W - 1) + j + lc, :] * cw_ref[j:j + 1, :]
    new_buf = xp_ref[bi, pad + lc - (CONV_W - 1):pad + lc, :]
    xp_ref[bi, pad - (CONV_W - 1):pad, :] = new_buf
    conv_ref[bi] = new_buf
    qk = qk * jax.nn.sigmoid(qk)

    row = lax.broadcasted_iota(jnp.int32, (lc, lc), 0)
    col = lax.broadcasted_iota(jnp.int32, (lc, lc), 1)
    causal = row >= col
    eye = row == col
    short = 2 * N_HEADS * lc <= LANES

    def row_of(c):
        return jnp.sum(jnp.where(eye, c, 0.0), axis=0, keepdims=True)

    gates = zg_ref[rows, :] + gb_ref[...]
    lf = jax.nn.log_sigmoid(gates)
    if short:
        bcum = _cumsum_rows(lf, lc)
    else:
        bcum = jnp.dot(causal.astype(F32), lf, preferred_element_type=F32, precision=lax.Precision.HIGHEST)
    updates = []

    scale = D_HEAD ** -0.5
    cos = cos_ref[...]
    sin = sin_ref[...]

    def rope(x):
        x1 = x[:, :D_HEAD // 2]
        x2 = x[:, D_HEAD // 2:]
        return jnp.concatenate([x1 * cos - x2 * sin, x1 * sin + x2 * cos], axis=-1)

    for h in range(N_HEADS):
        lo, hi = h * D_HEAD, (h + 1) * D_HEAD
        q = qk[:, lo:hi]
        k = qk[:, W_HEADS + lo:W_HEADS + hi] * scale
        v = z_ref[rows,2 * W_HEADS + lo:2 * W_HEADS + hi]
        og = z_ref[rows,3 * W_HEADS + lo:3 * W_HEADS + hi]
        qb, kb, vb = q.astype(BF16), k.astype(BF16), v.astype(BF16)
        c_prev = c_ref[bi, h]
        n_prev = n_ref[bi, h:h + 1, :]
        m_prev = m_ref[bi, :, h:h + 1]
        ig_c = gates[:, h:h + 1]
        b_c = bcum[:, N_HEADS + h:N_HEADS + h + 1]
        dlog = jnp.where(causal, b_c - row_of(b_c) + row_of(ig_c), -jnp.inf)
        s_log = b_c + m_prev
        m_row = jnp.maximum(s_log, jnp.max(dlog, axis=-1, keepdims=True))
        w = jnp.exp(dlog - m_row) * _dot_nt(qb, kb)
        sc = jnp.exp(s_log - m_row)
        num = sc * jnp.dot(qb, c_prev.astype(BF16), preferred_element_type=F32) + jnp.dot(
            w.astype(BF16), vb, preferred_element_type=F32)
        den = sc * jnp.sum(q * n_prev, axis=-1, keepdims=True) + jnp.sum(w, axis=-1, keepdims=True)
        hm = num / jnp.maximum(jnp.abs(den), jnp.exp(-m_row))
        m_new = m_row[lc - 1:lc, :]
        b_last = b_c[lc - 1:lc, :]
        w_state = jnp.exp(b_last + m_prev - m_new)
        kw = k * jnp.exp(b_last - b_c + ig_c - m_new)
        updates.append((c_ref, h, w_state * c_prev, kw, v))
        n_ref[bi, h:h + 1, :] = w_state * n_prev + jnp.sum(kw, axis=0, keepdims=True)
        m_ref[bi, :, h:h + 1] = m_new
        hm = _head_layer_norm(hm, nm_ref[:, lo:hi]) * jax.nn.sigmoid(og)
        h_ref[rows,lo:hi] = hm.astype(h_ref.dtype)
    for h in range(N_HEADS):
        lo, hi = h * D_HEAD, (h + 1) * D_HEAD
        qr = rope(z_ref[rows,4 * W_HEADS + lo:4 * W_HEADS + hi])
        kr = rope(z_ref[rows,5 * W_HEADS + lo:5 * W_HEADS + hi]) * scale
        vr = z_ref[rows,6 * W_HEADS + lo:6 * W_HEADS + hi]
        gr = z_ref[rows,7 * W_HEADS + lo:7 * W_HEADS + hi]
        qrb, krb, vrb = qr.astype(BF16), kr.astype(BF16), vr.astype(BF16)
        s_prev = s_ref[bi, h]
        att = _dot_nt(qrb, krb) * dec_ref[h]
        inner = jnp.dot(att.astype(BF16), vrb, preferred_element_type=F32)
        cross = jnp.dot(qrb, s_prev.astype(BF16), preferred_element_type=F32) * rtab_ref[:, h:h + 1]
        krw = kr * rtab_ref[:, N_HEADS + h:N_HEADS + h + 1]
        updates.append((s_ref, h, sdec_ref[0:1, h:h + 1] * s_prev, krw, vr))
        yr = _head_layer_norm(inner + cross, nr_ref[:, lo:hi]) * (gr * jax.nn.sigmoid(gr))
        h_ref[rows,W_HEADS + lo:W_HEADS + hi] = yr.astype(h_ref.dtype)

    if not short:
        for ref, h, decayed, kwt, val in updates:
            ref[bi, h] = decayed + _dot_tn(kwt.astype(BF16), val.astype(BF16))
    else:
        pad_rows = LANES - len(updates) * lc
        keys = jnp.concatenate([u[3] for u in updates] + [jnp.zeros((pad_rows, D_HEAD), F32)], axis=0)
        keys_t = keys.T.astype(BF16)
        for idx, (ref, h, decayed, _, val) in enumerate(updates):
            band = jnp.concatenate(
                [jnp.zeros((idx * lc, D_HEAD), F32)] * (idx > 0) + [val]
                + [jnp.zeros((LANES - (idx + 1) * lc, D_HEAD), F32)], axis=0)
            ref[bi, h] = decayed + jnp.dot(keys_t, band.astype(BF16), preferred_element_type=F32)


def _retention_tables(lc):
    lg = jnp.log1p(-(2.0 ** (-5.0 - jnp.arange(N_HEADS, dtype=F32))))
    idx = jnp.arange(lc, dtype=F32)
    diff = idx[:, None] - idx[None, :]
    decay = jnp.where(diff >= 0, jnp.exp(jnp.maximum(diff, 0.0)[None] * lg[:, None, None]), 0.0)
    cross = jnp.exp((idx[:, None] + 1.0) * lg[None, :])
    w_k = jnp.exp((lc - 1.0 - idx)[:, None] * lg[None, :])
    rtab = jnp.zeros((lc, GATE_COLS), F32).at[:, :N_HEADS].set(cross).at[:, N_HEADS:2 * N_HEADS].set(w_k)
    sdec = jnp.zeros((SUBLANES, GATE_COLS), F32).at[0, :N_HEADS].set(jnp.exp(lc * lg))
    return decay, rtab, sdec


def _rope_tables(pos):
    half = D_HEAD // 2
    freqs = ROPE_BASE ** (-jnp.arange(half, dtype=F32) / half)
    ang = pos[:, None] * freqs[None, :]
    return jnp.cos(ang), jnp.sin(ang)


def _mix_ab(z, zg, pos, conv0, c0, n0, m0, s0, conv_w, conv_b, gate_bias, norm_m, norm_r, *, bsz, length, row0, nb=1):
    lc = min(CHUNK, length)
    nc = length // lc
    assert nb == 1 or nc == 1
    rows = nb * lc
    blk0 = row0 // rows
    zw = z.shape[1]
    cos, sin = _rope_tables(pos)
    decay, rtab, sdec = _retention_tables(lc)
    full = lambda *shape: pl.BlockSpec(shape, lambda b, c: (0,) * len(shape))
    state4 = pl.BlockSpec((nb, N_HEADS, D_HEAD, D_HEAD), lambda b, c: (b, 0, 0, 0))
    state_n = pl.BlockSpec((nb, N_HEADS, D_HEAD), lambda b, c: (b, 0, 0))
    state_m = pl.BlockSpec((nb, 1, N_HEADS), lambda b, c: (b, 0, 0))
    state_conv = pl.BlockSpec((nb, CONV_W - 1, 2 * W_HEADS), lambda b, c: (b, 0, 0))
    outs = pl.pallas_call(
        functools.partial(_ab_kernel, lc=lc, nb=nb),
        grid=(bsz // nb, nc),
        in_specs=[
            pl.BlockSpec((rows, zw), lambda b, c: (blk0 + b * nc + c, 0)),
            pl.BlockSpec((rows, GATE_COLS), lambda b, c: (blk0 + b * nc + c, 0)),
            full(CONV_W, 2 * W_HEADS), full(1, 2 * W_HEADS), full(1, GATE_COLS),
            full(1, W_HEADS), full(1, W_HEADS),
            pl.BlockSpec((lc, D_HEAD // 2), lambda b, c: (c, 0)),
            pl.BlockSpec((lc, D_HEAD // 2), lambda b, c: (c, 0)),
            full(N_HEADS, lc, lc), full(lc, GATE_COLS), full(SUBLANES, GATE_COLS),
            state4, state_n, state_m, state_conv, state4,
        ],
        out_specs=[
            pl.BlockSpec((rows, 2 * W_HEADS), lambda b, c: (b * nc + c, 0)),
            state4, state_n, state_m, state_conv, state4,
        ],
        out_shape=[
            jax.ShapeDtypeStruct((bsz * length, 2 * W_HEADS), BF16 if lc % BF16_ROWS == 0 else F32),
            jax.ShapeDtypeStruct((bsz, N_HEADS, D_HEAD, D_HEAD), F32),
            jax.ShapeDtypeStruct((bsz, N_HEADS, D_HEAD), F32),
            jax.ShapeDtypeStruct((bsz, 1, N_HEADS), F32),
            jax.ShapeDtypeStruct((bsz, CONV_W - 1, 2 * W_HEADS), F32),
            jax.ShapeDtypeStruct((bsz, N_HEADS, D_HEAD, D_HEAD), F32),
        ],
        scratch_shapes=[pltpu.VMEM((nb, lc + CONV_PAD, 2 * W_HEADS), F32)],
        compiler_params=_params("parallel", "arbitrary"),
        name="mix_ab",
    )(z, zg, conv_w, conv_b.reshape(1, -1), gate_bias, norm_m.reshape(1, -1), norm_r.reshape(1, -1),
      cos, sin, decay, rtab, sdec, c0, n0, m0.reshape(bsz, 1, N_HEADS), conv0, s0)
    h, c_new, n_new, m_new, conv_new, s_new = outs
    return h, conv_new, c_new, n_new, m_new.reshape(bsz, N_HEADS), s_new


def _s5_prep_kernel(lr_ref, li_ref, ldt_ref, bre_ref, bim_ref, expand_ref, are_ref, aim_ref, bbre_ref, bbim_ref):
    lr = lr_ref[...]
    li = li_ref[...]
    dt = jnp.exp(ldt_ref[...])
    mag = jnp.exp(lr * dt)
    a_re = mag * jnp.cos(li * dt)
    a_im = mag * jnp.sin(li * dt)
    denom = lr * lr + li * li
    g_re = ((a_re - 1.0) * lr + a_im * li) / denom
    g_im = (a_im * lr - (a_re - 1.0) * li) / denom
    are_ref[...] = a_re
    aim_ref[...] = a_im
    ge_re = jnp.dot(g_re, expand_ref[...], preferred_element_type=F32, precision=lax.Precision.HIGHEST)
    ge_im = jnp.dot(g_im, expand_ref[...], preferred_element_type=F32, precision=lax.Precision.HIGHEST)
    bre = bre_ref[...]
    bim = bim_ref[...]
    bbre_ref[...] = ge_re * bre - ge_im * bim
    bbim_ref[...] = ge_re * bim + ge_im * bre


def _s5_prep(lam_re, lam_im, log_dt, bt_re, bt_im):
    g, p = lam_re.shape
    expand = jnp.tile(jnp.eye(p, dtype=F32), (1, S5_GROUP))
    return pl.pallas_call(
        _s5_prep_kernel,
        out_shape=[
            jax.ShapeDtypeStruct((g, p), F32), jax.ShapeDtypeStruct((g, p), F32),
            jax.ShapeDtypeStruct((g, p * S5_GROUP), F32), jax.ShapeDtypeStruct((g, p * S5_GROUP), F32),
        ],
        compiler_params=pltpu.CompilerParams(vmem_limit_bytes=VMEM_LIMIT_BYTES),
        name="s5_prep",
    )(lam_re, lam_im, log_dt.reshape(g, 1), bt_re.reshape(g, p * S5_GROUP), bt_im.reshape(g, p * S5_GROUP), expand)


S5_COLS = GROUPS_PER_BLOCK * S5_GROUP
S5_HALF = GROUPS_PER_BLOCK * S5_STATES
S5_TILES = S5_HALF // LANES
S5_COL_TILES = S5_COLS // LANES
S5_TILE_COLS = 2 * LANES


def _tile_major(re, im):
    lead = re.shape[:-1]
    pair = jnp.stack([re.reshape(lead + (S5_TILES, LANES)), im.reshape(lead + (S5_TILES, LANES))], axis=-2)
    return pair.reshape(lead + (2 * S5_HALF,))


def _s5_scan_dense(bu_ref, xn_ref, a_ref, nb, tt):
    a_re = [jnp.broadcast_to(a_ref[0, :, (2 * j) * LANES:(2 * j + 1) * LANES], (SUBLANES, LANES))
            for j in range(S5_TILES)]
    a_im = [jnp.broadcast_to(a_ref[0, :, (2 * j + 1) * LANES:(2 * j + 2) * LANES], (SUBLANES, LANES))
            for j in range(S5_TILES)]

    def group(s, _):
        r0 = pl.multiple_of(s * SUBLANES, SUBLANES)
        for j in range(S5_TILES):
            re_cols = slice((2 * j) * LANES, (2 * j + 1) * LANES)
            im_cols = slice((2 * j + 1) * LANES, (2 * j + 2) * LANES)
            xr = xn_ref[pl.ds(r0, SUBLANES), re_cols]
            xi = xn_ref[pl.ds(r0, SUBLANES), im_cols]
            for t in range(tt):
                rows = pl.ds(t * nb + r0, SUBLANES)
                xr, xi = (a_re[j] * xr - a_im[j] * xi + bu_ref[2 * j, rows, :],
                          a_re[j] * xi + a_im[j] * xr + bu_ref[2 * j + 1, rows, :])
                bu_ref[2 * j, rows, :] = xr
                bu_ref[2 * j + 1, rows, :] = xi
            xn_ref[pl.ds(r0, SUBLANES), re_cols] = xr
            xn_ref[pl.ds(r0, SUBLANES), im_cols] = xi
        return 0

    lax.fori_loop(0, nb // SUBLANES, group, 0)


def _s5_scan_pairs(bu, state_re, state_im, a_re, a_im, nb):
    upper = lax.broadcasted_iota(jnp.int32, (SUBLANES, LANES), 0) >= nb
    ar = jnp.broadcast_to(a_re, (SUBLANES, LANES))
    ai = jnp.broadcast_to(a_im, (SUBLANES, LANES))
    c1_re = jnp.where(upper, ar * ar - ai * ai, ar)
    c1_im = jnp.where(upper, 2.0 * ar * ai, ai)
    c2_re = jnp.where(upper, ar, 0.0)
    c2_im = jnp.where(upper, ai, 0.0)

    def both_halves(x):
        return jnp.where(upper, x, pltpu.roll(x, nb, axis=0))

    pr = jnp.where(upper, pltpu.roll(state_re, nb, axis=0), state_re)
    pi = jnp.where(upper, pltpu.roll(state_im, nb, axis=0), state_im)
    out = []
    for i in range(bu.shape[0] // SUBLANES):
        vr = bu[i * SUBLANES:(i + 1) * SUBLANES, :LANES]
        vi = bu[i * SUBLANES:(i + 1) * SUBLANES, LANES:]
        rr = pltpu.roll(vr, nb, axis=0)
        ri = pltpu.roll(vi, nb, axis=0)
        yr = c1_re * pr - c1_im * pi + (c2_re * rr - c2_im * ri) + vr
        yi = c1_re * pi + c1_im * pr + (c2_re * ri + c2_im * rr) + vi
        out.append(jnp.concatenate([yr, yi], axis=-1))
        pr, pi = both_halves(yr), both_halves(yi)
    return jnp.concatenate(out, axis=0), pr, pi


def _s5_kernel(*refs, nb, tt, interleave):
    n_u = nb * S5_COL_TILES if interleave else 1
    u_refs = refs[:n_u]
    bb_ref, cc_ref, a_ref, d_ref, x0_ref, y_ref, xn_ref = refs[n_u:n_u + 7]

    @pl.when(pl.program_id(1) == 0)
    def _():
        xn_ref[...] = x0_ref[...]

    if interleave:
        assert 2 * nb == SUBLANES
        ut_ref, yt_ref = refs[n_u + 7:]
        for b in range(nb):
            for c in range(S5_COL_TILES):
                ut_ref[c, pl.ds(b, tt, stride=nb), :] = u_refs[b * S5_COL_TILES + c][...]
        u = jnp.concatenate([ut_ref[c] for c in range(S5_COL_TILES)], axis=-1)
        ub = u.astype(BF16)
        y = d_ref[...] * u
        for j in range(S5_TILES):
            re_cols = slice(j * S5_TILE_COLS, j * S5_TILE_COLS + LANES)
            im_cols = slice(j * S5_TILE_COLS + LANES, (j + 1) * S5_TILE_COLS)
            bb_j = jnp.concatenate([bb_ref[0, 2 * j], bb_ref[0, 2 * j + 1]], axis=-1)
            cc_j = jnp.concatenate([cc_ref[0, 2 * j], cc_ref[0, 2 * j + 1]], axis=0)
            bu = jnp.dot(ub, bb_j, preferred_element_type=F32)
            x, fin_re, fin_im = _s5_scan_pairs(bu, xn_ref[:, re_cols], xn_ref[:, im_cols],
                                               a_ref[0, :, re_cols], a_ref[0, :, im_cols], nb)
            xn_ref[:, re_cols] = fin_re
            xn_ref[:, im_cols] = fin_im
            y = y + jnp.dot(x.astype(BF16), cc_j, preferred_element_type=F32)
    else:
        assert nb % SUBLANES == 0
        (bu_ref,) = refs[n_u + 7:]
        u = u_refs[0][...]
        bb_all = jnp.concatenate([bb_ref[0, k] for k in range(2 * S5_TILES)], axis=-1)
        cc_all = jnp.concatenate([cc_ref[0, k] for k in range(2 * S5_TILES)], axis=0)
        bu = jnp.dot(u.astype(BF16), bb_all, preferred_element_type=F32)
        for j in range(2 * S5_TILES):
            bu_ref[j] = bu[:, j * LANES:(j + 1) * LANES]
        _s5_scan_dense(bu_ref, xn_ref, a_ref, nb, tt)
        x = jnp.concatenate([bu_ref[j] for j in range(2 * S5_TILES)], axis=-1)
        y = jnp.dot(x.astype(BF16), cc_all, preferred_element_type=F32) + d_ref[...] * u
    y = jax.nn.gelu(y)
    if interleave:
        for c in range(S5_COL_TILES):
            yt_ref[c] = y[:, c * LANES:(c + 1) * LANES]
        for b in range(nb):
            for c in range(S5_COL_TILES):
                y_ref[b, :, c * LANES:(c + 1) * LANES] = yt_ref[c, pl.ds(b, tt, stride=nb), :].astype(BF16)
    else:
        y_ref[...] = y.astype(BF16)


def _mix_s5(u, x0_re, x0_im, bb_blk, cc_blk, a_blk, d_skip, *, bsz, length, tt, row0, interleave):
    w = u.shape[-1]
    nblk = w // S5_COLS
    nt = length // tt
    x0 = _tile_major(x0_re.reshape(bsz, nblk, S5_HALF), x0_im.reshape(bsz, nblk, S5_HALF))
    srows = max(bsz, SUBLANES)
    x0 = jnp.pad(x0.transpose(1, 0, 2), ((0, 0), (0, srows - bsz), (0, 0)))
    rows = bsz * tt
    if interleave:
        blk0 = row0 // tt
        u_specs = [pl.BlockSpec((tt, LANES), functools.partial(
            lambda g, t, b, c: (blk0 + b * nt + t, g * S5_COL_TILES + c), b=b, c=c))
            for b in range(bsz) for c in range(S5_COL_TILES)]
        u_args = [u] * (bsz * S5_COL_TILES)
        y_spec = pl.BlockSpec((bsz, tt, S5_COLS), lambda g, t: (0, t, g))
        y_shape = jax.ShapeDtypeStruct((bsz, length, w), BF16)
        scratch = [pltpu.VMEM((S5_COL_TILES, rows, LANES), F32), pltpu.VMEM((S5_COL_TILES, rows, LANES), F32)]
    else:
        assert nt == 1
        blk0 = row0 // rows
        u_specs = [pl.BlockSpec((rows, S5_COLS), lambda g, t: (blk0, g))]
        u_args = [u]
        y_spec = pl.BlockSpec((rows, S5_COLS), lambda g, t: (0, g))
        y_shape = jax.ShapeDtypeStruct((rows, w), BF16)
        scratch = [pltpu.VMEM((2 * S5_TILES, rows, LANES), F32)]
    y, xn = pl.pallas_call(
        functools.partial(_s5_kernel, nb=bsz, tt=tt, interleave=interleave),
        grid=(nblk, nt),
        in_specs=u_specs + [
            pl.BlockSpec((1, 2 * S5_TILES, S5_COLS, LANES), lambda g, t: (g, 0, 0, 0)),
            pl.BlockSpec((1, 2 * S5_TILES, LANES, S5_COLS), lambda g, t: (g, 0, 0, 0)),
            pl.BlockSpec((1, 1, 2 * S5_HALF), lambda g, t: (g, 0, 0)),
            pl.BlockSpec((1, S5_COLS), lambda g, t: (0, g)),
            pl.BlockSpec((None, srows, 2 * S5_HALF), lambda g, t: (g, 0, 0)),
        ],
        out_specs=[y_spec, pl.BlockSpec((None, srows, 2 * S5_HALF), lambda g, t: (g, 0, 0))],
        out_shape=[y_shape, jax.ShapeDtypeStruct((nblk, srows, 2 * S5_HALF), F32)],
        scratch_shapes=scratch,
        compiler_params=_params("parallel", "arbitrary"),
        name="mix_s5",
    )(*u_args, bb_blk, cc_blk, a_blk, d_skip.reshape(1, w), x0)
    xn = xn[:, :bsz].transpose(1, 0, 2).reshape(bsz, nblk, S5_TILES, 2, LANES)
    g_total = nblk * GROUPS_PER_BLOCK
    xr = xn[..., 0, :].reshape(bsz, g_total, S5_STATES)
    xi = xn[..., 1, :].reshape(bsz, g_total, S5_STATES)
    return y, xr, xi


def _block_bands(blk_re, blk_im):
    g = blk_re.shape[0]
    nblk = g // GROUPS_PER_BLOCK
    assert LANES == 2 * S5_STATES
    shape = (nblk, S5_TILES, 2, S5_GROUP, S5_STATES)
    pair = jnp.stack([blk_re.reshape(shape), blk_im.reshape(shape)], axis=2)
    zero = jnp.zeros(pair.shape[:3] + (S5_GROUP, S5_STATES), F32)
    band = jnp.concatenate([jnp.concatenate([pair[:, :, :, 0], zero], axis=-1),
                            jnp.concatenate([zero, pair[:, :, :, 1]], axis=-1)], axis=-2)
    full = jnp.einsum('ntirl,st->ntisrl', band, jnp.eye(S5_TILES, dtype=F32))
    return full.reshape(nblk, 2 * S5_TILES, S5_COLS, LANES).astype(BF16)


def _s5_block_weights(a_re, a_im, bb_re, bb_im, c_re, c_im):
    g, p = a_re.shape
    nblk = g // GROUPS_PER_BLOCK
    bb = _block_bands(bb_re.reshape(g, S5_GROUP, p), bb_im.reshape(g, S5_GROUP, p))
    cc = jnp.swapaxes(_block_bands(c_re, -c_im), 2, 3)
    a_blk = _tile_major(a_re.reshape(nblk, S5_HALF), a_im.reshape(nblk, S5_HALF)).reshape(nblk, 1, 2 * S5_HALF)
    return bb, cc, a_blk


def kernel(x_prompt, x_sample, state_mlstm_C, state_mlstm_n, state_mlstm_m, state_mlstm_conv, state_ret_S, state_s5_re, state_s5_im, norm_ffn1, norm_mix, norm_ffn2, norm_final, ffn1_w_gate, ffn1_w_up, ffn1_w_down, ffn2_w_gate, ffn2_w_up, ffn2_w_down, ab_w_in, mlstm_b_i, mlstm_b_f, mlstm_conv_w, mlstm_conv_b, mlstm_norm, ret_norm, ab_w_out, s5_w_in, s5_lambda_re, s5_lambda_im, s5_log_dt, s5_B_re, s5_B_im, s5_C_re, s5_C_im, s5_D, s5_w_glu, s5_b_glu):
    bp, lp, d = x_prompt.shape
    bs, ls, _ = x_sample.shape
    mp, ms = bp * lp, bs * ls
    depth = norm_ffn1.shape[0]
    assert mp % TOKEN_TILE == 0 and ms % TOKEN_TILE == 0

    ffn = functools.partial(_ffn, tm=TOKEN_TILE, tf=FFN_TILE)
    pos_p = jnp.arange(lp, dtype=F32)
    pos_s = PAST_LEN + jnp.arange(ls, dtype=F32)

    ys = (x_prompt.reshape(mp, d), x_sample.reshape(ms, d))
    out_mc, out_mn, out_mm, out_conv, out_rs, out_re, out_im = ([] for _ in range(7))
    for layer in range(depth):
        y = ffn(ys, norm_ffn1, ffn1_w_gate, ffn1_w_up, ffn1_w_down, layer)
        if layer % 2 == 0:
            e = layer // 2
            n_gate = 2 * N_HEADS
            wt = jnp.swapaxes(ab_w_in, 1, 2)[e]
            z, zg = _ab_proj(y, norm_mix, wt, layer, tm=AB_PROJ_ROWS, tn=PROJ_TILE)
            gate_bias = jnp.zeros((1, GATE_COLS), F32).at[0, :N_HEADS].set(mlstm_b_i[e]).at[
                0, N_HEADS:n_gate].set(mlstm_b_f[e])
            common = (mlstm_conv_w[e], mlstm_conv_b[e], gate_bias, mlstm_norm[e], ret_norm[e])
            hp, conv_p, c_p, n_p, m_p, s_p = _mix_ab(
                z, zg, pos_p, jnp.zeros((bp, CONV_W - 1, 2 * W_HEADS), F32),
                jnp.zeros((bp, N_HEADS, D_HEAD, D_HEAD), F32), jnp.zeros((bp, N_HEADS, D_HEAD), F32),
                jnp.full((bp, N_HEADS), M_INIT, F32), jnp.zeros((bp, N_HEADS, D_HEAD, D_HEAD), F32),
                *common, bsz=bp, length=lp, row0=0)
            hs, conv_s, c_s, n_s, m_s, s_s = _mix_ab(
                z, zg, pos_s, state_mlstm_conv[e], state_mlstm_C[e], state_mlstm_n[e], state_mlstm_m[e],
                state_ret_S[e], *common, bsz=bs, length=ls, row0=mp, nb=SAMPLE_BATCH_TILE)
            out_mc.append((c_p, c_s))
            out_mn.append((n_p, n_s))
            out_mm.append((m_p, m_s))
            out_conv.append((conv_p, conv_s))
            out_rs.append((s_p, s_s))
            y = _proj_res((hp, hs), ab_w_out[e], y, tm=TOKEN_TILE // 2)
        else:
            o = layer // 2
            u = _norm_proj(y, norm_mix, s5_w_in[o], layer, tm=TOKEN_TILE // 2)
            a_re, a_im, bb_re, bb_im = _s5_prep(s5_lambda_re[o], s5_lambda_im[o], s5_log_dt[o],
                                                jnp.swapaxes(s5_B_re[o], 1, 2), jnp.swapaxes(s5_B_im[o], 1, 2))
            bb, cc, a_blk = _s5_block_weights(a_re, a_im, bb_re, bb_im, s5_C_re[o], s5_C_im[o])
            zeros_p = jnp.zeros((bp,) + a_re.shape, F32)
            ya_p, re_p, im_p = _mix_s5(u, zeros_p, zeros_p, bb, cc, a_blk, s5_D[o], bsz=bp, length=lp,
                                       tt=S5_TIME_TILE, row0=0, interleave=True)
            u_s = u[mp:].reshape(bs, ls, -1).transpose(1, 0, 2).reshape(ms, -1)
            ya_s, re_s, im_s = _mix_s5(u_s, state_s5_re[o], state_s5_im[o], bb, cc, a_blk, s5_D[o], bsz=bs,
                                       length=ls, tt=ls, row0=0, interleave=False)
            ya_s = ya_s.reshape(ls, bs, -1).transpose(1, 0, 2).reshape(ms, -1)
            out_re.append((re_p, re_s))
            out_im.append((im_p, im_s))
            y = _glu((ya_p.reshape(mp, -1), ya_s), s5_w_glu, s5_b_glu, y, o, tm=TOKEN_TILE, tn=PROJ_TILE)
        ys = (ffn((y,), norm_ffn2, ffn2_w_gate, ffn2_w_up, ffn2_w_down, layer),)
    y = ys[0]
    y_p = _final_norm(y, norm_final, row0=0, m=mp, tm=TOKEN_TILE)
    y_s = _final_norm(y, norm_final, row0=mp, m=ms, tm=TOKEN_TILE)

    def both(pairs):
        return jnp.stack([p for p, _ in pairs]), jnp.stack([s for _, s in pairs])

    pc, sc = both(out_mc)
    pn, sn = both(out_mn)
    pm, sm = both(out_mm)
    pconv, sconv = both(out_conv)
    ps, ss = both(out_rs)
    pre, sre = both(out_re)
    pim, sim = both(out_im)
    return (y_p.reshape(bp, lp, d), y_s.reshape(bs, ls, d), pc, sc, pn, sn, pm, sm, pconv, sconv, ps, ss,
            pre, sre, pim, sim)
```

```python
import functools

import jax
import jax.numpy as jnp
from jax import lax
from jax.experimental import pallas as pl
from jax.experimental.pallas import tpu as pltpu

F32 = jnp.float32
BF16 = jnp.bfloat16

EPS = 1e-6
M_INIT = -1e30
CHUNK = 128
CONV_W = 4
ROPE_BASE = 10000.0
PAST_LEN = 16384
N_HEADS = 4
D_HEAD = 256
W_HEADS = N_HEADS * D_HEAD
S5_GROUP = 16
S5_STATES = 64
GROUPS_PER_BLOCK = 16
LANES = 128
SUBLANES = 8
BF16_ROWS = 16
GATE_COLS = LANES

VMEM_LIMIT_BYTES = 60 * 1024 * 1024

TOKEN_TILE = 1024
FFN_TILE = 256
PROJ_TILE = 512
AB_PROJ_ROWS = 1536
S5_TIME_TILE = 256
SAMPLE_BATCH_TILE = 4


def _params(*semantics):
    return pltpu.CompilerParams(dimension_semantics=semantics, vmem_limit_bytes=VMEM_LIMIT_BYTES)


def _single_buffered(shape, index_map):
    return pl.BlockSpec(shape, index_map, pipeline_mode=pl.Buffered(1))


def _rms_norm(x, g):
    return x * lax.rsqrt(jnp.mean(x * x, axis=-1, keepdims=True) + EPS) * g


def _two_group_specs(tm, width, n_a, single=False):
    mk = _single_buffered if single else pl.BlockSpec
    return [mk((tm, width), lambda i, *_: (jnp.minimum(i, n_a - 1), 0)),
            mk((tm, width), lambda i, *_: (jnp.maximum(i - n_a, 0), 0))]


def _for_row_group(i, n_a, a_ref, b_ref, fn):
    @pl.when(i < n_a)
    def _():
        fn(a_ref[...])

    @pl.when(i >= n_a)
    def _():
        fn(b_ref[...])


def _ffn_kernel(*refs, n_a):
    if n_a is None:
        x_ref, g_ref, wg_ref, wu_ref, wd_ref, o_ref, xn_ref = refs
    else:
        xa_ref, xb_ref, g_ref, wg_ref, wu_ref, wd_ref, o_ref, xn_ref = refs
    f = pl.program_id(1)

    def init(x):
        xn_ref[...] = _rms_norm(x, g_ref[...]).astype(BF16)
        o_ref[...] = x

    @pl.when(f == 0)
    def _():
        if n_a is None:
            init(x_ref[...])
        else:
            _for_row_group(pl.program_id(0), n_a, xa_ref, xb_ref, init)

    xn = xn_ref[...]
    a = jnp.dot(xn, wg_ref[...].astype(BF16), preferred_element_type=F32)
    b = jnp.dot(xn, wu_ref[...].astype(BF16), preferred_element_type=F32)
    h = (0.5 * (a * jax.nn.sigmoid(a)) * b).astype(BF16)
    o_ref[...] += jnp.dot(h, wd_ref[...].astype(BF16), preferred_element_type=F32)


def _ffn(xs, g, w_gate, w_up, w_down, layer, *, tm, tf):
    d = xs[0].shape[1]
    m = sum(x.shape[0] for x in xs)
    dff = w_gate.shape[2]
    if len(xs) == 1:
        n_a = None
        x_specs = [pl.BlockSpec((tm, d), lambda i, f: (i, 0))]
    else:
        n_a = xs[0].shape[0] // tm
        x_specs = _two_group_specs(tm, d, n_a, single=True)
    return pl.pallas_call(
        functools.partial(_ffn_kernel, n_a=n_a),
        grid=(m // tm, dff // tf),
        in_specs=x_specs + [
            pl.BlockSpec((None, 1, d), lambda i, f: (layer, 0, 0)),
            pl.BlockSpec((None, d, tf), lambda i, f: (layer, 0, f)),
            pl.BlockSpec((None, d, tf), lambda i, f: (layer, 0, f)),
            pl.BlockSpec((None, tf, d), lambda i, f: (layer, f, 0)),
        ],
        out_specs=pl.BlockSpec((tm, d), lambda i, f: (i, 0)),
        out_shape=jax.ShapeDtypeStruct((m, d), F32),
        scratch_shapes=[pltpu.VMEM((tm, d), BF16)],
        compiler_params=_params("parallel", "arbitrary"),
        name="ffn",
    )(*xs, g.reshape(g.shape[0], 1, d), w_gate, w_up, w_down)


def _ab_proj_kernel(x_ref, g_ref, wlo_ref, whi_ref, wgate_ref, z_ref, zg_ref, xn_ref, *, n_lo):
    j = pl.program_id(1)

    @pl.when(j == 0)
    def _():
        xn_ref[...] = _rms_norm(x_ref[...], g_ref[...]).astype(BF16)
        zg_ref[...] = _dot_nt(xn_ref[...], wgate_ref[...].astype(BF16))

    @pl.when(j < n_lo)
    def _():
        z_ref[...] = _dot_nt(xn_ref[...], wlo_ref[...].astype(BF16))

    @pl.when(j >= n_lo)
    def _():
        z_ref[...] = _dot_nt(xn_ref[...], whi_ref[...].astype(BF16))


def _ab_proj(x, g, wt, layer, *, tm, tn):
    m, d = x.shape
    n_lo = n_hi = 4 * W_HEADS // tn
    hi0 = 4 * W_HEADS + 2 * N_HEADS
    gate_blk = 4 * W_HEADS // GATE_COLS
    return pl.pallas_call(
        functools.partial(_ab_proj_kernel, n_lo=n_lo),
        grid=(m // tm, n_lo + n_hi),
        in_specs=[
            pl.BlockSpec((tm, d), lambda i, j: (i, 0)),
            pl.BlockSpec((None, 1, d), lambda i, j: (layer, 0, 0)),
            pl.BlockSpec((tn, d), lambda i, j: (jnp.minimum(j, n_lo - 1), 0)),
            pl.BlockSpec((pl.Element(tn), pl.Element(d)),
                         lambda i, j: (pl.multiple_of(hi0 + jnp.maximum(j - n_lo, 0) * tn, SUBLANES), 0)),
            pl.BlockSpec((GATE_COLS, d), lambda i, j: (gate_blk, 0)),
        ],
        out_specs=[
            pl.BlockSpec((tm, tn), lambda i, j: (i, j)),
            pl.BlockSpec((tm, GATE_COLS), lambda i, j: (i, 0)),
        ],
        out_shape=[
            jax.ShapeDtypeStruct((m, (n_lo + n_hi) * tn), F32),
            jax.ShapeDtypeStruct((m, GATE_COLS), F32),
        ],
        scratch_shapes=[pltpu.VMEM((tm, d), BF16)],
        compiler_params=_params("parallel", "arbitrary"),
        name="ab_proj",
    )(x, g.reshape(g.shape[0], 1, d), wt, wt, wt)


def _norm_proj_kernel(x_ref, g_ref, w_ref, o_ref, wb_ref):
    @pl.when(pl.program_id(0) == 0)
    def _():
        wb_ref[...] = w_ref[...].astype(BF16)

    xn = _rms_norm(x_ref[...], g_ref[...]).astype(BF16)
    o_ref[...] = jnp.dot(xn, wb_ref[...], preferred_element_type=F32)


def _norm_proj(x, g, w, layer, *, tm):
    m, d = x.shape
    n = w.shape[1]
    return pl.pallas_call(
        _norm_proj_kernel,
        grid=(m // tm,),
        in_specs=[
            pl.BlockSpec((tm, d), lambda i: (i, 0)),
            pl.BlockSpec((None, 1, d), lambda i: (layer, 0, 0)),
            _single_buffered((d, n), lambda i: (0, 0)),
        ],
        out_specs=pl.BlockSpec((tm, n), lambda i: (i, 0)),
        out_shape=jax.ShapeDtypeStruct((m, n), F32),
        scratch_shapes=[pltpu.VMEM((d, n), BF16)],
        compiler_params=_params("arbitrary"),
        name="norm_proj",
    )(x, g.reshape(g.shape[0], 1, d), w)


def _proj_res_kernel(aa_ref, ab_ref, w_ref, r_ref, o_ref, wb_ref, *, n_a):
    i = pl.program_id(0)

    @pl.when(i == 0)
    def _():
        wb_ref[...] = w_ref[...].astype(BF16)

    def run(a):
        o_ref[...] = r_ref[...] + jnp.dot(a.astype(BF16), wb_ref[...], preferred_element_type=F32)

    _for_row_group(i, n_a, aa_ref, ab_ref, run)


def _proj_res(a_pair, w, res, *, tm):
    k = a_pair[0].shape[1]
    m = res.shape[0]
    n = w.shape[1]
    n_a = a_pair[0].shape[0] // tm
    return pl.pallas_call(
        functools.partial(_proj_res_kernel, n_a=n_a),
        grid=(m // tm,),
        in_specs=_two_group_specs(tm, k, n_a) + [
            _single_buffered((k, n), lambda i: (0, 0)),
            pl.BlockSpec((tm, n), lambda i: (i, 0)),
        ],
        out_specs=pl.BlockSpec((tm, n), lambda i: (i, 0)),
        out_shape=jax.ShapeDtypeStruct((m, n), F32),
        scratch_shapes=[pltpu.VMEM((k, n), BF16)],
        compiler_params=_params("arbitrary"),
        name="proj_res",
    )(*a_pair, w, res)


def _glu_kernel(aa_ref, ab_ref, wv_ref, wg_ref, bv_ref, bg_ref, r_ref, o_ref, *, n_a):
    def run(a):
        v = jnp.dot(a, wv_ref[...].astype(BF16), preferred_element_type=F32) + bv_ref[...]
        t = jnp.dot(a, wg_ref[...].astype(BF16), preferred_element_type=F32) + bg_ref[...]
        o_ref[...] = r_ref[...] + v * jax.nn.sigmoid(t)

    _for_row_group(pl.program_id(0), n_a, aa_ref, ab_ref, run)


def _glu(a_pair, w, b, res, layer, *, tm, tn):
    k = a_pair[0].shape[1]
    m = res.shape[0]
    n = w.shape[2] // 2
    nj = n // tn
    n_a = a_pair[0].shape[0] // tm
    b3 = b.reshape(b.shape[0], 1, 2 * n)
    return pl.pallas_call(
        functools.partial(_glu_kernel, n_a=n_a),
        grid=(m // tm, nj),
        in_specs=_two_group_specs(tm, k, n_a) + [
            pl.BlockSpec((None, k, tn), lambda i, j: (layer, 0, j)),
            pl.BlockSpec((None, k, tn), lambda i, j: (layer, 0, j + nj)),
            pl.BlockSpec((None, 1, tn), lambda i, j: (layer, 0, j)),
            pl.BlockSpec((None, 1, tn), lambda i, j: (layer, 0, j + nj)),
            pl.BlockSpec((tm, tn), lambda i, j: (i, j)),
        ],
        out_specs=pl.BlockSpec((tm, tn), lambda i, j: (i, j)),
        out_shape=jax.ShapeDtypeStruct((m, n), F32),
        compiler_params=_params("parallel", "arbitrary"),
        name="glu",
    )(*a_pair, w, w, b3, b3, res)


def _final_norm_kernel(x_ref, g_ref, o_ref):
    o_ref[...] = _rms_norm(x_ref[...], g_ref[...])


def _final_norm(x, g, *, row0, m, tm):
    d = x.shape[1]
    blk0 = row0 // tm
    return pl.pallas_call(
        _final_norm_kernel,
        grid=(m // tm,),
        in_specs=[pl.BlockSpec((tm, d), lambda i: (blk0 + i, 0)), pl.BlockSpec((1, d), lambda i: (0, 0))],
        out_specs=pl.BlockSpec((tm, d), lambda i: (i, 0)),
        out_shape=jax.ShapeDtypeStruct((m, d), F32),
        compiler_params=_params("parallel"),
        name="final_norm",
    )(x, g.reshape(1, d))


def _head_layer_norm(x, g):
    xc = x - jnp.mean(x, axis=-1, keepdims=True)
    var = jnp.mean(xc * xc, axis=-1, keepdims=True)
    return xc * lax.rsqrt(var + EPS) * g


def _cumsum_rows(x, n):
    row = lax.broadcasted_iota(jnp.int32, x.shape, 0)
    k = 1
    while k < n:
        x = x + jnp.where(row >= k, pltpu.roll(x, k, axis=0), 0.0)
        k *= 2
    return x


def _dot_nt(a, b):
    return lax.dot_general(a, b, (((1,), (1,)), ((), ())), preferred_element_type=F32)


def _dot_tn(a, b):
    return lax.dot_general(a, b, (((0,), (0,)), ((), ())), preferred_element_type=F32)


CONV_PAD = SUBLANES


def _ab_kernel(z_ref, zg_ref, cw_ref, cb_ref, gb_ref, nm_ref, nr_ref, cos_ref, sin_ref, dec_ref, rtab_ref, sdec_ref,
               c0_ref, n0_ref, m0_ref, conv0_ref, s0_ref,
               h_ref, c_ref, n_ref, m_ref, conv_ref, s_ref,
               xp_ref, *, lc, nb):
    @pl.when(pl.program_id(1) == 0)
    def _():
        c_ref[...] = c0_ref[...]
        n_ref[...] = n0_ref[...]
        m_ref[...] = m0_ref[...]
        s_ref[...] = s0_ref[...]
        xp_ref[:, CONV_PAD - (CONV_W - 1):CONV_PAD, :] = conv0_ref[...]

    for bi in range(nb):
        _ab_sequence(bi, z_ref, zg_ref, cw_ref, cb_ref, gb_ref, nm_ref, nr_ref, cos_ref, sin_ref, dec_ref, rtab_ref,
                     sdec_ref, h_ref, c_ref, n_ref, m_ref, conv_ref, s_ref, xp_ref, lc=lc)


def _ab_sequence(bi, z_ref, zg_ref, cw_ref, cb_ref, gb_ref, nm_ref, nr_ref, cos_ref, sin_ref, dec_ref, rtab_ref,
                 sdec_ref, h_ref, c_ref, n_ref, m_ref, conv_ref, s_ref, xp_ref, *, lc):
    rows = slice(bi * lc, (bi + 1) * lc)
    pad = CONV_PAD

    xp_ref[bi, pad:pad + lc, :] = z_ref[rows, 0:2 * W_HEADS]
    qk = cb_ref[...]
    for j in range(CONV_W):
        qk = qk + xp_ref[bi, pad - (CONV_W - 1) + j:pad - (CONV_W - 1) + j + lc, :] * cw_ref[j:j + 1, :]
    new_buf = xp_ref[bi, pad + lc - (CONV_W - 1):pad + lc, :]
    xp_ref[bi, pad - (CONV_W - 1):pad, :] = new_buf
    conv_ref[bi] = new_buf
    qk = qk * jax.nn.sigmoid(qk)

    row = lax.broadcasted_iota(jnp.int32, (lc, lc), 0)
    col = lax.broadcasted_iota(jnp.int32, (lc, lc), 1)
    causal = row >= col
    eye = row == col
    short = 2 * N_HEADS * lc <= LANES

    def row_of(c):
        return jnp.sum(jnp.where(eye, c, 0.0), axis=0, keepdims=True)

    gates = zg_ref[rows, :] + gb_ref[...]
    lf = jax.nn.log_sigmoid(gates)
    if short:
        bcum = _cumsum_rows(lf, lc)
    else:
        bcum = jnp.dot(causal.astype(F32), lf, preferred_element_type=F32, precision=lax.Precision.HIGHEST)
    updates = []

    scale = D_HEAD ** -0.5
    cos = cos_ref[...]
    sin = sin_ref[...]

    def rope(x):
        x1 = x[:, :D_HEAD // 2]
        x2 = x[:, D_HEAD // 2:]
        return jnp.concatenate([x1 * cos - x2 * sin, x1 * sin + x2 * cos], axis=-1)

    for h in range(N_HEADS):
        lo, hi = h * D_HEAD, (h + 1) * D_HEAD
        q = qk[:, lo:hi]
        k = qk[:, W_HEADS + lo:W_HEADS + hi] * scale
        v = z_ref[rows,2 * W_HEADS + lo:2 * W_HEADS + hi]
        og = z_ref[rows,3 * W_HEADS + lo:3 * W_HEADS + hi]
        qb, kb, vb = q.astype(BF16), k.astype(BF16), v.astype(BF16)
        c_prev = c_ref[bi, h]
        n_prev = n_ref[bi, h:h + 1, :]
        m_prev = m_ref[bi, :, h:h + 1]
        ig_c = gates[:, h:h + 1]
        b_c = bcum[:, N_HEADS + h:N_HEADS + h + 1]
        dlog = jnp.where(causal, b_c - row_of(b_c) + row_of(ig_c), -jnp.inf)
        s_log = b_c + m_prev
        m_row = jnp.maximum(s_log, jnp.max(dlog, axis=-1, keepdims=True))
        w = jnp.exp(dlog - m_row) * _dot_nt(qb, kb)
        sc = jnp.exp(s_log - m_row)
        num = sc * jnp.dot(qb, c_prev.astype(BF16), preferred_element_type=F32) + jnp.dot(
            w.astype(BF16), vb, preferred_element_type=F32)
        den = sc * jnp.sum(q * n_prev, axis=-1, keepdims=True) + jnp.sum(w, axis=-1, keepdims=True)
        hm = num / jnp.maximum(jnp.abs(den), jnp.exp(-m_row))
        m_new = m_row[lc - 1:lc, :]
        b_last = b_c[lc - 1:lc, :]
        w_state = jnp.exp(b_last + m_prev - m_new)
        kw = k * jnp.exp(b_last - b_c + ig_c - m_new)
        updates.append((c_ref, h, w_state * c_prev, kw, v))
        n_ref[bi, h:h + 1, :] = w_state * n_prev + jnp.sum(kw, axis=0, keepdims=True)
        m_ref[bi, :, h:h + 1] = m_new
        hm = _head_layer_norm(hm, nm_ref[:, lo:hi]) * jax.nn.sigmoid(og)
        h_ref[rows,lo:hi] = hm.astype(h_ref.dtype)
    for h in range(N_HEADS):
        lo, hi = h * D_HEAD, (h + 1) * D_HEAD
        qr = rope(z_ref[rows,4 * W_HEADS + lo:4 * W_HEADS + hi])
        kr = rope(z_ref[rows,5 * W_HEADS + lo:5 * W_HEADS + hi]) * scale
        vr = z_ref[rows,6 * W_HEADS + lo:6 * W_HEADS + hi]
        gr = z_ref[rows,7 * W_HEADS + lo:7 * W_HEADS + hi]
        qrb, krb, vrb = qr.astype(BF16), kr.astype(BF16), vr.astype(BF16)
        s_prev = s_ref[bi, h]
        att = _dot_nt(qrb, krb) * dec_ref[h]
        inner = jnp.dot(att.astype(BF16), vrb, preferred_element_type=F32)
        cross = jnp.dot(qrb, s_prev.astype(BF16), preferred_element_type=F32) * rtab_ref[:, h:h + 1]
        krw = kr * rtab_ref[:, N_HEADS + h:N_HEADS + h + 1]
        updates.append((s_ref, h, sdec_ref[0:1, h:h + 1] * s_prev, krw, vr))
        yr = _head_layer_norm(inner + cross, nr_ref[:, lo:hi]) * (gr * jax.nn.sigmoid(gr))
        h_ref[rows,W_HEADS + lo:W_HEADS + hi] = yr.astype(h_ref.dtype)

    if not short:
        for ref, h, decayed, kwt, val in updates:
            ref[bi, h] = decayed + _dot_tn(kwt.astype(BF16), val.astype(BF16))
    else:
        pad_rows = LANES - len(updates) * lc
        keys = jnp.concatenate([u[3] for u in updates] + [jnp.zeros((pad_rows, D_HEAD), F32)], axis=0)
        keys_t = keys.T.astype(BF16)
        for idx, (ref, h, decayed, _, val) in enumerate(updates):
            band = jnp.concatenate(
                [jnp.zeros((idx * lc, D_HEAD), F32)] * (idx > 0) + [val]
                + [jnp.zeros((LANES - (idx + 1) * lc, D_HEAD), F32)], axis=0)
            ref[bi, h] = decayed + jnp.dot(keys_t, band.astype(BF16), preferred_element_type=F32)


def _retention_tables(lc):
    lg = jnp.log1p(-(2.0 ** (-5.0 - jnp.arange(N_HEADS, dtype=F32))))
    idx = jnp.arange(lc, dtype=F32)
    diff = idx[:, None] - idx[None, :]
    decay = jnp.where(diff >= 0, jnp.exp(jnp.maximum(diff, 0.0)[None] * lg[:, None, None]), 0.0)
    cross = jnp.exp((idx[:, None] + 1.0) * lg[None, :])
    w_k = jnp.exp((lc - 1.0 - idx)[:, None] * lg[None, :])
    rtab = jnp.zeros((lc, GATE_COLS), F32).at[:, :N_HEADS].set(cross).at[:, N_HEADS:2 * N_HEADS].set(w_k)
    sdec = jnp.zeros((SUBLANES, GATE_COLS), F32).at[0, :N_HEADS].set(jnp.exp(lc * lg))
    return decay, rtab, sdec


def _rope_tables(pos):
    half = D_HEAD // 2
    freqs = ROPE_BASE ** (-jnp.arange(half, dtype=F32) / half)
    ang = pos[:, None] * freqs[None, :]
    return jnp.cos(ang), jnp.sin(ang)


def _mix_ab(z, zg, pos, conv0, c0, n0, m0, s0, conv_w, conv_b, gate_bias, norm_m, norm_r, *, bsz, length, row0, nb=1):
    lc = min(CHUNK, length)
    nc = length // lc
    assert nb == 1 or nc == 1
    rows = nb * lc
    blk0 = row0 // rows
    zw = z.shape[1]
    cos, sin = _rope_tables(pos)
    decay, rtab, sdec = _retention_tables(lc)
    full = lambda *shape: pl.BlockSpec(shape, lambda b, c: (0,) * len(shape))
    state4 = pl.BlockSpec((nb, N_HEADS, D_HEAD, D_HEAD), lambda b, c: (b, 0, 0, 0))
    state_n = pl.BlockSpec((nb, N_HEADS, D_HEAD), lambda b, c: (b, 0, 0))
    state_m = pl.BlockSpec((nb, 1, N_HEADS), lambda b, c: (b, 0, 0))
    state_conv = pl.BlockSpec((nb, CONV_W - 1, 2 * W_HEADS), lambda b, c: (b, 0, 0))
    outs = pl.pallas_call(
        functools.partial(_ab_kernel, lc=lc, nb=nb),
        grid=(bsz // nb, nc),
        in_specs=[
            pl.BlockSpec((rows, zw), lambda b, c: (blk0 + b * nc + c, 0)),
            pl.BlockSpec((rows, GATE_COLS), lambda b, c: (blk0 + b * nc + c, 0)),
            full(CONV_W, 2 * W_HEADS), full(1, 2 * W_HEADS), full(1, GATE_COLS),
            full(1, W_HEADS), full(1, W_HEADS),
            pl.BlockSpec((lc, D_HEAD // 2), lambda b, c: (c, 0)),
            pl.BlockSpec((lc, D_HEAD // 2), lambda b, c: (c, 0)),
            full(N_HEADS, lc, lc), full(lc, GATE_COLS), full(SUBLANES, GATE_COLS),
            state4, state_n, state_m, state_conv, state4,
        ],
        out_specs=[
            pl.BlockSpec((rows, 2 * W_HEADS), lambda b, c: (b * nc + c, 0)),
            state4, state_n, state_m, state_conv, state4,
        ],
        out_shape=[
            jax.ShapeDtypeStruct((bsz * length, 2 * W_HEADS), BF16 if lc % BF16_ROWS == 0 else F32),
            jax.ShapeDtypeStruct((bsz, N_HEADS, D_HEAD, D_HEAD), F32),
            jax.ShapeDtypeStruct((bsz, N_HEADS, D_HEAD), F32),
            jax.ShapeDtypeStruct((bsz, 1, N_HEADS), F32),
            jax.ShapeDtypeStruct((bsz, CONV_W - 1, 2 * W_HEADS), F32),
            jax.ShapeDtypeStruct((bsz, N_HEADS, D_HEAD, D_HEAD), F32),
        ],
        scratch_shapes=[pltpu.VMEM((nb, lc + CONV_PAD, 2 * W_HEADS), F32)],
        compiler_params=_params("parallel", "arbitrary"),
        name="mix_ab",
    )(z, zg, conv_w, conv_b.reshape(1, -1), gate_bias, norm_m.reshape(1, -1), norm_r.reshape(1, -1),
      cos, sin, decay, rtab, sdec, c0, n0, m0.reshape(bsz, 1, N_HEADS), conv0, s0)
    h, c_new, n_new, m_new, conv_new, s_new = outs
    return h, conv_new, c_new, n_new, m_new.reshape(bsz, N_HEADS), s_new


def _s5_prep_kernel(lr_ref, li_ref, ldt_ref, bre_ref, bim_ref, expand_ref, are_ref, aim_ref, bbre_ref, bbim_ref):
    lr = lr_ref[...]
    li = li_ref[...]
    dt = jnp.exp(ldt_ref[...])
    mag = jnp.exp(lr * dt)
    a_re = mag * jnp.cos(li * dt)
    a_im = mag * jnp.sin(li * dt)
    denom = lr * lr + li * li
    g_re = ((a_re - 1.0) * lr + a_im * li) / denom
    g_im = (a_im * lr - (a_re - 1.0) * li) / denom
    are_ref[...] = a_re
    aim_ref[...] = a_im
    ge_re = jnp.dot(g_re, expand_ref[...], preferred_element_type=F32, precision=lax.Precision.HIGHEST)
    ge_im = jnp.dot(g_im, expand_ref[...], preferred_element_type=F32, precision=lax.Precision.HIGHEST)
    bre = bre_ref[...]
    bim = bim_ref[...]
    bbre_ref[...] = ge_re * bre - ge_im * bim
    bbim_ref[...] = ge_re * bim + ge_im * bre


def _s5_prep(lam_re, lam_im, log_dt, bt_re, bt_im):
    g, p = lam_re.shape
    expand = jnp.tile(jnp.eye(p, dtype=F32), (1, S5_GROUP))
    return pl.pallas_call(
        _s5_prep_kernel,
        out_shape=[
            jax.ShapeDtypeStruct((g, p), F32), jax.ShapeDtypeStruct((g, p), F32),
            jax.ShapeDtypeStruct((g, p * S5_GROUP), F32), jax.ShapeDtypeStruct((g, p * S5_GROUP), F32),
        ],
        compiler_params=pltpu.CompilerParams(vmem_limit_bytes=VMEM_LIMIT_BYTES),
        name="s5_prep",
    )(lam_re, lam_im, log_dt.reshape(g, 1), bt_re.reshape(g, p * S5_GROUP), bt_im.reshape(g, p * S5_GROUP), expand)


S5_COLS = GROUPS_PER_BLOCK * S5_GROUP
S5_HALF = GROUPS_PER_BLOCK * S5_STATES
S5_TILES = S5_HALF // LANES
S5_COL_TILES = S5_COLS // LANES
S5_TILE_COLS = 2 * LANES


def _tile_major(re, im):
    lead = re.shape[:-1]
    pair = jnp.stack([re.reshape(lead + (S5_TILES, LANES)), im.reshape(lead + (S5_TILES, LANES))], axis=-2)
    return pair.reshape(lead + (2 * S5_HALF,))


def _s5_scan_dense(bu_ref, xn_ref, a_ref, nb, tt):
    a_re = [jnp.broadcast_to(a_ref[0, :, (2 * j) * LANES:(2 * j + 1) * LANES], (SUBLANES, LANES))
            for j in range(S5_TILES)]
    a_im = [jnp.broadcast_to(a_ref[0, :, (2 * j + 1) * LANES:(2 * j + 2) * LANES], (SUBLANES, LANES))
            for j in range(S5_TILES)]

    def group(s, _):
        r0 = pl.multiple_of(s * SUBLANES, SUBLANES)
        for j in range(S5_TILES):
            re_cols = slice((2 * j) * LANES, (2 * j + 1) * LANES)
            im_cols = slice((2 * j + 1) * LANES, (2 * j + 2) * LANES)
            xr = xn_ref[pl.ds(r0, SUBLANES), re_cols]
            xi = xn_ref[pl.ds(r0, SUBLANES), im_cols]
            for t in range(tt):
                rows = pl.ds(t * nb + r0, SUBLANES)
                xr, xi = (a_re[j] * xr - a_im[j] * xi + bu_ref[2 * j, rows, :],
                          a_re[j] * xi + a_im[j] * xr + bu_ref[2 * j + 1, rows, :])
                bu_ref[2 * j, rows, :] = xr
                bu_ref[2 * j + 1, rows, :] = xi
            xn_ref[pl.ds(r0, SUBLANES), re_cols] = xr
            xn_ref[pl.ds(r0, SUBLANES), im_cols] = xi
        return 0

    lax.fori_loop(0, nb // SUBLANES, group, 0)


def _s5_powers_kernel(a_ref, pw_ref, *, steps):
    row = lax.broadcasted_iota(jnp.int32, (SUBLANES, LANES), 0)
    for j in range(S5_TILES):
        re_cols = slice(j * S5_TILE_COLS, j * S5_TILE_COLS + LANES)
        im_cols = slice(j * S5_TILE_COLS + LANES, (j + 1) * S5_TILE_COLS)
        ar = a_ref[0, :, re_cols]
        ai = a_ref[0, :, im_cols]
        pr, pi = ar, ai
        tab_re = jnp.broadcast_to(pr, (SUBLANES, LANES))
        tab_im = jnp.broadcast_to(pi, (SUBLANES, LANES))
        for k in range(1, SUBLANES):
            pr, pi = pr * ar - pi * ai, pr * ai + pi * ar
            tab_re = jnp.where(row >= k, pr, tab_re)
            tab_im = jnp.where(row >= k, pi, tab_im)
        n = SUBLANES
        while n < steps:
            tab_re, tab_im = (jnp.concatenate([tab_re, tab_re * pr - tab_im * pi], axis=0),
                              jnp.concatenate([tab_im, tab_re * pi + tab_im * pr], axis=0))
            pr, pi = pr * pr - pi * pi, 2.0 * pr * pi
            n *= 2
        pw_ref[0, :, re_cols] = tab_re
        pw_ref[0, :, im_cols] = tab_im


def _s5_powers(a_blk, steps):
    nblk = a_blk.shape[0]
    return pl.pallas_call(
        functools.partial(_s5_powers_kernel, steps=steps),
        grid=(nblk,),
        in_specs=[pl.BlockSpec((1, 1, 2 * S5_HALF), lambda g: (g, 0, 0))],
        out_specs=pl.BlockSpec((1, steps, 2 * S5_HALF), lambda g: (g, 0, 0)),
        out_shape=jax.ShapeDtypeStruct((nblk, steps, 2 * S5_HALF), F32),
        compiler_params=_params("parallel"),
        name="s5_powers",
    )(a_blk)


def _s5_scan_halves(bu, state_re, state_im, a_re, a_im, pw_ref, re_cols, im_cols, nb):
    upper = lax.broadcasted_iota(jnp.int32, (SUBLANES, LANES), 0) >= nb
    ar = jnp.broadcast_to(a_re, (SUBLANES, LANES))
    ai = jnp.broadcast_to(a_im, (SUBLANES, LANES))
    pr = jnp.where(upper, 0.0, state_re)
    pi = jnp.where(upper, 0.0, state_im)
    groups = bu.shape[0] // SUBLANES
    part = []
    for i in range(groups):
        pr, pi = (ar * pr - ai * pi + bu[i * SUBLANES:(i + 1) * SUBLANES, :LANES],
                  ar * pi + ai * pr + bu[i * SUBLANES:(i + 1) * SUBLANES, LANES:])
        part.append((pr, pi))
    cr = jnp.where(upper, pltpu.roll(pr, nb, axis=0), 0.0)
    ci = jnp.where(upper, pltpu.roll(pi, nb, axis=0), 0.0)
    out = []
    for i in range(groups):
        wr = pw_ref[0, i:i + 1, re_cols]
        wi = pw_ref[0, i:i + 1, im_cols]
        yr = part[i][0] + (wr * cr - wi * ci)
        yi = part[i][1] + (wr * ci + wi * cr)
        out.append(jnp.concatenate([yr, yi], axis=-1))
    return jnp.concatenate(out, axis=0), pltpu.roll(yr, nb, axis=0), pltpu.roll(yi, nb, axis=0)


def _s5_kernel(*refs, nb, tt, interleave):
    n_u = nb * S5_COL_TILES if interleave else 1
    u_refs = refs[:n_u]
    bb_ref, cc_ref, a_ref, d_ref, x0_ref = refs[n_u:n_u + 5]
    n_in = n_u + (6 if interleave else 5)
    y_ref, xn_ref = refs[n_in:n_in + 2]

    @pl.when(pl.program_id(1) == 0)
    def _():
        xn_ref[...] = x0_ref[...]

    if interleave:
        assert 2 * nb == SUBLANES
        pw_ref = refs[n_u + 5]
        ut_ref, yt_ref = refs[n_in + 2:]
        half = tt // 2
        for b in range(nb):
            for c in range(S5_COL_TILES):
                for hf in range(2):
                    ut_ref[c, pl.ds(hf * nb + b, half, stride=SUBLANES), :] = (
                        u_refs[b * S5_COL_TILES + c][hf * half:(hf + 1) * half, :])
        u = jnp.concatenate([ut_ref[c] for c in range(S5_COL_TILES)], axis=-1)
        ub = u.astype(BF16)
        y = d_ref[...] * u
        for j in range(S5_TILES):
            re_cols = slice(j * S5_TILE_COLS, j * S5_TILE_COLS + LANES)
            im_cols = slice(j * S5_TILE_COLS + LANES, (j + 1) * S5_TILE_COLS)
            bb_j = jnp.concatenate([bb_ref[0, 2 * j], bb_ref[0, 2 * j + 1]], axis=-1)
            cc_j = jnp.concatenate([cc_ref[0, 2 * j], cc_ref[0, 2 * j + 1]], axis=0)
            bu = jnp.dot(ub, bb_j, preferred_element_type=F32)
            x, fin_re, fin_im = _s5_scan_halves(bu, xn_ref[:, re_cols], xn_ref[:, im_cols], a_ref[0, :, re_cols],
                                                a_ref[0, :, im_cols], pw_ref, re_cols, im_cols, nb)
            xn_ref[:, re_cols] = fin_re
            xn_ref[:, im_cols] = fin_im
            y = y + jnp.dot(x.astype(BF16), cc_j, preferred_element_type=F32)
    else:
        assert nb % SUBLANES == 0
        (bu_ref,) = refs[n_in + 2:]
        u = u_refs[0][...]
        bb_all = jnp.concatenate([bb_ref[0, k] for k in range(2 * S5_TILES)], axis=-1)
        cc_all = jnp.concatenate([cc_ref[0, k] for k in range(2 * S5_TILES)], axis=0)
        bu = jnp.dot(u.astype(BF16), bb_all, preferred_element_type=F32)
        for j in range(2 * S5_TILES):
            bu_ref[j] = bu[:, j * LANES:(j + 1) * LANES]
        _s5_scan_dense(bu_ref, xn_ref, a_ref, nb, tt)
        x = jnp.concatenate([bu_ref[j] for j in range(2 * S5_TILES)], axis=-1)
        y = jnp.dot(x.astype(BF16), cc_all, preferred_element_type=F32) + d_ref[...] * u
    y = jax.nn.gelu(y)
    if interleave:
        for c in range(S5_COL_TILES):
            yt_ref[c] = y[:, c * LANES:(c + 1) * LANES]
        for b in range(nb):
            for c in range(S5_COL_TILES):
                for hf in range(2):
                    y_ref[b, hf * half:(hf + 1) * half, c * LANES:(c + 1) * LANES] = (
                        yt_ref[c, pl.ds(hf * nb + b, half, stride=SUBLANES), :].astype(BF16))
    else:
        y_ref[...] = y.astype(BF16)


def _mix_s5(u, x0_re, x0_im, bb_blk, cc_blk, a_blk, d_skip, *, bsz, length, tt, row0, interleave):
    w = u.shape[-1]
    nblk = w // S5_COLS
    nt = length // tt
    x0 = _tile_major(x0_re.reshape(bsz, nblk, S5_HALF), x0_im.reshape(bsz, nblk, S5_HALF))
    srows = max(bsz, SUBLANES)
    x0 = jnp.pad(x0.transpose(1, 0, 2), ((0, 0), (0, srows - bsz), (0, 0)))
    rows = bsz * tt
    if interleave:
        blk0 = row0 // tt
        u_specs = [pl.BlockSpec((tt, LANES), functools.partial(
            lambda g, t, b, c: (blk0 + b * nt + t, g * S5_COL_TILES + c), b=b, c=c))
            for b in range(bsz) for c in range(S5_COL_TILES)]
        u_args = [u] * (bsz * S5_COL_TILES)
        y_spec = pl.BlockSpec((bsz, tt, S5_COLS), lambda g, t: (0, t, g))
        y_shape = jax.ShapeDtypeStruct((bsz, length, w), BF16)
        scratch = [pltpu.VMEM((S5_COL_TILES, rows, LANES), F32), pltpu.VMEM((S5_COL_TILES, rows, LANES), F32)]
        extra_args = [_s5_powers(a_blk, tt // 2)]
        extra_specs = [pl.BlockSpec((1, tt // 2, 2 * S5_HALF), lambda g, t: (g, 0, 0))]
    else:
        extra_args, extra_specs = [], []
        assert nt == 1
        blk0 = row0 // rows
        u_specs = [pl.BlockSpec((rows, S5_COLS), lambda g, t: (blk0, g))]
        u_args = [u]
        y_spec = pl.BlockSpec((rows, S5_COLS), lambda g, t: (0, g))
        y_shape = jax.ShapeDtypeStruct((rows, w), BF16)
        scratch = [pltpu.VMEM((2 * S5_TILES, rows, LANES), F32)]
    y, xn = pl.pallas_call(
        functools.partial(_s5_kernel, nb=bsz, tt=tt, interleave=interleave),
        grid=(nblk, nt),
        in_specs=u_specs + [
            pl.BlockSpec((1, 2 * S5_TILES, S5_COLS, LANES), lambda g, t: (g, 0, 0, 0)),
            pl.BlockSpec((1, 2 * S5_TILES, LANES, S5_COLS), lambda g, t: (g, 0, 0, 0)),
            pl.BlockSpec((1, 1, 2 * S5_HALF), lambda g, t: (g, 0, 0)),
            pl.BlockSpec((1, S5_COLS), lambda g, t: (0, g)),
            pl.BlockSpec((None, srows, 2 * S5_HALF), lambda g, t: (g, 0, 0)),
        ] + extra_specs,
        out_specs=[y_spec, pl.BlockSpec((None, srows, 2 * S5_HALF), lambda g, t: (g, 0, 0))],
        out_shape=[y_shape, jax.ShapeDtypeStruct((nblk, srows, 2 * S5_HALF), F32)],
        scratch_shapes=scratch,
        compiler_params=_params("parallel", "arbitrary"),
        name="mix_s5",
    )(*u_args, bb_blk, cc_blk, a_blk, d_skip.reshape(1, w), x0, *extra_args)
    xn = xn[:, :bsz].transpose(1, 0, 2).reshape(bsz, nblk, S5_TILES, 2, LANES)
    g_total = nblk * GROUPS_PER_BLOCK
    xr = xn[..., 0, :].reshape(bsz, g_total, S5_STATES)
    xi = xn[..., 1, :].reshape(bsz, g_total, S5_STATES)
    return y, xr, xi


def _block_bands(blk_re, blk_im):
    g = blk_re.shape[0]
    nblk = g // GROUPS_PER_BLOCK
    assert LANES == 2 * S5_STATES
    shape = (nblk, S5_TILES, 2, S5_GROUP, S5_STATES)
    pair = jnp.stack([blk_re.reshape(shape), blk_im.reshape(shape)], axis=2)
    zero = jnp.zeros(pair.shape[:3] + (S5_GROUP, S5_STATES), F32)
    band = jnp.concatenate([jnp.concatenate([pair[:, :, :, 0], zero], axis=-1),
                            jnp.concatenate([zero, pair[:, :, :, 1]], axis=-1)], axis=-2)
    full = jnp.einsum('ntirl,st->ntisrl', band, jnp.eye(S5_TILES, dtype=F32))
    return full.reshape(nblk, 2 * S5_TILES, S5_COLS, LANES).astype(BF16)


def _s5_block_weights(a_re, a_im, bb_re, bb_im, c_re, c_im):
    g, p = a_re.shape
    nblk = g // GROUPS_PER_BLOCK
    bb = _block_bands(bb_re.reshape(g, S5_GROUP, p), bb_im.reshape(g, S5_GROUP, p))
    cc = jnp.swapaxes(_block_bands(c_re, -c_im), 2, 3)
    a_blk = _tile_major(a_re.reshape(nblk, S5_HALF), a_im.reshape(nblk, S5_HALF)).reshape(nblk, 1, 2 * S5_HALF)
    return bb, cc, a_blk


def kernel(x_prompt, x_sample, state_mlstm_C, state_mlstm_n, state_mlstm_m, state_mlstm_conv, state_ret_S, state_s5_re, state_s5_im, norm_ffn1, norm_mix, norm_ffn2, norm_final, ffn1_w_gate, ffn1_w_up, ffn1_w_down, ffn2_w_gate, ffn2_w_up, ffn2_w_down, ab_w_in, mlstm_b_i, mlstm_b_f, mlstm_conv_w, mlstm_conv_b, mlstm_norm, ret_norm, ab_w_out, s5_w_in, s5_lambda_re, s5_lambda_im, s5_log_dt, s5_B_re, s5_B_im, s5_C_re, s5_C_im, s5_D, s5_w_glu, s5_b_glu):
    bp, lp, d = x_prompt.shape
    bs, ls, _ = x_sample.shape
    mp, ms = bp * lp, bs * ls
    depth = norm_ffn1.shape[0]
    assert mp % TOKEN_TILE == 0 and ms % TOKEN_TILE == 0

    ffn = functools.partial(_ffn, tm=TOKEN_TILE, tf=FFN_TILE)
    pos_p = jnp.arange(lp, dtype=F32)
    pos_s = PAST_LEN + jnp.arange(ls, dtype=F32)

    ys = (x_prompt.reshape(mp, d), x_sample.reshape(ms, d))
    out_mc, out_mn, out_mm, out_conv, out_rs, out_re, out_im = ([] for _ in range(7))
    for layer in range(depth):
        y = ffn(ys, norm_ffn1, ffn1_w_gate, ffn1_w_up, ffn1_w_down, layer)
        if layer % 2 == 0:
            e = layer // 2
            n_gate = 2 * N_HEADS
            wt = jnp.swapaxes(ab_w_in, 1, 2)[e]
            z, zg = _ab_proj(y, norm_mix, wt, layer, tm=AB_PROJ_ROWS, tn=PROJ_TILE)
            gate_bias = jnp.zeros((1, GATE_COLS), F32).at[0, :N_HEADS].set(mlstm_b_i[e]).at[
                0, N_HEADS:n_gate].set(mlstm_b_f[e])
            common = (mlstm_conv_w[e], mlstm_conv_b[e], gate_bias, mlstm_norm[e], ret_norm[e])
            hp, conv_p, c_p, n_p, m_p, s_p = _mix_ab(
                z, zg, pos_p, jnp.zeros((bp, CONV_W - 1, 2 * W_HEADS), F32),
                jnp.zeros((bp, N_HEADS, D_HEAD, D_HEAD), F32), jnp.zeros((bp, N_HEADS, D_HEAD), F32),
                jnp.full((bp, N_HEADS), M_INIT, F32), jnp.zeros((bp, N_HEADS, D_HEAD, D_HEAD), F32),
                *common, bsz=bp, length=lp, row0=0)
            hs, conv_s, c_s, n_s, m_s, s_s = _mix_ab(
                z, zg, pos_s, state_mlstm_conv[e], state_mlstm_C[e], state_mlstm_n[e], state_mlstm_m[e],
                state_ret_S[e], *common, bsz=bs, length=ls, row0=mp, nb=SAMPLE_BATCH_TILE)
            out_mc.append((c_p, c_s))
            out_mn.append((n_p, n_s))
            out_mm.append((m_p, m_s))
            out_conv.append((conv_p, conv_s))
            out_rs.append((s_p, s_s))
            y = _proj_res((hp, hs), ab_w_out[e], y, tm=TOKEN_TILE // 2)
        else:
            o = layer // 2
            u = _norm_proj(y, norm_mix, s5_w_in[o], layer, tm=TOKEN_TILE // 2)
            a_re, a_im, bb_re, bb_im = _s5_prep(s5_lambda_re[o], s5_lambda_im[o], s5_log_dt[o],
                                                jnp.swapaxes(s5_B_re[o], 1, 2), jnp.swapaxes(s5_B_im[o], 1, 2))
            bb, cc, a_blk = _s5_block_weights(a_re, a_im, bb_re, bb_im, s5_C_re[o], s5_C_im[o])
            zeros_p = jnp.zeros((bp,) + a_re.shape, F32)
            ya_p, re_p, im_p = _mix_s5(u, zeros_p, zeros_p, bb, cc, a_blk, s5_D[o], bsz=bp, length=lp,
                                       tt=S5_TIME_TILE, row0=0, interleave=True)
            u_s = u[mp:].reshape(bs, ls, -1).transpose(1, 0, 2).reshape(ms, -1)
            ya_s, re_s, im_s = _mix_s5(u_s, state_s5_re[o], state_s5_im[o], bb, cc, a_blk, s5_D[o], bsz=bs,
                                       length=ls, tt=ls, row0=0, interleave=False)
            ya_s = ya_s.reshape(ls, bs, -1).transpose(1, 0, 2).reshape(ms, -1)
            out_re.append((re_p, re_s))
            out_im.append((im_p, im_s))
            y = _glu((ya_p.reshape(mp, -1), ya_s), s5_w_glu, s5_b_glu, y, o, tm=TOKEN_TILE, tn=PROJ_TILE)
        ys = (ffn((y,), norm_ffn2, ffn2_w_gate, ffn2_w_up, ffn2_w_down, layer),)
    y = ys[0]
    y_p = _final_norm(y, norm_final, row0=0, m=mp, tm=TOKEN_TILE)
    y_s = _final_norm(y, norm_final, row0=mp, m=ms, tm=TOKEN_TILE)

    def both(pairs):
        return jnp.stack([p for p, _ in pairs]), jnp.stack([s for _, s in pairs])

    pc, sc = both(out_mc)
    pn, sn = both(out_mn)
    pm, sm = both(out_mm)
    pconv, sconv = both(out_conv)
    ps, ss = both(out_rs)
    pre, sre = both(out_re)
    pim, sim = both(out_im)
    return (y_p.reshape(bp, lp, d), y_s.reshape(bs, ls, d), pc, sc, pn, sn, pm, sm, pconv, sconv, ps, ss,
            pre, sre, pim, sim)
```
